```python
import math
import jax
import jax.numpy as jnp
from jax import lax
import numpy as np

D_MODEL = 1024
BATCH = 8
SEQ = 8192
DEPTH = 2

GRID_W = 64
CTX_LEN = 256
EPS = 1e-6
DN_HEADS = 4
DN_DK = 128
DN_DV = 128
DN_KEY = DN_HEADS * DN_DK
DN_VAL = DN_HEADS * DN_DV
DN_CONV = 3
CHUNK = 64
SC_WIDTH = 512
SC_CONV = 3
FN_GROUPS = 4
FN_GROUP_DIM = 128
FN_WIDTH = FN_GROUPS * FN_GROUP_DIM
N_BRANCH = 3
D_FF = ((8 * D_MODEL // 3 + 255) // 256) * 256
IN_SIZES = (2 * DN_KEY + DN_VAL, DN_VAL, 4 * DN_HEADS, 3 * SC_WIDTH, FN_WIDTH)
N_IN = 2 * DN_KEY + DN_VAL + DN_VAL + 4 * DN_HEADS + 3 * SC_WIDTH + FN_WIDTH

kernel_name = "hybrid_gdn_shortconv_fnet_dit"


def split_cols(p, sizes):
    out = []
    start = 0
    for s in sizes:
        out.append(p[..., start:start + s])
        start += s
    return out


def rmsnorm(t, gain):
    tf = t.astype(jnp.float32)
    y = tf * lax.rsqrt(jnp.mean(tf * tf, axis=-1, keepdims=True) + EPS) * gain.astype(jnp.float32)
    return y.astype(t.dtype)


def modulate(t, gain, shift, scale):
    return rmsnorm(t, gain) * (1 + scale) + shift


def l2norm(t):
    tf = t.astype(jnp.float32)
    return tf * lax.rsqrt(jnp.sum(tf * tf, axis=-1, keepdims=True) + EPS)


def dwconv(u, w):
    k = w.shape[0]
    return lax.conv_general_dilated(
        u, w[:, None, :].astype(u.dtype), window_strides=(1,), padding=[(k // 2, k // 2)],
        dimension_numbers=("NWC", "WIO", "NWC"), feature_group_count=u.shape[-1])


def conv_seq(u, w):
    return dwconv(u, w)


def conv_grid_rows(u, w):
    bsz, length, ch = u.shape
    rows = length // GRID_W
    return dwconv(u.reshape(bsz * rows, GRID_W, ch), w).reshape(bsz, length, ch)


def gated_delta_chunked(q, k, v, beta, g, s0):
    bsz, length, heads, _ = q.shape
    dv = v.shape[-1]
    n = length // CHUNK

    def blocks(t):
        t = t.astype(jnp.float32).reshape((bsz, n, CHUNK, heads) + t.shape[3:])
        return jnp.moveaxis(t, 3, 1)

    q, k, v, beta, g = (blocks(t) for t in (q, k, v, beta, g))
    g = jnp.cumsum(g, axis=-1)
    idx = jnp.arange(CHUNK)
    lower = idx[:, None] >= idx[None, :]
    strict = idx[:, None] > idx[None, :]
    decay = jnp.exp(jnp.where(lower, g[..., :, None] - g[..., None, :], -jnp.inf))
    kb = k * beta[..., None]
    a = jnp.where(strict, jnp.einsum("bhnid,bhnjd->bhnij", kb, k) * decay, 0.0)
    eye = jnp.eye(CHUNK, dtype=jnp.float32)
    t_inv = lax.linalg.triangular_solve(a + eye, jnp.broadcast_to(eye, a.shape), left_side=True,
                                        lower=True, unit_diagonal=True)
    u = t_inv @ (v * beta[..., None])
    w = t_inv @ (kb * jnp.exp(g)[..., None])
    attn = jnp.einsum("bhnid,bhnjd->bhnij", q, k) * decay
    qg = q * jnp.exp(g)[..., None]
    g_last = g[..., -1]
    kd = k * jnp.exp(g_last[..., None] - g)[..., None]

    def step(s, inp):
        u_i, w_i, qg_i, kd_i, attn_i, gl_i = inp
        v_new = u_i - jnp.einsum("bhcd,bhde->bhce", w_i, s)
        o_i = jnp.einsum("bhcd,bhde->bhce", qg_i, s) + jnp.einsum("bhij,bhje->bhie", attn_i, v_new)
        s = s * jnp.exp(gl_i)[..., None, None] + jnp.einsum("bhcd,bhce->bhde", kd_i, v_new)
        return s, o_i

    xs = tuple(jnp.moveaxis(t, 2, 0) for t in (u, w, qg, kd, attn, g_last))
    s_final, o = lax.scan(step, s0, xs)
    o = jnp.transpose(o, (1, 0, 3, 2, 4)).reshape(bsz, length, heads, dv)
    return o, s_final


def deltanet_prep(p_qkv, p_ba, conv_fn, conv_w, a_log, dt_bias):
    qkv = jax.nn.silu(conv_fn(p_qkv, conv_w))
    bsz, length, _ = qkv.shape
    q, k, v = split_cols(qkv, (DN_KEY, DN_KEY, DN_VAL))
    q = l2norm(q.reshape(bsz, length, DN_HEADS, DN_DK)) * (DN_DK ** -0.5)
    k = l2norm(k.reshape(bsz, length, DN_HEADS, DN_DK))
    v = v.reshape(bsz, length, DN_HEADS, DN_DV)
    ba = p_ba.astype(jnp.float32)
    beta = jax.nn.sigmoid(ba[..., :2 * DN_HEADS]).reshape(bsz, length, 2, DN_HEADS)
    alpha = ba[..., 2 * DN_HEADS:].reshape(bsz, length, 2, DN_HEADS)
    g = -jnp.exp(a_log.astype(jnp.float32)) * jax.nn.softplus(alpha + dt_bias.astype(jnp.float32))
    return q, k, v, beta, g


def bidir_gated_delta(dn_ctx, dn_lat):
    qc, kc, vc, bc, gc = dn_ctx
    ql, kl, vl, bl, gl = dn_lat
    s0 = jnp.zeros((qc.shape[0], DN_HEADS, DN_DK, DN_DV), jnp.float32)
    outs_c = []
    outs_l = []
    for d in range(2):
        rev = (lambda t: jnp.flip(t, axis=1)) if d == 1 else (lambda t: t)
        oc, s_ctx = gated_delta_chunked(rev(qc), rev(kc), rev(vc), rev(bc[:, :, d]), rev(gc[:, :, d]), s0)
        ol, _ = gated_delta_chunked(rev(ql), rev(kl), rev(vl), rev(bl[:, :, d]), rev(gl[:, :, d]), s_ctx)
        outs_c.append(rev(oc))
        outs_l.append(rev(ol))
    return (outs_c[0] + outs_c[1]).astype(vc.dtype), (outs_l[0] + outs_l[1]).astype(vl.dtype)


def fourier_mix(u):
    bsz, length, _ = u.shape
    uf = u.astype(jnp.float32).reshape(bsz, length, FN_GROUPS, FN_GROUP_DIM)
    y = jnp.fft.fftn(uf, axes=(1, 3), norm="ortho").real
    return y.reshape(bsz, length, FN_WIDTH).astype(u.dtype)


def branch_merge(h, o_dn, z, sc_p, fn_u, conv_fn, onorm_g, sc_conv_w, w_dn_out, w_sc_out, w_fn_out,
                 w_gate, b_gate, w_o):
    bsz, length, _ = h.shape
    zf = jax.nn.silu(z.reshape(bsz, length, DN_HEADS, DN_DV))
    y_dn = (rmsnorm(o_dn, onorm_g) * zf).reshape(bsz, length, DN_VAL) @ w_dn_out
    sb, scc, sx = split_cols(sc_p, (SC_WIDTH, SC_WIDTH, SC_WIDTH))
    y_sc = (sb * conv_fn(scc * sx, sc_conv_w)) @ w_sc_out
    y_fn = fourier_mix(fn_u) @ w_fn_out
    gates = jax.nn.sigmoid((h @ w_gate + b_gate).astype(jnp.float32)).astype(h.dtype)
    gates = gates.reshape(bsz, length, N_BRANCH, D_MODEL)
    merged = gates[:, :, 0] * y_dn + gates[:, :, 1] * y_sc + gates[:, :, 2] * y_fn
    return merged @ w_o


def swiglu(h, w_ffn_in, w_ffn_out):
    a, b = split_cols(h @ w_ffn_in, (D_FF, D_FF))
    return (jax.nn.silu(a) * b) @ w_ffn_out


def setup_inputs(seed: int = 0) -> dict:
    key = jax.random.key(seed)
    ks = jax.random.split(key, 24)
    f32 = jnp.float32

    def nrm(k, shape, fan_in, s=1.0):
        return jax.random.normal(k, shape, f32) * (s * fan_in ** -0.5)

    def gain(k, shape):
        return 1.0 + 0.02 * jax.random.normal(k, shape, f32)

    dt = jnp.exp(jax.random.uniform(ks[10], (DEPTH, 2, DN_HEADS), f32, math.log(1e-3), math.log(1e-1)))
    return {
        "x": jax.random.normal(ks[0], (BATCH, SEQ, D_MODEL), f32),
        "c": jax.random.normal(ks[1], (BATCH, D_MODEL), f32),
        "ctx": jax.random.normal(ks[2], (BATCH, CTX_LEN, D_MODEL), f32),
        "c_ctx": jax.random.normal(ks[3], (D_MODEL,), f32),
        "w_mod": nrm(ks[4], (DEPTH, D_MODEL, 6 * D_MODEL), D_MODEL, 0.5),
        "b_mod": 0.02 * jax.random.normal(ks[5], (DEPTH, 6 * D_MODEL), f32),
        "norm1_g": gain(ks[6], (DEPTH, D_MODEL)),
        "w_in": nrm(ks[7], (DEPTH, D_MODEL, N_IN), D_MODEL),
        "dn_conv_w": nrm(ks[8], (DEPTH, DN_CONV, 2 * DN_KEY + DN_VAL), DN_CONV),
        "dn_a_log": jnp.log(jax.random.uniform(ks[9], (DEPTH, 2, DN_HEADS), f32, 1.0, 16.0)),
        "dn_dt_bias": jnp.log(jnp.expm1(dt)),
        "dn_onorm_g": gain(ks[11], (DEPTH, DN_DV)),
        "w_dn_out": nrm(ks[12], (DEPTH, DN_VAL, D_MODEL), DN_VAL),
        "sc_conv_w": nrm(ks[13], (DEPTH, SC_CONV, SC_WIDTH), SC_CONV),
        "w_sc_out": nrm(ks[14], (DEPTH, SC_WIDTH, D_MODEL), SC_WIDTH),
        "w_fn_out": nrm(ks[15], (DEPTH, FN_WIDTH, D_MODEL), FN_WIDTH),
        "w_gate": nrm(ks[16], (DEPTH, D_MODEL, N_BRANCH * D_MODEL), D_MODEL),
        "b_gate": 0.02 * jax.random.normal(ks[17], (DEPTH, N_BRANCH * D_MODEL), f32),
        "w_o": nrm(ks[18], (DEPTH, D_MODEL, D_MODEL), D_MODEL),
        "norm2_g": gain(ks[19], (DEPTH, D_MODEL)),
        "w_ffn_in": nrm(ks[20], (DEPTH, D_MODEL, 2 * D_FF), D_MODEL),
        "w_ffn_out": nrm(ks[21], (DEPTH, D_FF, D_MODEL), D_FF),
        "final_g": gain(ks[22], (D_MODEL,)),
    }


def reference(x, c, ctx, c_ctx, w_mod, b_mod, norm1_g, w_in, dn_conv_w, dn_a_log, dn_dt_bias,
              dn_onorm_g, w_dn_out, sc_conv_w, w_sc_out, w_fn_out, w_gate, b_gate, w_o, norm2_g,
              w_ffn_in, w_ffn_out, final_g):
    xc = ctx
    for layer in range(DEPTH):
        last = layer == DEPTH - 1
        mod = jax.nn.silu(c) @ w_mod[layer] + b_mod[layer]
        sh1, sc1, g1, sh2, sc2, g2 = jnp.split(mod[:, None, :], 6, axis=-1)
        mod_c = jax.nn.silu(c_ctx) @ w_mod[layer] + b_mod[layer]
        csh1, csc1, cg1, csh2, csc2, cg2 = jnp.split(mod_c, 6)

        h = modulate(x, norm1_g[layer], sh1, sc1)
        hc = modulate(xc, norm1_g[layer], csh1, csc1)
        qkv, z, ba, sc_p, fn_u = split_cols(h @ w_in[layer], IN_SIZES)
        qkv_c, z_c, ba_c, sc_pc, fn_uc = split_cols(hc @ w_in[layer], IN_SIZES)
        dn_lat = deltanet_prep(qkv, ba, conv_grid_rows, dn_conv_w[layer], dn_a_log[layer], dn_dt_bias[layer])
        dn_ctx = deltanet_prep(qkv_c, ba_c, conv_seq, dn_conv_w[layer], dn_a_log[layer], dn_dt_bias[layer])
        o_ctx, o_lat = bidir_gated_delta(dn_ctx, dn_lat)
        mix = branch_merge(h, o_lat, z, sc_p, fn_u, conv_grid_rows, dn_onorm_g[layer], sc_conv_w[layer],
                           w_dn_out[layer], w_sc_out[layer], w_fn_out[layer], w_gate[layer],
                           b_gate[layer], w_o[layer])
        x = x + g1 * mix
        if not last:
            mix_c = branch_merge(hc, o_ctx, z_c, sc_pc, fn_uc, conv_seq, dn_onorm_g[layer],
                                 sc_conv_w[layer], w_dn_out[layer], w_sc_out[layer], w_fn_out[layer],
                                 w_gate[layer], b_gate[layer], w_o[layer])
            xc = xc + cg1 * mix_c
            xc = xc + cg2 * swiglu(modulate(xc, norm2_g[layer], csh2, csc2), w_ffn_in[layer], w_ffn_out[layer])

        x = x + g2 * swiglu(modulate(x, norm2_g[layer], sh2, sc2), w_ffn_in[layer], w_ffn_out[layer])
    return rmsnorm(x, final_g)
```

```python
import functools
import math

import numpy as np
import jax
import jax.numpy as jnp
from jax import lax
from jax.experimental import pallas as pl
from jax.experimental.pallas import tpu as pltpu

F32 = jnp.float32
BF16 = jnp.bfloat16
HIGHEST = lax.Precision.HIGHEST

EPS = 1e-6
D_MODEL = 1024
DEPTH = 2
HEADS = 4
HEAD_DIM = 128
KEYW = HEADS * HEAD_DIM
CHUNK = 64
D_FF = 2816
TILE = 256
N_DH = 2 * HEADS
VMEM_LIMIT = 56 * 1024 * 1024


def _cparams(n_axes):
    return pltpu.CompilerParams(dimension_semantics=("arbitrary",) * n_axes, vmem_limit_bytes=VMEM_LIMIT)


def _dot(a, b):
    return jnp.dot(a, b, preferred_element_type=F32)


def _dot_hi(a, b):
    return jnp.dot(a, b, preferred_element_type=F32, precision=HIGHEST)


def _dot_nt(a, b):
    return lax.dot_general(a, b, (((1,), (1,)), ((), ())), preferred_element_type=F32)


def _dot_tn(a, b):
    return lax.dot_general(a, b, (((0,), (0,)), ((), ())), preferred_element_type=F32)


def _silu(t):
    return t * jax.nn.sigmoid(t)


def _rms(t, axis_size):
    return t * lax.rsqrt(jnp.sum(t * t, axis=-1, keepdims=True) * (1.0 / axis_size) + EPS)


def _mod_kernel(c_ref, w_ref, b_ref, o_ref):
    o_ref[...] = _dot_hi(_silu(c_ref[...]), w_ref[...]) + b_ref[...]


def _mod_call(c_rows, w_mod, b_mod):
    depth, d, n = w_mod.shape
    rows = c_rows.shape[0]
    tn = 1536
    return pl.pallas_call(
        _mod_kernel,
        grid=(depth, n // tn),
        in_specs=[
            pl.BlockSpec((rows, d), lambda l, j: (0, 0)),
            pl.BlockSpec((None, d, tn), lambda l, j: (l, 0, j)),
            pl.BlockSpec((None, 1, tn), lambda l, j: (l, 0, j)),
        ],
        out_specs=pl.BlockSpec((None, rows, tn), lambda l, j: (l, 0, j)),
        out_shape=jax.ShapeDtypeStruct((depth, rows, n), F32),
        compiler_params=_cparams(2),
        name="mod_vectors",
    )(c_rows, w_mod, b_mod.reshape(depth, 1, n))


def _row_masks(is_lat):
    pos = lax.broadcasted_iota(jnp.int32, (TILE, 1), 0)
    row_mask = jnp.where(is_lat, CHUNK - 1, TILE - 1)
    in_row = pos & row_mask
    return jnp.where(in_row == 0, 0.0, 1.0).astype(F32), jnp.where(in_row == row_mask, 0.0, 1.0).astype(F32)


def _conv3(t, w_ref, m_prev, m_next):
    prev = pltpu.roll(t, 1, 0) * m_prev
    nxt = pltpu.roll(t, TILE - 1, 0) * m_next
    return prev * w_ref[0:1, :] + t * w_ref[1:2, :] + nxt * w_ref[2:3, :]


def _modulated(x, g_ref, shift, scale):
    return _rms(x, D_MODEL) * g_ref[...] * (1.0 + scale) + shift


def _inproj_kernel(nt_lat, x_ref, mod_ref, g_ref, wqkv_ref, wz_ref, wba_ref, wsc_ref, wfn_ref,
                   dnconv_ref, scconv_ref, alog_ref, dtb_ref,
                   q_ref, k_ref, v_ref, z_ref, bg_ref, sc_ref, fn_ref):
    is_lat = pl.program_id(1) < nt_lat
    m_prev, m_next = _row_masks(is_lat)
    h = _modulated(x_ref[...], g_ref, mod_ref[0:1, :], mod_ref[1:2, :]).astype(BF16)

    qkv = _silu(_conv3(_dot(h, wqkv_ref[...]), dnconv_ref, m_prev, m_next))
    for hd in range(HEADS):
        lo = hd * HEAD_DIM
        qh = qkv[:, lo:lo + HEAD_DIM]
        kh = qkv[:, KEYW + lo:KEYW + lo + HEAD_DIM]
        q_ref[:, lo:lo + HEAD_DIM] = qh * lax.rsqrt(jnp.sum(qh * qh, axis=-1, keepdims=True) + EPS) * (HEAD_DIM ** -0.5)
        k_ref[:, lo:lo + HEAD_DIM] = kh * lax.rsqrt(jnp.sum(kh * kh, axis=-1, keepdims=True) + EPS)
    v_ref[...] = qkv[:, 2 * KEYW:]
    z_ref[...] = _dot(h, wz_ref[...])

    ba = _dot(h, wba_ref[...])
    sp_in = ba + dtb_ref[...]
    softplus = jnp.maximum(sp_in, 0.0) + jnp.log1p(jnp.exp(-jnp.abs(sp_in)))
    lane = lax.broadcasted_iota(jnp.int32, ba.shape, 1)
    bg_ref[...] = jnp.where(lane < N_DH, jax.nn.sigmoid(ba), -jnp.exp(alog_ref[...]) * softplus)

    scp = _dot(h, wsc_ref[...])
    sc_ref[...] = scp[:, :KEYW] * _conv3(scp[:, KEYW:2 * KEYW] * scp[:, 2 * KEYW:], scconv_ref, m_prev, m_next)
    fn_ref[...] = _dot(h, wfn_ref[...])


def _mod_row_map(nt_lat, ctx_row):
    return lambda b, j: (jnp.where(j < nt_lat, b, ctx_row), 0, 0)


def _inproj_call(x_cat, mod_l, n1g, wparts, dnconv, scconv, alog_row, dtb_row, nt_lat):
    bsz, ltot, d = x_cat.shape
    nt = ltot // TILE
    wqkv, wz, wba, wsc, wfn = wparts
    ctx_row = bsz
    tok = lambda w: pl.BlockSpec((None, TILE, w), lambda b, j: (b, j, 0))
    const = lambda a: pl.BlockSpec(a.shape, lambda b, j: (0,) * a.ndim)
    out_w = (KEYW, KEYW, KEYW, KEYW, HEAD_DIM, KEYW, KEYW)
    return pl.pallas_call(
        functools.partial(_inproj_kernel, nt_lat),
        grid=(bsz, nt),
        in_specs=[tok(d), pl.BlockSpec((None, 6, d), _mod_row_map(nt_lat, ctx_row)), const(n1g),
                  const(wqkv), const(wz), const(wba), const(wsc), const(wfn),
                  const(dnconv), const(scconv), const(alog_row), const(dtb_row)],
        out_specs=[tok(w) for w in out_w],
        out_shape=[jax.ShapeDtypeStruct((bsz, ltot, w), F32) for w in out_w],
        compiler_params=_cparams(2),
        name="in_proj",
    )(x_cat, mod_l, n1g, wqkv, wz, wba, wsc, wfn, dnconv, scconv, alog_row, dtb_row)


def _dnprep_kernel(q_ref, k_ref, v_ref, bg_ref, u_ref, w_ref, qg_ref, kd_ref, attn_ref, adec_ref):
    bg = bg_ref[...]
    bg_t = bg.T
    ri = lax.broadcasted_iota(jnp.int32, (CHUNK, CHUNK), 0)
    ci = lax.broadcasted_iota(jnp.int32, (CHUNK, CHUNK), 1)
    low = ri >= ci
    upp = ri <= ci
    low_f = jnp.where(low, 1.0, 0.0).astype(F32)
    upp_f = jnp.where(upp, 1.0, 0.0).astype(F32)
    eye = jnp.where(ri == ci, 1.0, 0.0).astype(F32)
    gc_col = (_dot_hi(low_f, bg), _dot_hi(upp_f, bg))
    gc_row = (_dot_hi(bg_t, upp_f), _dot_hi(bg_t, low_f))

    for hd in range(HEADS):
        lo = hd * HEAD_DIM
        qh = q_ref[:, lo:lo + HEAD_DIM]
        kh = k_ref[:, lo:lo + HEAD_DIM]
        vh = v_ref[:, lo:lo + HEAD_DIM]
        kh_b = kh.astype(BF16)
        kk = _dot_nt(kh_b, kh_b)
        qk = _dot_nt(qh.astype(BF16), kh_b)
        for dr in range(2):
            unit = dr * HEADS + hd
            incl = low if dr == 0 else upp
            strict = (ri > ci) if dr == 0 else (ri < ci)
            gcol = gc_col[dr][:, N_DH + unit:N_DH + unit + 1]
            grow = gc_row[dr][N_DH + unit:N_DH + unit + 1, :]
            beta = bg[:, unit:unit + 1]
            decay = jnp.exp(jnp.where(incl, gcol - grow, -jnp.inf))
            neg_a = jnp.where(strict, -(beta * kk * decay), 0.0)
            p_acc = eye + neg_a
            q_pow = _dot_hi(neg_a, neg_a)
            for _ in range(4):
                both = _dot_hi(jnp.concatenate([q_pow, p_acc], axis=0), q_pow)
                q_pow = both[:CHUNK]
                p_acc = p_acc + both[CHUNK:]
            p_acc = p_acc + _dot_hi(p_acc, q_pow)
            eg = jnp.exp(gcol)
            rhs = jnp.concatenate([vh * beta, kh * (beta * eg)], axis=1).astype(BF16)
            uw = _dot(p_acc.astype(BF16), rhs)
            g_last = gcol[CHUNK - 1:CHUNK, :] if dr == 0 else gcol[0:1, :]
            u_ref[dr, :, lo:lo + HEAD_DIM] = uw[:, :HEAD_DIM]
            w_ref[dr, :, lo:lo + HEAD_DIM] = uw[:, HEAD_DIM:].astype(BF16)
            qg_ref[dr, :, lo:lo + HEAD_DIM] = (qh * eg).astype(BF16)
            kd_ref[dr, :, lo:lo + HEAD_DIM] = (kh * jnp.exp(g_last - gcol)).astype(BF16)
            attn_ref[dr, :, hd * CHUNK:(hd + 1) * CHUNK] = (qk * decay).astype(BF16)
            adec_ref[unit:unit + 1, :] = jnp.broadcast_to(jnp.exp(g_last), (1, HEAD_DIM))


def _dnprep_call(q, k, v, bg):
    bsz, ltot, _ = q.shape
    nch = ltot // CHUNK
    tok = lambda w: pl.BlockSpec((None, CHUNK, w), lambda b, n: (b, n, 0))
    dtok = lambda w: pl.BlockSpec((None, 2, CHUNK, w), lambda b, n: (b, 0, n, 0))
    return pl.pallas_call(
        _dnprep_kernel,
        grid=(bsz, nch),
        in_specs=[tok(KEYW), tok(KEYW), tok(KEYW), tok(HEAD_DIM)],
        out_specs=[dtok(KEYW), dtok(KEYW), dtok(KEYW), dtok(KEYW), dtok(HEADS * CHUNK),
                   pl.BlockSpec((None, None, N_DH, HEAD_DIM), lambda b, n: (b, n, 0, 0))],
        out_shape=[jax.ShapeDtypeStruct((bsz, 2, ltot, KEYW), F32),
                   jax.ShapeDtypeStruct((bsz, 2, ltot, KEYW), BF16),
                   jax.ShapeDtypeStruct((bsz, 2, ltot, KEYW), BF16),
                   jax.ShapeDtypeStruct((bsz, 2, ltot, KEYW), BF16),
                   jax.ShapeDtypeStruct((bsz, 2, ltot, HEADS * CHUNK), BF16),
                   jax.ShapeDtypeStruct((bsz, nch, N_DH, HEAD_DIM), F32)],
        compiler_params=_cparams(2),
        name="dn_prep",
    )(q, k, v, bg)


def _dnscan_kernel(uf_ref, wf_ref, qgf_ref, kdf_ref, af_ref, df_ref,
                   ub_ref, wb_ref, qgb_ref, kdb_ref, ab_ref, db_ref,
                   of_ref, ob_ref, s_ref):
    @pl.when(pl.program_id(1) == 0)
    def _():
        s_ref[...] = jnp.zeros_like(s_ref)

    per_dir = ((uf_ref, wf_ref, qgf_ref, kdf_ref, af_ref, df_ref, of_ref),
               (ub_ref, wb_ref, qgb_ref, kdb_ref, ab_ref, db_ref, ob_ref))
    for dr, (u_ref, w_ref, qg_ref, kd_ref, a_ref, d_ref, o_ref) in enumerate(per_dir):
        for hd in range(HEADS):
            unit = dr * HEADS + hd
            lo = hd * HEAD_DIM
            s = s_ref[unit]
            s_b = s.astype(BF16)
            both = _dot(jnp.concatenate([w_ref[:, lo:lo + HEAD_DIM], qg_ref[:, lo:lo + HEAD_DIM]], axis=0), s_b)
            v_new = u_ref[:, lo:lo + HEAD_DIM] - both[:CHUNK]
            v_new_b = v_new.astype(BF16)
            o_ref[:, lo:lo + HEAD_DIM] = both[CHUNK:] + _dot(a_ref[:, hd * CHUNK:(hd + 1) * CHUNK], v_new_b)
            s_ref[unit] = s * d_ref[unit:unit + 1, :] + _dot_tn(kd_ref[:, lo:lo + HEAD_DIM], v_new_b)


def _dnscan_call(u, w, qg, kd, attn, adec, n_lat_chunks):
    bsz, _, ltot, _ = u.shape
    nch = ltot // CHUNK
    fwd = lambda n: (n + n_lat_chunks) % nch
    bwd = lambda n: nch - 1 - n

    def dspec(wd, dr, order):
        return pl.BlockSpec((None, None, CHUNK, wd), lambda b, n: (b, dr, order(n), 0))

    def aspec(order):
        return pl.BlockSpec((None, None, N_DH, HEAD_DIM), lambda b, n: (b, order(n), 0, 0))

    def ospec(order):
        return pl.BlockSpec((None, CHUNK, KEYW), lambda b, n: (b, order(n), 0))

    in_specs = []
    for dr, order in ((0, fwd), (1, bwd)):
        in_specs += [dspec(KEYW, dr, order), dspec(KEYW, dr, order), dspec(KEYW, dr, order), dspec(KEYW, dr, order),
                     dspec(HEADS * CHUNK, dr, order), aspec(order)]
    return pl.pallas_call(
        _dnscan_kernel,
        grid=(bsz, nch),
        in_specs=in_specs,
        out_specs=[ospec(fwd), ospec(bwd)],
        out_shape=[jax.ShapeDtypeStruct((bsz, ltot, KEYW), F32)] * 2,
        scratch_shapes=[pltpu.VMEM((N_DH, HEAD_DIM, HEAD_DIM), F32)],
        compiler_params=_cparams(2),
        name="dn_scan",
    )(u, w, qg, kd, attn, adec, u, w, qg, kd, attn, adec)


def _dft_mats(n):
    idx = np.arange(n)
    ang = 2.0 * np.pi * ((idx[:, None] * idx[None, :]) % n) / n
    return np.cos(ang), np.sin(ang)


def _fft1_kernel(n2blk, u_ref, f1_ref, tc_ref, ts_ref, br_ref, bi_ref):
    l1 = f1_ref.shape[1]
    a = _dot_hi(f1_ref[...], u_ref[...])
    ar, ai = a[:l1], a[l1:]
    for jj in range(n2blk):
        sl = slice(jj * KEYW, (jj + 1) * KEYW)
        tc = jnp.concatenate([tc_ref[jj]] * (KEYW // HEAD_DIM), axis=1)
        ts = jnp.concatenate([ts_ref[jj]] * (KEYW // HEAD_DIM), axis=1)
        br_ref[:, sl] = ar[:, sl] * tc + ai[:, sl] * ts
        bi_ref[:, sl] = ai[:, sl] * tc - ar[:, sl] * ts


def _fft2_kernel(k1blk, nlen, br_ref, bi_ref, f2_ref, wc_ref, y_ref):
    for jj in range(k1blk):
        rows = slice(jj * nlen, (jj + 1) * nlen)
        x = _dot_hi(f2_ref[...], jnp.concatenate([br_ref[rows, :], bi_ref[rows, :]], axis=0))
        xr, xi = x[:nlen], x[nlen:]
        for g in range(KEYW // HEAD_DIM):
            gs = slice(g * HEAD_DIM, (g + 1) * HEAD_DIM)
            y_ref[:, jj * KEYW + g * HEAD_DIM:jj * KEYW + (g + 1) * HEAD_DIM] = _dot_hi(
                jnp.concatenate([xr[:, gs], xi[:, gs]], axis=1), wc_ref[...])


def _dftctx_kernel(nlen, u_ref, f_ref, wc_ref, y_ref):
    x = _dot_hi(f_ref[...], u_ref[...])
    xr, xi = x[:nlen], x[nlen:]
    for g in range(KEYW // HEAD_DIM):
        gs = slice(g * HEAD_DIM, (g + 1) * HEAD_DIM)
        y_ref[:, gs] = _dot_hi(jnp.concatenate([xr[:, gs], xi[:, gs]], axis=1), wc_ref[...])


def _chan_mat(total_len):
    cc, sc = _dft_mats(HEAD_DIM)
    return jnp.asarray(np.concatenate([cc, sc], axis=0) / math.sqrt(total_len * HEAD_DIM), F32)


def _fourier_lat_call(fn_cat, l_lat):
    bsz, ltot, _ = fn_cat.shape
    l1 = l_lat // CHUNK
    n2blk = 8
    k1blk = 8 if l1 % 8 == 0 else l1
    c1, s1 = _dft_mats(l1)
    f1 = jnp.asarray(np.concatenate([c1, -s1], axis=0), F32)
    n2 = np.arange(CHUNK)[:, None, None]
    k1 = np.arange(l1)[None, :, None]
    ang = 2.0 * np.pi * (n2 * k1) / l_lat * np.ones((1, 1, HEAD_DIM))
    tc, ts = jnp.asarray(np.cos(ang), F32), jnp.asarray(np.sin(ang), F32)
    c2, s2 = _dft_mats(CHUNK)
    f2 = jnp.asarray(np.block([[c2, s2], [-s2, c2]]), F32)
    wc = _chan_mat(l_lat)

    u_view = fn_cat.reshape(bsz, ltot // CHUNK, CHUNK * KEYW)
    bspec = pl.BlockSpec((None, l1, KEYW * n2blk), lambda b, i: (b, 0, i))
    br, bi = pl.pallas_call(
        functools.partial(_fft1_kernel, n2blk),
        grid=(bsz, CHUNK // n2blk),
        in_specs=[bspec,
                  pl.BlockSpec(f1.shape, lambda b, i: (0, 0)),
                  pl.BlockSpec((n2blk, l1, HEAD_DIM), lambda b, i: (i, 0, 0)),
                  pl.BlockSpec((n2blk, l1, HEAD_DIM), lambda b, i: (i, 0, 0))],
        out_specs=[bspec, bspec],
        out_shape=[jax.ShapeDtypeStruct((bsz, l1, CHUNK * KEYW), F32)] * 2,
        compiler_params=_cparams(2),
        name="fourier_stage1",
    )(u_view, f1, tc, ts)

    rspec = pl.BlockSpec((None, CHUNK * k1blk, KEYW), lambda b, i: (b, i, 0))
    y = pl.pallas_call(
        functools.partial(_fft2_kernel, k1blk, CHUNK),
        grid=(bsz, l1 // k1blk),
        in_specs=[rspec, rspec,
                  pl.BlockSpec(f2.shape, lambda b, i: (0, 0)),
                  pl.BlockSpec(wc.shape, lambda b, i: (0, 0))],
        out_specs=pl.BlockSpec((None, CHUNK, KEYW * k1blk), lambda b, i: (b, 0, i)),
        out_shape=jax.ShapeDtypeStruct((bsz, CHUNK, l1 * KEYW), F32),
        compiler_params=_cparams(2),
        name="fourier_stage2",
    )(br.reshape(bsz, l_lat, KEYW), bi.reshape(bsz, l_lat, KEYW), f2, wc)
    return y.reshape(bsz, l_lat, KEYW)


def _fourier_ctx_call(fn_cat, l_lat):
    bsz, ltot, _ = fn_cat.shape
    lc = ltot - l_lat
    c, s = _dft_mats(lc)
    f = jnp.asarray(np.concatenate([c, -s], axis=0), F32)
    wc = _chan_mat(lc)
    return pl.pallas_call(
        functools.partial(_dftctx_kernel, lc),
        grid=(bsz,),
        in_specs=[pl.BlockSpec((None, lc, KEYW), lambda b: (b, l_lat // lc, 0)),
                  pl.BlockSpec(f.shape, lambda b: (0, 0)),
                  pl.BlockSpec(wc.shape, lambda b: (0, 0))],
        out_specs=pl.BlockSpec((None, lc, KEYW), lambda b: (b, 0, 0)),
        out_shape=jax.ShapeDtypeStruct((bsz, lc, KEYW), F32),
        compiler_params=_cparams(1),
        name="fourier_ctx",
    )(fn_cat, f, wc)


def _merge_kernel(nt_lat, x_ref, mod_ref, g_ref, of_ref, ob_ref, z_ref, sc_ref, fl_ref, fc_ref,
                  wgate_ref, bgate_ref, og_ref, wdn_ref, wsc_ref, wfn_ref, wo_ref, o_ref):
    is_lat = pl.program_id(1) < nt_lat
    x = x_ref[...]
    h = _modulated(x, g_ref, mod_ref[0:1, :], mod_ref[1:2, :]).astype(BF16)
    gates = jax.nn.sigmoid(_dot(h, wgate_ref[...]) + bgate_ref[...])

    o = of_ref[...] + ob_ref[...]
    z = z_ref[...]
    dn_parts = []
    for hd in range(HEADS):
        sl = slice(hd * HEAD_DIM, (hd + 1) * HEAD_DIM)
        dn_parts.append(_rms(o[:, sl], HEAD_DIM) * og_ref[...] * _silu(z[:, sl]))
    y_dn = _dot(jnp.concatenate(dn_parts, axis=1).astype(BF16), wdn_ref[...])
    y_sc = _dot(sc_ref[...].astype(BF16), wsc_ref[...])
    fn = jnp.where(is_lat, fl_ref[...], fc_ref[...])
    y_fn = _dot(fn.astype(BF16), wfn_ref[...])
    merged = (gates[:, :D_MODEL] * y_dn + gates[:, D_MODEL:2 * D_MODEL] * y_sc + gates[:, 2 * D_MODEL:] * y_fn)
    o_ref[...] = x + mod_ref[2:3, :] * _dot(merged.astype(BF16), wo_ref[...])


def _merge_call(x_cat, mod_l, n1g, o_f, o_b, z, sc, fn_lat, fn_ctx, wgate, bgate, og, wdn, wsc, wfn, wo, nt_lat, nt_run):
    bsz, ltot, d = x_cat.shape
    tok = lambda w: pl.BlockSpec((None, TILE, w), lambda b, j: (b, j, 0))
    const = lambda a: pl.BlockSpec(a.shape, lambda b, j: (0,) * a.ndim)
    return pl.pallas_call(
        functools.partial(_merge_kernel, nt_lat),
        grid=(bsz, nt_run),
        in_specs=[tok(d), pl.BlockSpec((None, 6, d), _mod_row_map(nt_lat, bsz)), const(n1g),
                  tok(KEYW), tok(KEYW), tok(KEYW), tok(KEYW),
                  pl.BlockSpec((None, TILE, KEYW), lambda b, j: (b, jnp.minimum(j, nt_lat - 1), 0)),
                  pl.BlockSpec((None, TILE, KEYW), lambda b, j: (b, 0, 0)),
                  const(wgate), const(bgate), const(og), const(wdn), const(wsc), const(wfn), const(wo)],
        out_specs=tok(d),
        out_shape=jax.ShapeDtypeStruct((bsz, ltot, d), F32),
        compiler_params=_cparams(2),
        name="branch_merge",
    )(x_cat, mod_l, n1g, o_f, o_b, z, sc, fn_lat, fn_ctx, wgate, bgate, og, wdn, wsc, wfn, wo)


def _ffn_kernel(final, x_ref, mod_ref, g_ref, win_ref, wout_ref, fg_ref, o_ref):
    x = x_ref[...]
    h = _modulated(x, g_ref, mod_ref[3:4, :], mod_ref[4:5, :]).astype(BF16)
    ab = _dot(h, win_ref[...])
    act = (_silu(ab[:, :D_FF]) * ab[:, D_FF:]).astype(BF16)
    y = x + mod_ref[5:6, :] * _dot(act, wout_ref[...])
    if final:
        y = _rms(y, D_MODEL) * fg_ref[...]
    o_ref[...] = y


def _ffn_call(x_cat, mod_l, n2g, win, wout, fg, nt_lat, nt_run, final):
    bsz, ltot, d = x_cat.shape
    out_len = nt_run * TILE if final else ltot
    tok = pl.BlockSpec((None, TILE, d), lambda b, j: (b, j, 0))
    const = lambda a: pl.BlockSpec(a.shape, lambda b, j: (0,) * a.ndim)
    return pl.pallas_call(
        functools.partial(_ffn_kernel, final),
        grid=(bsz, nt_run),
        in_specs=[tok, pl.BlockSpec((None, 6, d), _mod_row_map(nt_lat, bsz)), const(n2g),
                  const(win), const(wout), const(fg)],
        out_specs=tok,
        out_shape=jax.ShapeDtypeStruct((bsz, out_len, d), F32),
        compiler_params=_cparams(2),
        name="swiglu_final" if final else "swiglu",
    )(x_cat, mod_l, n2g, win, wout, fg)


def kernel(x, c, ctx, c_ctx, w_mod, b_mod, norm1_g, w_in, dn_conv_w, dn_a_log, dn_dt_bias, dn_onorm_g, w_dn_out,
           sc_conv_w, w_sc_out, w_fn_out, w_gate, b_gate, w_o, norm2_g, w_ffn_in, w_ffn_out, final_g):
    bsz, l_lat, d = x.shape
    lc = ctx.shape[1]
    assert d == D_MODEL and lc == TILE and l_lat % TILE == 0 and w_ffn_in.shape[-1] == 2 * D_FF
    depth = w_mod.shape[0]
    nt_lat = l_lat // TILE
    nt_all = nt_lat + 1
    n_lat_chunks = l_lat // CHUNK

    rows = ((bsz + 1 + 7) // 8) * 8
    c_rows = jnp.zeros((rows, d), F32).at[:bsz].set(c).at[bsz].set(c_ctx)
    mod = _mod_call(c_rows, w_mod, b_mod).reshape(depth, rows, 6, d)

    x_cat = jnp.concatenate([x, ctx], axis=1)
    row = lambda a: a.reshape(1, -1)
    lane_pad = lambda a: jnp.zeros((1, HEAD_DIM), F32).at[0, N_DH:2 * N_DH].set(a.reshape(-1))
    for layer in range(depth):
        last = layer == depth - 1
        wl = w_in[layer]
        qkv_w = 3 * KEYW
        o_z, o_ba, o_sc, o_fn = qkv_w, qkv_w + KEYW, qkv_w + KEYW + 2 * N_DH, qkv_w + KEYW + 2 * N_DH + 3 * KEYW
        wba = jnp.zeros((d, HEAD_DIM), F32).at[:, :2 * N_DH].set(wl[:, o_ba:o_sc])
        wparts = tuple(a.astype(BF16) for a in (wl[:, :o_z], wl[:, o_z:o_ba], wba, wl[:, o_sc:o_fn], wl[:, o_fn:]))
        q, k, v, z, bg, sc, fn = _inproj_call(
            x_cat, mod[layer], row(norm1_g[layer]), wparts, dn_conv_w[layer], sc_conv_w[layer],
            lane_pad(dn_a_log[layer]), lane_pad(dn_dt_bias[layer]), nt_lat)
        u, w, qg, kd, attn, adec = _dnprep_call(q, k, v, bg)
        o_f, o_b = _dnscan_call(u, w, qg, kd, attn, adec, n_lat_chunks)
        fn_lat = _fourier_lat_call(fn, l_lat)
        nt_run = nt_lat if last else nt_all
        fn_ctx = jnp.zeros((bsz, lc, KEYW), F32) if last else _fourier_ctx_call(fn, l_lat)
        x_cat = _merge_call(
            x_cat, mod[layer], row(norm1_g[layer]), o_f, o_b, z, sc, fn_lat, fn_ctx,
            w_gate[layer].astype(BF16), row(b_gate[layer]), row(dn_onorm_g[layer]),
            w_dn_out[layer].astype(BF16), w_sc_out[layer].astype(BF16), w_fn_out[layer].astype(BF16),
            w_o[layer].astype(BF16), nt_lat, nt_run)
        x_cat = _ffn_call(x_cat, mod[layer], row(norm2_g[layer]), w_ffn_in[layer].astype(BF16),
                          w_ffn_out[layer].astype(BF16), row(final_g), nt_lat, nt_run, last)
    return x_cat
```

```python
import functools
import math

import numpy as np
import jax
import jax.numpy as jnp
from jax import lax
from jax.experimental import pallas as pl
from jax.experimental.pallas import tpu as pltpu

F32 = jnp.float32
BF16 = jnp.bfloat16
HIGHEST = lax.Precision.HIGHEST

EPS = 1e-6
D_MODEL = 1024
DEPTH = 2
HEADS = 4
HEAD_DIM = 128
KEYW = HEADS * HEAD_DIM
CHUNK = 64
D_FF = 2816
TILE = 256
N_DH = 2 * HEADS
PACKW = HEADS * CHUNK
EXP_W = 2 * KEYW + PACKW
PREP_CHUNKS = 4
SCAN_BATCH = 4
VMEM_LIMIT = 56 * 1024 * 1024


def _cparams(n_axes):
    return pltpu.CompilerParams(dimension_semantics=("arbitrary",) * n_axes, vmem_limit_bytes=VMEM_LIMIT)


def _dot(a, b):
    return jnp.dot(a, b, preferred_element_type=F32)


def _dot_hi(a, b):
    return jnp.dot(a, b, preferred_element_type=F32, precision=HIGHEST)


def _dot_nt(a, b):
    return lax.dot_general(a, b, (((1,), (1,)), ((), ())), preferred_element_type=F32)


def _dot_tn(a, b):
    return lax.dot_general(a, b, (((0,), (0,)), ((), ())), preferred_element_type=F32)


def _silu(t):
    return t * jax.nn.sigmoid(t)


def _rms(t, axis_size):
    return t * lax.rsqrt(jnp.sum(t * t, axis=-1, keepdims=True) * (1.0 / axis_size) + EPS)


def _mod_kernel(c_ref, w_ref, b_ref, o_ref):
    o_ref[...] = _dot_hi(_silu(c_ref[...]), w_ref[...]) + b_ref[...]


def _mod_call(c_rows, w_mod, b_mod):
    depth, d, n = w_mod.shape
    rows = c_rows.shape[0]
    tn = 1536
    return pl.pallas_call(
        _mod_kernel,
        grid=(depth, n // tn),
        in_specs=[
            pl.BlockSpec((rows, d), lambda l, j: (0, 0)),
            pl.BlockSpec((None, d, tn), lambda l, j: (l, 0, j)),
            pl.BlockSpec((None, 1, tn), lambda l, j: (l, 0, j)),
        ],
        out_specs=pl.BlockSpec((None, rows, tn), lambda l, j: (l, 0, j)),
        out_shape=jax.ShapeDtypeStruct((depth, rows, n), F32),
        compiler_params=_cparams(2),
        name="mod_vectors",
    )(c_rows, w_mod, b_mod.reshape(depth, 1, n))


def _row_masks(is_lat):
    pos = lax.broadcasted_iota(jnp.int32, (TILE, 1), 0)
    row_mask = jnp.where(is_lat, CHUNK - 1, TILE - 1)
    in_row = pos & row_mask
    return jnp.where(in_row == 0, 0.0, 1.0).astype(F32), jnp.where(in_row == row_mask, 0.0, 1.0).astype(F32)


def _conv3(t, w_ref, m_prev, m_next):
    prev = pltpu.roll(t, 1, 0) * m_prev
    nxt = pltpu.roll(t, TILE - 1, 0) * m_next
    return prev * w_ref[0:1, :] + t * w_ref[1:2, :] + nxt * w_ref[2:3, :]


def _modulated(x, g_ref, shift, scale):
    return _rms(x, D_MODEL) * g_ref[...] * (1.0 + scale) + shift


def _inproj_kernel(nt_lat, x_ref, mod_ref, g_ref, wqkv_ref, wz_ref, wba_ref, wsc_ref, wfn_ref,
                   dnconv_ref, scconv_ref, alog_ref, dtb_ref,
                   q_ref, k_ref, v_ref, z_ref, bg_ref, sc_ref, fn_ref):
    is_lat = pl.program_id(1) < nt_lat
    m_prev, m_next = _row_masks(is_lat)
    h = _modulated(x_ref[...], g_ref, mod_ref[0:1, :], mod_ref[1:2, :]).astype(BF16)

    qkv = _silu(_conv3(_dot(h, wqkv_ref[...]), dnconv_ref, m_prev, m_next))
    for hd in range(HEADS):
        lo = hd * HEAD_DIM
        qh = qkv[:, lo:lo + HEAD_DIM]
        kh = qkv[:, KEYW + lo:KEYW + lo + HEAD_DIM]
        q_ref[:, lo:lo + HEAD_DIM] = qh * lax.rsqrt(jnp.sum(qh * qh, axis=-1, keepdims=True) + EPS) * (HEAD_DIM ** -0.5)
        k_ref[:, lo:lo + HEAD_DIM] = kh * lax.rsqrt(jnp.sum(kh * kh, axis=-1, keepdims=True) + EPS)
    v_ref[...] = qkv[:, 2 * KEYW:]
    z_ref[...] = _dot(h, wz_ref[...])

    ba = _dot(h, wba_ref[...])
    sp_in = ba + dtb_ref[...]
    softplus = jnp.maximum(sp_in, 0.0) + jnp.log1p(jnp.exp(-jnp.abs(sp_in)))
    lane = lax.broadcasted_iota(jnp.int32, ba.shape, 1)
    g = -jnp.exp(alog_ref[...]) * softplus
    in_chunk = lax.broadcasted_iota(jnp.int32, (TILE, 1), 0) & (CHUNK - 1)
    pre, suf = g, g
    step = 1
    while step < CHUNK:
        pre = pre + jnp.where(in_chunk >= step, pltpu.roll(pre, step, 0), 0.0)
        suf = suf + jnp.where(in_chunk < CHUNK - step, pltpu.roll(suf, TILE - step, 0), 0.0)
        step *= 2
    gc = jnp.where(lane < N_DH + HEADS, pre, suf)
    bg_ref[...] = jnp.where(lane < N_DH, jax.nn.sigmoid(ba), gc)

    scp = _dot(h, wsc_ref[...])
    sc_ref[...] = scp[:, :KEYW] * _conv3(scp[:, KEYW:2 * KEYW] * scp[:, 2 * KEYW:], scconv_ref, m_prev, m_next)
    fn_ref[...] = _dot(h, wfn_ref[...])


def _mod_row_map(nt_lat, ctx_row):
    return lambda b, j: (jnp.where(j < nt_lat, b, ctx_row), 0, 0)


def _inproj_call(x_cat, mod_l, n1g, wparts, dnconv, scconv, alog_row, dtb_row, nt_lat):
    bsz, ltot, d = x_cat.shape
    nt = ltot // TILE
    wqkv, wz, wba, wsc, wfn = wparts
    ctx_row = bsz
    tok = lambda w: pl.BlockSpec((None, TILE, w), lambda b, j: (b, j, 0))
    const = lambda a: pl.BlockSpec(a.shape, lambda b, j: (0,) * a.ndim)
    out_w = (KEYW, KEYW, KEYW, KEYW, HEAD_DIM, KEYW, KEYW)
    return pl.pallas_call(
        functools.partial(_inproj_kernel, nt_lat),
        grid=(bsz, nt),
        in_specs=[tok(d), pl.BlockSpec((None, 6, d), _mod_row_map(nt_lat, ctx_row)), const(n1g),
                  const(wqkv), const(wz), const(wba), const(wsc), const(wfn),
                  const(dnconv), const(scconv), const(alog_row), const(dtb_row)],
        out_specs=[tok(w) for w in out_w],
        out_shape=[jax.ShapeDtypeStruct((bsz, ltot, w), F32) for w in out_w],
        compiler_params=_cparams(2),
        name="in_proj",
    )(x_cat, mod_l, n1g, wqkv, wz, wba, wsc, wfn, dnconv, scconv, alog_row, dtb_row)


def _split2(t):
    hi = t.astype(BF16)
    return hi, (t - hi.astype(F32)).astype(BF16)


def _mm3(lhs, rhs):
    m = lhs.shape[0]
    lh, ll = _split2(lhs)
    rh, rl = _split2(rhs)
    top = _dot(jnp.concatenate([lh, ll], axis=0), rh)
    return top[:m] + top[m:] + _dot(lh, rl)


def _expand_matrix():
    e = np.zeros((HEAD_DIM, 2 * EXP_W), np.float32)
    for dr in range(2):
        for hd in range(HEADS):
            unit = dr * HEADS + hd
            base = dr * EXP_W
            e[N_DH + unit, base + hd * HEAD_DIM:base + (hd + 1) * HEAD_DIM] = 1.0
            e[unit, base + KEYW + hd * HEAD_DIM:base + KEYW + (hd + 1) * HEAD_DIM] = 1.0
            e[N_DH + unit, base + 2 * KEYW + hd * CHUNK:base + 2 * KEYW + (hd + 1) * CHUNK] = 1.0
    return jnp.asarray(e, BF16)


def _dnprep_kernel(cps, q_ref, k_ref, v_ref, bg_ref, e_ref, u_ref, w_ref, qg_ref, kd_ref, attn_ref, adec_ref):
    ri = lax.broadcasted_iota(jnp.int32, (CHUNK, PACKW), 0)
    cj = lax.broadcasted_iota(jnp.int32, (CHUNK, PACKW), 1) & (CHUNK - 1)
    eye_p = jnp.where(ri == cj, 1.0, 0.0).astype(F32)
    incl_m = ((ri >= cj), (ri <= cj))
    strict_m = ((ri > cj), (ri < cj))

    def half_masked(t, hd):
        lane = lax.broadcasted_iota(jnp.int32, t.shape, 1)
        keep = (lane < CHUNK) if hd % 2 == 0 else (lane >= CHUNK)
        return jnp.where(keep, t, jnp.zeros((), BF16))

    def blockdiag(t):
        z = jnp.zeros((CHUNK, HEAD_DIM), BF16)
        blocks = []
        for hd in range(HEADS):
            piece = half_masked(t[:, (hd // 2) * HEAD_DIM:(hd // 2 + 1) * HEAD_DIM], hd)
            blocks.append(jnp.concatenate([piece, z] if hd < 2 else [z, piece], axis=1))
        return jnp.concatenate(blocks, axis=0)

    def mm3_all(lhs_list, rhs_list):
        lsp = [_split2(l) for l in lhs_list]
        rsp = [tuple(blockdiag(p) for p in _split2(r)) for r in rhs_list]
        tops = [_dot(jnp.concatenate([lh, ll], axis=0), rh) for (lh, ll), (rh, _) in zip(lsp, rsp)]
        lows = [_dot(lh, rl) for (lh, _), (_, rl) in zip(lsp, rsp)]
        return [t[:l.shape[0]] + t[l.shape[0]:] + lo for t, lo, l in zip(tops, lows, lhs_list)]

    chunk_rows = [slice(ci * CHUNK, (ci + 1) * CHUNK) for ci in range(cps)]
    qs = [q_ref[r, :] for r in chunk_rows]
    ks = [k_ref[r, :] for r in chunk_rows]
    vs = [v_ref[r, :] for r in chunk_rows]

    exs = []
    for r in chunk_rows:
        bg = bg_ref[r, :]
        hi = bg.astype(BF16)
        r1 = bg - hi.astype(F32)
        mid = r1.astype(BF16)
        low = (r1 - mid.astype(F32)).astype(BF16)
        exs.append(_dot(jnp.concatenate([hi, mid, low], axis=0), e_ref[...]))
    exs = [ex[:CHUNK] + ex[CHUNK:2 * CHUNK] + ex[2 * CHUNK:] for ex in exs]

    def parts(ci, dr):
        base = dr * EXP_W
        ex = exs[ci]
        return ex[:, base:base + KEYW], ex[:, base + KEYW:base + 2 * KEYW], ex[:, base + 2 * KEYW:base + EXP_W]

    m1s = []
    for ci in range(cps):
        k = ks[ci]
        k_t = jnp.concatenate([k, k], axis=0).T.astype(BF16)
        zk = jnp.zeros((HEAD_DIM, HEAD_DIM), BF16)
        bd_rows = []
        for hd in range(HEADS):
            piece = half_masked(k_t[hd * HEAD_DIM:(hd + 1) * HEAD_DIM, :], hd)
            bd_rows.append(jnp.concatenate([piece, zk] if hd < 2 else [zk, piece], axis=1))
        bd_k = jnp.concatenate(bd_rows, axis=0)
        kbs = [k * parts(ci, dr)[1] for dr in range(2)]
        m1s.append(_dot(jnp.concatenate(kbs + [qs[ci]], axis=0).astype(BF16), bd_k))

    chains = [(ci, dr) for ci in range(cps) for dr in range(2)]
    decays, neg_as = [], []
    for ci, dr in chains:
        gcol_p = parts(ci, dr)[2]
        grow_p = jnp.sum(gcol_p * eye_p, axis=0, keepdims=True)
        decay = jnp.exp(jnp.where(incl_m[dr], gcol_p - grow_p, -jnp.inf))
        decays.append(decay)
        neg_as.append(jnp.where(strict_m[dr], -(m1s[ci][dr * CHUNK:(dr + 1) * CHUNK] * decay), 0.0))

    p_accs = [eye_p + a for a in neg_as]
    q_pows = mm3_all(neg_as, neg_as)
    for _ in range(4):
        boths = mm3_all([jnp.concatenate([qp, pa], axis=0) for qp, pa in zip(q_pows, p_accs)], q_pows)
        q_pows = [b[:CHUNK] for b in boths]
        p_accs = [pa + b[CHUNK:] for pa, b in zip(p_accs, boths)]
    lasts = mm3_all(p_accs, q_pows)
    p_accs = [pa + la for pa, la in zip(p_accs, lasts)]

    uws, egs = [], []
    for (ci, dr), t_inv in zip(chains, p_accs):
        gcol, beta, _ = parts(ci, dr)
        eg = jnp.exp(gcol)
        egs.append(eg)
        rhs = jnp.concatenate([vs[ci] * beta, ks[ci] * (beta * eg)], axis=1).astype(BF16)
        zr = jnp.zeros((CHUNK, HEAD_DIM), BF16)
        n_tiles = 2 * KEYW // HEAD_DIM
        bd_r = jnp.concatenate([
            jnp.concatenate([rhs[:, t * HEAD_DIM:(t + 1) * HEAD_DIM] if t % HEADS == hd else zr
                             for t in range(n_tiles)], axis=1)
            for hd in range(HEADS)], axis=0)
        uws.append(_dot(t_inv.astype(BF16), bd_r))

    for (ci, dr), uw, eg, decay in zip(chains, uws, egs, decays):
        rows = chunk_rows[ci]
        gcol = parts(ci, dr)[0]
        g_last = gcol[CHUNK - 1:CHUNK, :] if dr == 0 else gcol[0:1, :]
        u_ref[dr, rows, :] = uw[:, :KEYW]
        w_ref[dr, rows, :] = uw[:, KEYW:].astype(BF16)
        qg_ref[dr, rows, :] = (qs[ci] * eg).astype(BF16)
        kd_ref[dr, rows, :] = (ks[ci] * jnp.exp(g_last - gcol)).astype(BF16)
        attn_ref[dr, rows, :] = (m1s[ci][2 * CHUNK:] * decay).astype(BF16)
        a_dec = jnp.exp(g_last)
        for hd in range(HEADS):
            unit = dr * HEADS + hd
            adec_ref[ci, unit:unit + 1, :] = a_dec[:, hd * HEAD_DIM:(hd + 1) * HEAD_DIM]


def _dnprep_call(q, k, v, bg):
    bsz, ltot, _ = q.shape
    nch = ltot // CHUNK
    cps = PREP_CHUNKS
    assert nch % cps == 0
    e_mat = _expand_matrix()
    tok = lambda w: pl.BlockSpec((None, cps * CHUNK, w), lambda b, n: (b, n, 0))
    dtok = lambda w: pl.BlockSpec((None, 2, cps * CHUNK, w), lambda b, n: (b, 0, n, 0))
    return pl.pallas_call(
        functools.partial(_dnprep_kernel, cps),
        grid=(bsz, nch // cps),
        in_specs=[tok(KEYW), tok(KEYW), tok(KEYW), tok(HEAD_DIM),
                  pl.BlockSpec(e_mat.shape, lambda b, n: (0, 0))],
        out_specs=[dtok(KEYW), dtok(KEYW), dtok(KEYW), dtok(KEYW), dtok(PACKW),
                   pl.BlockSpec((None, cps, N_DH, HEAD_DIM), lambda b, n: (b, n, 0, 0))],
        out_shape=[jax.ShapeDtypeStruct((bsz, 2, ltot, KEYW), F32),
                   jax.ShapeDtypeStruct((bsz, 2, ltot, KEYW), BF16),
                   jax.ShapeDtypeStruct((bsz, 2, ltot, KEYW), BF16),
                   jax.ShapeDtypeStruct((bsz, 2, ltot, KEYW), BF16),
                   jax.ShapeDtypeStruct((bsz, 2, ltot, PACKW), BF16),
                   jax.ShapeDtypeStruct((bsz, nch, N_DH, HEAD_DIM), F32)],
        compiler_params=_cparams(2),
        name="dn_prep",
    )(q, k, v, bg, e_mat)


def _dnscan_kernel(uf_ref, wf_ref, qgf_ref, kdf_ref, af_ref, df_ref,
                   ub_ref, wb_ref, qgb_ref, kdb_ref, ab_ref, db_ref,
                   of_ref, ob_ref, s_ref):
    @pl.when(pl.program_id(1) == 0)
    def _():
        s_ref[...] = jnp.zeros_like(s_ref)

    per_dir = ((uf_ref, wf_ref, qgf_ref, kdf_ref, af_ref, df_ref, of_ref),
               (ub_ref, wb_ref, qgb_ref, kdb_ref, ab_ref, db_ref, ob_ref))
    units = [(bi, dr, hd) for bi in range(uf_ref.shape[0]) for dr in range(2) for hd in range(HEADS)]
    first = []
    for bi, dr, hd in units:
        u_ref, w_ref, qg_ref = per_dir[dr][:3]
        lo = hd * HEAD_DIM
        s_b = s_ref[bi, dr * HEADS + hd].astype(BF16)
        first.append(_dot(jnp.concatenate([w_ref[bi, :, lo:lo + HEAD_DIM], qg_ref[bi, :, lo:lo + HEAD_DIM]], axis=0), s_b))
    for (bi, dr, hd), both in zip(units, first):
        u_ref, _, _, kd_ref, a_ref, d_ref, o_ref = per_dir[dr]
        unit = dr * HEADS + hd
        lo = hd * HEAD_DIM
        v_new_b = (u_ref[bi, :, lo:lo + HEAD_DIM] - both[:CHUNK]).astype(BF16)
        o_ref[bi, :, lo:lo + HEAD_DIM] = both[CHUNK:] + _dot(a_ref[bi, :, hd * CHUNK:(hd + 1) * CHUNK], v_new_b)
        s_ref[bi, unit] = (s_ref[bi, unit] * d_ref[bi, unit:unit + 1, :]
                           + _dot_tn(kd_ref[bi, :, lo:lo + HEAD_DIM], v_new_b))


def _dnscan_call(u, w, qg, kd, attn, adec, n_lat_chunks):
    bsz, _, ltot, _ = u.shape
    nch = ltot // CHUNK
    bb = SCAN_BATCH if bsz % SCAN_BATCH == 0 else 1
    fwd = lambda n: (n + n_lat_chunks) % nch
    bwd = lambda n: nch - 1 - n

    def dspec(wd, dr, order):
        return pl.BlockSpec((bb, None, CHUNK, wd), lambda b, n: (b, dr, order(n), 0))

    def aspec(order):
        return pl.BlockSpec((bb, None, N_DH, HEAD_DIM), lambda b, n: (b, order(n), 0, 0))

    def ospec(order):
        return pl.BlockSpec((bb, CHUNK, KEYW), lambda b, n: (b, order(n), 0))

    in_specs = []
    for dr, order in ((0, fwd), (1, bwd)):
        in_specs += [dspec(KEYW, dr, order), dspec(KEYW, dr, order), dspec(KEYW, dr, order), dspec(KEYW, dr, order),
                     dspec(PACKW, dr, order), aspec(order)]
    return pl.pallas_call(
        _dnscan_kernel,
        grid=(bsz // bb, nch),
        in_specs=in_specs,
        out_specs=[ospec(fwd), ospec(bwd)],
        out_shape=[jax.ShapeDtypeStruct((bsz, ltot, KEYW), F32)] * 2,
        scratch_shapes=[pltpu.VMEM((bb, N_DH, HEAD_DIM, HEAD_DIM), F32)],
        compiler_params=_cparams(2),
        name="dn_scan",
    )(u, w, qg, kd, attn, adec, u, w, qg, kd, attn, adec)


def _dft_mats(n):
    idx = np.arange(n)
    ang = 2.0 * np.pi * ((idx[:, None] * idx[None, :]) % n) / n
    return np.cos(ang), np.sin(ang)


def _fft1_kernel(n2blk, u_ref, f1_ref, tc_ref, ts_ref, br_ref, bi_ref):
    l1 = f1_ref.shape[1]
    for jj in range(n2blk):
        a = _dot(f1_ref[...], u_ref[:, jj, :].astype(BF16))
        ar, ai = a[:l1], a[l1:]
        tc = jnp.concatenate([tc_ref[jj]] * (KEYW // HEAD_DIM), axis=1)
        ts = jnp.concatenate([ts_ref[jj]] * (KEYW // HEAD_DIM), axis=1)
        br_ref[:, jj, :] = ar * tc + ai * ts
        bi_ref[:, jj, :] = ai * tc - ar * ts


def _fft2_kernel(k1blk, nlen, br_ref, bi_ref, f2_ref, wc_ref, y_ref):
    for jj in range(k1blk):
        rows = slice(jj * nlen, (jj + 1) * nlen)
        x = _dot(f2_ref[...], jnp.concatenate([br_ref[rows, :], bi_ref[rows, :]], axis=0).astype(BF16))
        xr, xi = x[:nlen].astype(BF16), x[nlen:].astype(BF16)
        y_ref[:, jj, :] = jnp.concatenate(
            [_dot(jnp.concatenate([xr[:, g * HEAD_DIM:(g + 1) * HEAD_DIM], xi[:, g * HEAD_DIM:(g + 1) * HEAD_DIM]],
                                  axis=1), wc_ref[...]) for g in range(KEYW // HEAD_DIM)], axis=1)


def _dftctx_kernel(nlen, u_ref, f_ref, wc_ref, y_ref):
    x = _dot(f_ref[...], u_ref[...].astype(BF16))
    xr, xi = x[:nlen].astype(BF16), x[nlen:].astype(BF16)
    for g in range(KEYW // HEAD_DIM):
        gs = slice(g * HEAD_DIM, (g + 1) * HEAD_DIM)
        y_ref[:, gs] = _dot(jnp.concatenate([xr[:, gs], xi[:, gs]], axis=1), wc_ref[...])


def _chan_mat(total_len):
    cc, sc = _dft_mats(HEAD_DIM)
    return jnp.asarray(np.concatenate([cc, sc], axis=0) / math.sqrt(total_len * HEAD_DIM), BF16)


def _fourier_lat_call(fn_cat, l_lat):
    bsz, ltot, _ = fn_cat.shape
    l1 = l_lat // CHUNK
    n2blk = 8
    k1blk = 8 if l1 % 8 == 0 else l1
    c1, s1 = _dft_mats(l1)
    f1 = jnp.asarray(np.concatenate([c1, -s1], axis=0), BF16)
    n2 = np.arange(CHUNK)[:, None, None]
    k1 = np.arange(l1)[None, :, None]
    ang = 2.0 * np.pi * (n2 * k1) / l_lat * np.ones((1, 1, HEAD_DIM))
    tc, ts = jnp.asarray(np.cos(ang), F32), jnp.asarray(np.sin(ang), F32)
    c2, s2 = _dft_mats(CHUNK)
    f2 = jnp.asarray(np.block([[c2, s2], [-s2, c2]]), BF16)
    wc = _chan_mat(l_lat)

    u_view = fn_cat.reshape(bsz, ltot // CHUNK, CHUNK, KEYW)
    bspec = pl.BlockSpec((None, l1, n2blk, KEYW), lambda b, i: (b, 0, i, 0))
    br, bi = pl.pallas_call(
        functools.partial(_fft1_kernel, n2blk),
        grid=(bsz, CHUNK // n2blk),
        in_specs=[bspec,
                  pl.BlockSpec(f1.shape, lambda b, i: (0, 0)),
                  pl.BlockSpec((n2blk, l1, HEAD_DIM), lambda b, i: (i, 0, 0)),
                  pl.BlockSpec((n2blk, l1, HEAD_DIM), lambda b, i: (i, 0, 0))],
        out_specs=[bspec, bspec],
        out_shape=[jax.ShapeDtypeStruct((bsz, l1, CHUNK, KEYW), F32)] * 2,
        compiler_params=_cparams(2),
        name="fourier_stage1",
    )(u_view, f1, tc, ts)

    rspec = pl.BlockSpec((None, CHUNK * k1blk, KEYW), lambda b, i: (b, i, 0))
    y = pl.pallas_call(
        functools.partial(_fft2_kernel, k1blk, CHUNK),
        grid=(bsz, l1 // k1blk),
        in_specs=[rspec, rspec,
                  pl.BlockSpec(f2.shape, lambda b, i: (0, 0)),
                  pl.BlockSpec(wc.shape, lambda b, i: (0, 0))],
        out_specs=pl.BlockSpec((None, CHUNK, k1blk, KEYW), lambda b, i: (b, 0, i, 0)),
        out_shape=jax.ShapeDtypeStruct((bsz, CHUNK, l1, KEYW), F32),
        compiler_params=_cparams(2),
        name="fourier_stage2",
    )(br.reshape(bsz, l_lat, KEYW), bi.reshape(bsz, l_lat, KEYW), f2, wc)
    return y.reshape(bsz, l_lat, KEYW)


def _fourier_ctx_call(fn_cat, l_lat):
    bsz, ltot, _ = fn_cat.shape
    lc = ltot - l_lat
    c, s = _dft_mats(lc)
    f = jnp.asarray(np.concatenate([c, -s], axis=0), BF16)
    wc = _chan_mat(lc)
    return pl.pallas_call(
        functools.partial(_dftctx_kernel, lc),
        grid=(bsz,),
        in_specs=[pl.BlockSpec((None, lc, KEYW), lambda b: (b, l_lat // lc, 0)),
                  pl.BlockSpec(f.shape, lambda b: (0, 0)),
                  pl.BlockSpec(wc.shape, lambda b: (0, 0))],
        out_specs=pl.BlockSpec((None, lc, KEYW), lambda b: (b, 0, 0)),
        out_shape=jax.ShapeDtypeStruct((bsz, lc, KEYW), F32),
        compiler_params=_cparams(1),
        name="fourier_ctx",
    )(fn_cat, f, wc)


def _merge_kernel(nt_lat, x_ref, mod_ref, g_ref, of_ref, ob_ref, z_ref, sc_ref, fl_ref, fc_ref,
                  wgate_ref, bgate_ref, og_ref, wdn_ref, wsc_ref, wfn_ref, wo_ref, o_ref):
    is_lat = pl.program_id(1) < nt_lat
    x = x_ref[...]
    h = _modulated(x, g_ref, mod_ref[0:1, :], mod_ref[1:2, :]).astype(BF16)
    gates = jax.nn.sigmoid(_dot(h, wgate_ref[...]) + bgate_ref[...])

    o = of_ref[...] + ob_ref[...]
    z = z_ref[...]
    dn_parts = []
    for hd in range(HEADS):
        sl = slice(hd * HEAD_DIM, (hd + 1) * HEAD_DIM)
        dn_parts.append(_rms(o[:, sl], HEAD_DIM) * og_ref[...] * _silu(z[:, sl]))
    y_dn = _dot(jnp.concatenate(dn_parts, axis=1).astype(BF16), wdn_ref[...])
    y_sc = _dot(sc_ref[...].astype(BF16), wsc_ref[...])
    fn = jnp.where(is_lat, fl_ref[...], fc_ref[...])
    y_fn = _dot(fn.astype(BF16), wfn_ref[...])
    merged = (gates[:, :D_MODEL] * y_dn + gates[:, D_MODEL:2 * D_MODEL] * y_sc + gates[:, 2 * D_MODEL:] * y_fn)
    o_ref[...] = x + mod_ref[2:3, :] * _dot(merged.astype(BF16), wo_ref[...])


def _merge_call(x_cat, mod_l, n1g, o_f, o_b, z, sc, fn_lat, fn_ctx, wgate, bgate, og, wdn, wsc, wfn, wo, nt_lat, nt_run):
    bsz, ltot, d = x_cat.shape
    tok = lambda w: pl.BlockSpec((None, TILE, w), lambda b, j: (b, j, 0))
    const = lambda a: pl.BlockSpec(a.shape, lambda b, j: (0,) * a.ndim)
    return pl.pallas_call(
        functools.partial(_merge_kernel, nt_lat),
        grid=(bsz, nt_run),
        in_specs=[tok(d), pl.BlockSpec((None, 6, d), _mod_row_map(nt_lat, bsz)), const(n1g),
                  tok(KEYW), tok(KEYW), tok(KEYW), tok(KEYW),
                  pl.BlockSpec((None, TILE, KEYW), lambda b, j: (b, jnp.minimum(j, nt_lat - 1), 0)),
                  pl.BlockSpec((None, TILE, KEYW), lambda b, j: (b, 0, 0)),
                  const(wgate), const(bgate), const(og), const(wdn), const(wsc), const(wfn), const(wo)],
        out_specs=tok(d),
        out_shape=jax.ShapeDtypeStruct((bsz, ltot, d), F32),
        compiler_params=_cparams(2),
        name="branch_merge",
    )(x_cat, mod_l, n1g, o_f, o_b, z, sc, fn_lat, fn_ctx, wgate, bgate, og, wdn, wsc, wfn, wo)


def _ffn_kernel(final, x_ref, mod_ref, g_ref, win_ref, wout_ref, fg_ref, o_ref):
    x = x_ref[...]
    h = _modulated(x, g_ref, mod_ref[3:4, :], mod_ref[4:5, :]).astype(BF16)
    ab = _dot(h, win_ref[...])
    act = (_silu(ab[:, :D_FF]) * ab[:, D_FF:]).astype(BF16)
    y = x + mod_ref[5:6, :] * _dot(act, wout_ref[...])
    if final:
        y = _rms(y, D_MODEL) * fg_ref[...]
    o_ref[...] = y


def _ffn_call(x_cat, mod_l, n2g, win, wout, fg, nt_lat, nt_run, final):
    bsz, ltot, d = x_cat.shape
    out_len = nt_run * TILE if final else ltot
    tok = pl.BlockSpec((None, TILE, d), lambda b, j: (b, j, 0))
    const = lambda a: pl.BlockSpec(a.shape, lambda b, j: (0,) * a.ndim)
    return pl.pallas_call(
        functools.partial(_ffn_kernel, final),
        grid=(bsz, nt_run),
        in_specs=[tok, pl.BlockSpec((None, 6, d), _mod_row_map(nt_lat, bsz)), const(n2g),
                  const(win), const(wout), const(fg)],
        out_specs=tok,
        out_shape=jax.ShapeDtypeStruct((bsz, out_len, d), F32),
        compiler_params=_cparams(2),
        name="swiglu_final" if final else "swiglu",
    )(x_cat, mod_l, n2g, win, wout, fg)


def kernel(x, c, ctx, c_ctx, w_mod, b_mod, norm1_g, w_in, dn_conv_w, dn_a_log, dn_dt_bias, dn_onorm_g, w_dn_out,
           sc_conv_w, w_sc_out, w_fn_out, w_gate, b_gate, w_o, norm2_g, w_ffn_in, w_ffn_out, final_g):
    bsz, l_lat, d = x.shape
    lc = ctx.shape[1]
    assert d == D_MODEL and lc == TILE and l_lat % TILE == 0 and w_ffn_in.shape[-1] == 2 * D_FF
    depth = w_mod.shape[0]
    nt_lat = l_lat // TILE
    nt_all = nt_lat + 1
    n_lat_chunks = l_lat // CHUNK

    rows = ((bsz + 1 + 7) // 8) * 8
    c_rows = jnp.zeros((rows, d), F32).at[:bsz].set(c).at[bsz].set(c_ctx)
    mod = _mod_call(c_rows, w_mod, b_mod).reshape(depth, rows, 6, d)

    x_cat = jnp.concatenate([x, ctx], axis=1)
    row = lambda a: a.reshape(1, -1)
    lane_pad = lambda a: jnp.zeros((1, HEAD_DIM), F32).at[0, N_DH:2 * N_DH].set(a.reshape(-1))
    for layer in range(depth):
        last = layer == depth - 1
        wl = w_in[layer]
        qkv_w = 3 * KEYW
        o_z, o_ba, o_sc, o_fn = qkv_w, qkv_w + KEYW, qkv_w + KEYW + 2 * N_DH, qkv_w + KEYW + 2 * N_DH + 3 * KEYW
        wba = jnp.zeros((d, HEAD_DIM), F32).at[:, :2 * N_DH].set(wl[:, o_ba:o_sc])
        wparts = tuple(a.astype(BF16) for a in (wl[:, :o_z], wl[:, o_z:o_ba], wba, wl[:, o_sc:o_fn], wl[:, o_fn:]))
        q, k, v, z, bg, sc, fn = _inproj_call(
            x_cat, mod[layer], row(norm1_g[layer]), wparts, dn_conv_w[layer], sc_conv_w[layer],
            lane_pad(dn_a_log[layer]), lane_pad(dn_dt_bias[layer]), nt_lat)
        u, w, qg, kd, attn, adec = _dnprep_call(q, k, v, bg)
        o_f, o_b = _dnscan_call(u, w, qg, kd, attn, adec, n_lat_chunks)
        fn_lat = _fourier_lat_call(fn, l_lat)
        nt_run = nt_lat if last else nt_all
        fn_ctx = jnp.zeros((bsz, lc, KEYW), F32) if last else _fourier_ctx_call(fn, l_lat)
        x_cat = _merge_call(
            x_cat, mod[layer], row(norm1_g[layer]), o_f, o_b, z, sc, fn_lat, fn_ctx,
            w_gate[layer].astype(BF16), row(b_gate[layer]), row(dn_onorm_g[layer]),
            w_dn_out[layer].astype(BF16), w_sc_out[layer].astype(BF16), w_fn_out[layer].astype(BF16),
            w_o[layer].astype(BF16), nt_lat, nt_run)
        x_cat = _ffn_call(x_cat, mod[layer], row(norm2_g[layer]), w_ffn_in[layer].astype(BF16),
                          w_ffn_out[layer].astype(BF16), row(final_g), nt_lat, nt_run, last)
    return x_cat
```

```python
import functools
import math

import numpy as np
import jax
import jax.numpy as jnp
from jax import lax
from jax.experimental import pallas as pl
from jax.experimental.pallas import tpu as pltpu

F32 = jnp.float32
BF16 = jnp.bfloat16
HIGHEST = lax.Precision.HIGHEST

EPS = 1e-6
D_MODEL = 1024
DEPTH = 2
HEADS = 4
HEAD_DIM = 128
KEYW = HEADS * HEAD_DIM
CHUNK = 64
D_FF = 2816
TILE = 256
N_DH = 2 * HEADS
PACKW = HEADS * CHUNK
EXP_W = 2 * KEYW + PACKW
PREP_CHUNKS = 4
SCAN_BATCH = 8
ROW_BATCH = 2
VMEM_LIMIT = 56 * 1024 * 1024


def _cparams(n_axes):
    return pltpu.CompilerParams(dimension_semantics=("arbitrary",) * n_axes, vmem_limit_bytes=VMEM_LIMIT)


def _dot(a, b):
    return jnp.dot(a, b, preferred_element_type=F32)


def _dot_hi(a, b):
    return jnp.dot(a, b, preferred_element_type=F32, precision=HIGHEST)


def _dot_nt(a, b):
    return lax.dot_general(a, b, (((1,), (1,)), ((), ())), preferred_element_type=F32)


def _dot_tn(a, b):
    return lax.dot_general(a, b, (((0,), (0,)), ((), ())), preferred_element_type=F32)


def _silu(t):
    return t * jax.nn.sigmoid(t)


def _rms(t, axis_size):
    return t * lax.rsqrt(jnp.sum(t * t, axis=-1, keepdims=True) * (1.0 / axis_size) + EPS)


def _mod_kernel(c_ref, w_ref, b_ref, o_ref):
    o_ref[...] = _dot_hi(_silu(c_ref[...]), w_ref[...]) + b_ref[...]


def _mod_call(c_rows, w_mod, b_mod):
    depth, d, n = w_mod.shape
    rows = c_rows.shape[0]
    tn = 1536
    return pl.pallas_call(
        _mod_kernel,
        grid=(depth, n // tn),
        in_specs=[
            pl.BlockSpec((rows, d), lambda l, j: (0, 0)),
            pl.BlockSpec((None, d, tn), lambda l, j: (l, 0, j)),
            pl.BlockSpec((None, 1, tn), lambda l, j: (l, 0, j)),
        ],
        out_specs=pl.BlockSpec((None, rows, tn), lambda l, j: (l, 0, j)),
        out_shape=jax.ShapeDtypeStruct((depth, rows, n), F32),
        compiler_params=_cparams(2),
        name="mod_vectors",
    )(c_rows, w_mod, b_mod.reshape(depth, 1, n))


def _row_masks(is_lat):
    pos = lax.broadcasted_iota(jnp.int32, (TILE, 1), 0)
    row_mask = jnp.where(is_lat, CHUNK - 1, TILE - 1)
    in_row = pos & row_mask
    return jnp.where(in_row == 0, 0.0, 1.0).astype(F32), jnp.where(in_row == row_mask, 0.0, 1.0).astype(F32)


def _conv3(t, w_ref, m_prev, m_next):
    prev = pltpu.roll(t, 1, 0) * m_prev
    nxt = pltpu.roll(t, TILE - 1, 0) * m_next
    return prev * w_ref[0:1, :] + t * w_ref[1:2, :] + nxt * w_ref[2:3, :]


def _modulated(x, g_ref, shift, scale):
    return _rms(x, D_MODEL) * g_ref[...] * (1.0 + scale) + shift


def _inproj_kernel(nt_lat, x_ref, ctx_ref, mod_ref, g_ref, wqkv_ref, wz_ref, wba_ref, wsc_ref, wfn_ref,
                   dnconv_ref, scconv_ref, alog_ref, dtb_ref,
                   q_ref, k_ref, v_ref, z_ref, bg_ref, sc_ref, fn_ref):
    is_lat = pl.program_id(1) < nt_lat
    m_prev, m_next = _row_masks(is_lat)
    n_rows = x_ref.shape[0]
    hs = []
    for s in range(n_rows):
        x = jnp.where(is_lat, x_ref[s], ctx_ref[s])
        hs.append(_modulated(x, g_ref, mod_ref[s, 0:1, :], mod_ref[s, 1:2, :]).astype(BF16))
    projs = [tuple(_dot(h, w_ref[...]) for w_ref in (wqkv_ref, wz_ref, wba_ref, wsc_ref, wfn_ref)) for h in hs]
    for s in range(n_rows):
        p_qkv, p_z, ba, scp, p_fn = projs[s]
        qkv = _silu(_conv3(p_qkv, dnconv_ref, m_prev, m_next))
        for hd in range(HEADS):
            lo = hd * HEAD_DIM
            qh = qkv[:, lo:lo + HEAD_DIM]
            kh = qkv[:, KEYW + lo:KEYW + lo + HEAD_DIM]
            q_ref[s, :, lo:lo + HEAD_DIM] = (qh * lax.rsqrt(jnp.sum(qh * qh, axis=-1, keepdims=True) + EPS)
                                             * (HEAD_DIM ** -0.5))
            k_ref[s, :, lo:lo + HEAD_DIM] = kh * lax.rsqrt(jnp.sum(kh * kh, axis=-1, keepdims=True) + EPS)
        v_ref[s] = qkv[:, 2 * KEYW:]
        z_ref[s] = p_z

        sp_in = ba + dtb_ref[...]
        softplus = jnp.maximum(sp_in, 0.0) + jnp.log1p(jnp.exp(-jnp.abs(sp_in)))
        lane = lax.broadcasted_iota(jnp.int32, ba.shape, 1)
        g = -jnp.exp(alog_ref[...]) * softplus
        in_chunk = lax.broadcasted_iota(jnp.int32, (TILE, 1), 0) & (CHUNK - 1)
        pre, suf = g, g
        step = 1
        while step < CHUNK:
            pre = pre + jnp.where(in_chunk >= step, pltpu.roll(pre, step, 0), 0.0)
            suf = suf + jnp.where(in_chunk < CHUNK - step, pltpu.roll(suf, TILE - step, 0), 0.0)
            step *= 2
        gc = jnp.where(lane < N_DH + HEADS, pre, suf)
        bg_ref[s] = jnp.where(lane < N_DH, jax.nn.sigmoid(ba), gc)

        sc_ref[s] = scp[:, :KEYW] * _conv3(scp[:, KEYW:2 * KEYW] * scp[:, 2 * KEYW:], scconv_ref, m_prev, m_next)
        fn_ref[s] = p_fn


def _const_spec(a):
    return pl.BlockSpec(a.shape, lambda b, j: (0,) * a.ndim, pipeline_mode=pl.Buffered(1))


def _mod_spec(bb, nt_lat, d):
    return pl.BlockSpec((bb, None, 6, d), lambda b, j: (b, jnp.where(j < nt_lat, 0, 1), 0, 0))


def _x_specs(bb, nt_lat, ctx_blk, d):
    return [pl.BlockSpec((bb, TILE, d), lambda b, j: (b, jnp.minimum(j, nt_lat - 1), 0)),
            pl.BlockSpec((bb, TILE, d), lambda b, j: (b, ctx_blk, 0))]


def _inproj_call(x_src, ctx_src, ctx_blk, mod_sel, n1g, wparts, dnconv, scconv, alog_row, dtb_row, nt_lat):
    bsz, _, d = x_src.shape
    nt = nt_lat + 1
    ltot = nt * TILE
    bb = ROW_BATCH if bsz % ROW_BATCH == 0 else 1
    wqkv, wz, wba, wsc, wfn = wparts
    tok = lambda w: pl.BlockSpec((bb, TILE, w), lambda b, j: (b, j, 0))
    out_w = (KEYW, KEYW, KEYW, KEYW, HEAD_DIM, KEYW, KEYW)
    consts = (n1g, wqkv, wz, wba, wsc, wfn, dnconv, scconv, alog_row, dtb_row)
    return pl.pallas_call(
        functools.partial(_inproj_kernel, nt_lat),
        grid=(bsz // bb, nt),
        in_specs=_x_specs(bb, nt_lat, ctx_blk, d) + [_mod_spec(bb, nt_lat, d)] + [_const_spec(a) for a in consts],
        out_specs=[tok(w) for w in out_w],
        out_shape=[jax.ShapeDtypeStruct((bsz, ltot, w), F32) for w in out_w],
        compiler_params=_cparams(2),
        name="in_proj",
    )(x_src, ctx_src, mod_sel, *consts)


def _split2(t):
    hi = t.astype(BF16)
    return hi, (t - hi.astype(F32)).astype(BF16)


def _mm3(lhs, rhs):
    m = lhs.shape[0]
    lh, ll = _split2(lhs)
    rh, rl = _split2(rhs)
    top = _dot(jnp.concatenate([lh, ll], axis=0), rh)
    return top[:m] + top[m:] + _dot(lh, rl)


def _expand_matrix():
    e = np.zeros((HEAD_DIM, 2 * EXP_W), np.float32)
    for dr in range(2):
        for hd in range(HEADS):
            unit = dr * HEADS + hd
            base = dr * EXP_W
            e[N_DH + unit, base + hd * HEAD_DIM:base + (hd + 1) * HEAD_DIM] = 1.0
            e[unit, base + KEYW + hd * HEAD_DIM:base + KEYW + (hd + 1) * HEAD_DIM] = 1.0
            e[N_DH + unit, base + 2 * KEYW + hd * CHUNK:base + 2 * KEYW + (hd + 1) * CHUNK] = 1.0
    return jnp.asarray(e, BF16)


def _dnprep_kernel(cps, q_ref, k_ref, v_ref, bg_ref, e_ref, u_ref, w_ref, qg_ref, kd_ref, attn_ref, adec_ref):
    ri = lax.broadcasted_iota(jnp.int32, (CHUNK, PACKW), 0)
    cj = lax.broadcasted_iota(jnp.int32, (CHUNK, PACKW), 1) & (CHUNK - 1)
    eye_p = jnp.where(ri == cj, 1.0, 0.0).astype(F32)
    incl_m = ((ri >= cj), (ri <= cj))
    strict_m = ((ri > cj), (ri < cj))

    def half_masked(t, hd):
        lane = lax.broadcasted_iota(jnp.int32, t.shape, 1)
        keep = (lane < CHUNK) if hd % 2 == 0 else (lane >= CHUNK)
        return jnp.where(keep, t, jnp.zeros((), BF16))

    def blockdiag(t):
        z = jnp.zeros((CHUNK, HEAD_DIM), BF16)
        blocks = []
        for hd in range(HEADS):
            piece = half_masked(t[:, (hd // 2) * HEAD_DIM:(hd // 2 + 1) * HEAD_DIM], hd)
            blocks.append(jnp.concatenate([piece, z] if hd < 2 else [z, piece], axis=1))
        return jnp.concatenate(blocks, axis=0)

    def mm3_all(lhs_list, rhs_list):
        lsp = [_split2(l) for l in lhs_list]
        rsp = [tuple(blockdiag(p) for p in _split2(r)) for r in rhs_list]
        tops = [_dot(jnp.concatenate([lh, ll], axis=0), rh) for (lh, ll), (rh, _) in zip(lsp, rsp)]
        lows = [_dot(lh, rl) for (lh, _), (_, rl) in zip(lsp, rsp)]
        return [t[:l.shape[0]] + t[l.shape[0]:] + lo for t, lo, l in zip(tops, lows, lhs_list)]

    chunk_rows = [slice(ci * CHUNK, (ci + 1) * CHUNK) for ci in range(cps)]
    qs = [q_ref[r, :] for r in chunk_rows]
    ks = [k_ref[r, :] for r in chunk_rows]
    vs = [v_ref[r, :] for r in chunk_rows]

    exs = []
    for r in chunk_rows:
        bg = bg_ref[r, :]
        hi = bg.astype(BF16)
        r1 = bg - hi.astype(F32)
        mid = r1.astype(BF16)
        low = (r1 - mid.astype(F32)).astype(BF16)
        exs.append(_dot(jnp.concatenate([hi, mid, low], axis=0), e_ref[...]))
    exs = [ex[:CHUNK] + ex[CHUNK:2 * CHUNK] + ex[2 * CHUNK:] for ex in exs]

    def parts(ci, dr):
        base = dr * EXP_W
        ex = exs[ci]
        return ex[:, base:base + KEYW], ex[:, base + KEYW:base + 2 * KEYW], ex[:, base + 2 * KEYW:base + EXP_W]

    m1s = []
    for ci in range(cps):
        k = ks[ci]
        k_t = jnp.concatenate([k, k], axis=0).T.astype(BF16)
        zk = jnp.zeros((HEAD_DIM, HEAD_DIM), BF16)
        bd_rows = []
        for hd in range(HEADS):
            piece = half_masked(k_t[hd * HEAD_DIM:(hd + 1) * HEAD_DIM, :], hd)
            bd_rows.append(jnp.concatenate([piece, zk] if hd < 2 else [zk, piece], axis=1))
        bd_k = jnp.concatenate(bd_rows, axis=0)
        kbs = [k * parts(ci, dr)[1] for dr in range(2)]
        m1s.append(_dot(jnp.concatenate(kbs + [qs[ci]], axis=0).astype(BF16), bd_k))

    chains = [(ci, dr) for ci in range(cps) for dr in range(2)]
    decays, neg_as = [], []
    for ci, dr in chains:
        gcol_p = parts(ci, dr)[2]
        grow_p = jnp.sum(gcol_p * eye_p, axis=0, keepdims=True)
        decay = jnp.exp(jnp.where(incl_m[dr], gcol_p - grow_p, -jnp.inf))
        decays.append(decay)
        neg_as.append(jnp.where(strict_m[dr], -(m1s[ci][dr * CHUNK:(dr + 1) * CHUNK] * decay), 0.0))

    p_accs = [eye_p + a for a in neg_as]
    q_pows = mm3_all(neg_as, neg_as)
    for _ in range(4):
        boths = mm3_all([jnp.concatenate([qp, pa], axis=0) for qp, pa in zip(q_pows, p_accs)], q_pows)
        q_pows = [b[:CHUNK] for b in boths]
        p_accs = [pa + b[CHUNK:] for pa, b in zip(p_accs, boths)]
    lasts = mm3_all(p_accs, q_pows)
    p_accs = [pa + la for pa, la in zip(p_accs, lasts)]

    uws, egs = [], []
    for (ci, dr), t_inv in zip(chains, p_accs):
        gcol, beta, _ = parts(ci, dr)
        eg = jnp.exp(gcol)
        egs.append(eg)
        rhs = jnp.concatenate([vs[ci] * beta, ks[ci] * (beta * eg)], axis=1).astype(BF16)
        zr = jnp.zeros((CHUNK, HEAD_DIM), BF16)
        n_tiles = 2 * KEYW // HEAD_DIM
        bd_r = jnp.concatenate([
            jnp.concatenate([rhs[:, t * HEAD_DIM:(t + 1) * HEAD_DIM] if t % HEADS == hd else zr
                             for t in range(n_tiles)], axis=1)
            for hd in range(HEADS)], axis=0)
        uws.append(_dot(t_inv.astype(BF16), bd_r))

    for (ci, dr), uw, eg, decay in zip(chains, uws, egs, decays):
        rows = chunk_rows[ci]
        gcol = parts(ci, dr)[0]
        g_last = gcol[CHUNK - 1:CHUNK, :] if dr == 0 else gcol[0:1, :]
        u_ref[dr, rows, :] = uw[:, :KEYW].astype(BF16)
        w_ref[dr, rows, :] = uw[:, KEYW:].astype(BF16)
        qg_ref[dr, rows, :] = (qs[ci] * eg).astype(BF16)
        kd_ref[dr, rows, :] = (ks[ci] * jnp.exp(g_last - gcol)).astype(BF16)
        attn_ref[dr, rows, :] = (m1s[ci][2 * CHUNK:] * decay).astype(BF16)
        a_dec = jnp.exp(g_last)
        for hd in range(HEADS):
            unit = dr * HEADS + hd
            adec_ref[ci, unit:unit + 1, :] = a_dec[:, hd * HEAD_DIM:(hd + 1) * HEAD_DIM]


def _dnprep_call(q, k, v, bg):
    bsz, ltot, _ = q.shape
    nch = ltot // CHUNK
    cps = PREP_CHUNKS
    assert nch % cps == 0
    e_mat = _expand_matrix()
    tok = lambda w: pl.BlockSpec((None, cps * CHUNK, w), lambda b, n: (b, n, 0))
    dtok = lambda w: pl.BlockSpec((None, 2, cps * CHUNK, w), lambda b, n: (b, 0, n, 0))
    return pl.pallas_call(
        functools.partial(_dnprep_kernel, cps),
        grid=(bsz, nch // cps),
        in_specs=[tok(KEYW), tok(KEYW), tok(KEYW), tok(HEAD_DIM),
                  pl.BlockSpec(e_mat.shape, lambda b, n: (0, 0))],
        out_specs=[dtok(KEYW), dtok(KEYW), dtok(KEYW), dtok(KEYW), dtok(PACKW),
                   pl.BlockSpec((None, cps, N_DH, HEAD_DIM), lambda b, n: (b, n, 0, 0))],
        out_shape=[jax.ShapeDtypeStruct((bsz, 2, ltot, KEYW), BF16),
                   jax.ShapeDtypeStruct((bsz, 2, ltot, KEYW), BF16),
                   jax.ShapeDtypeStruct((bsz, 2, ltot, KEYW), BF16),
                   jax.ShapeDtypeStruct((bsz, 2, ltot, KEYW), BF16),
                   jax.ShapeDtypeStruct((bsz, 2, ltot, PACKW), BF16),
                   jax.ShapeDtypeStruct((bsz, nch, N_DH, HEAD_DIM), F32)],
        compiler_params=_cparams(2),
        name="dn_prep",
    )(q, k, v, bg, e_mat)


def _dnscan_kernel(uf_ref, wf_ref, qgf_ref, kdf_ref, af_ref, df_ref,
                   ub_ref, wb_ref, qgb_ref, kdb_ref, ab_ref, db_ref,
                   of_ref, ob_ref, s_ref):
    @pl.when(pl.program_id(1) == 0)
    def _():
        s_ref[...] = jnp.zeros_like(s_ref)

    per_dir = ((uf_ref, wf_ref, qgf_ref, kdf_ref, af_ref, df_ref, of_ref),
               (ub_ref, wb_ref, qgb_ref, kdb_ref, ab_ref, db_ref, ob_ref))
    units = [(bi, dr, hd) for bi in range(uf_ref.shape[0]) for dr in range(2) for hd in range(HEADS)]
    first = []
    for bi, dr, hd in units:
        u_ref, w_ref, qg_ref = per_dir[dr][:3]
        lo = hd * HEAD_DIM
        s_b = s_ref[bi, dr * HEADS + hd].astype(BF16)
        first.append(_dot(jnp.concatenate([w_ref[bi, :, lo:lo + HEAD_DIM], qg_ref[bi, :, lo:lo + HEAD_DIM]], axis=0), s_b))
    for (bi, dr, hd), both in zip(units, first):
        u_ref, _, _, kd_ref, a_ref, d_ref, o_ref = per_dir[dr]
        unit = dr * HEADS + hd
        lo = hd * HEAD_DIM
        v_new_b = (u_ref[bi, :, lo:lo + HEAD_DIM].astype(F32) - both[:CHUNK]).astype(BF16)
        o_ref[bi, :, lo:lo + HEAD_DIM] = (both[CHUNK:] + _dot(a_ref[bi, :, hd * CHUNK:(hd + 1) * CHUNK], v_new_b)
                                          ).astype(BF16)
        s_ref[bi, unit] = (s_ref[bi, unit] * d_ref[bi, unit:unit + 1, :]
                           + _dot_tn(kd_ref[bi, :, lo:lo + HEAD_DIM], v_new_b))


def _dnscan_call(u, w, qg, kd, attn, adec, n_lat_chunks):
    bsz, _, ltot, _ = u.shape
    nch = ltot // CHUNK
    bb = SCAN_BATCH if bsz % SCAN_BATCH == 0 else 1
    fwd = lambda n: (n + n_lat_chunks) % nch
    bwd = lambda n: nch - 1 - n

    def dspec(wd, dr, order):
        return pl.BlockSpec((bb, None, CHUNK, wd), lambda b, n: (b, dr, order(n), 0))

    def aspec(order):
        return pl.BlockSpec((bb, None, N_DH, HEAD_DIM), lambda b, n: (b, order(n), 0, 0))

    def ospec(order):
        return pl.BlockSpec((bb, CHUNK, KEYW), lambda b, n: (b, order(n), 0))

    in_specs = []
    for dr, order in ((0, fwd), (1, bwd)):
        in_specs += [dspec(KEYW, dr, order), dspec(KEYW, dr, order), dspec(KEYW, dr, order), dspec(KEYW, dr, order),
                     dspec(PACKW, dr, order), aspec(order)]
    return pl.pallas_call(
        _dnscan_kernel,
        grid=(bsz // bb, nch),
        in_specs=in_specs,
        out_specs=[ospec(fwd), ospec(bwd)],
        out_shape=[jax.ShapeDtypeStruct((bsz, ltot, KEYW), BF16)] * 2,
        scratch_shapes=[pltpu.VMEM((bb, N_DH, HEAD_DIM, HEAD_DIM), F32)],
        compiler_params=_cparams(2),
        name="dn_scan",
    )(u, w, qg, kd, attn, adec, u, w, qg, kd, attn, adec)


def _dft_mats(n):
    idx = np.arange(n)
    ang = 2.0 * np.pi * ((idx[:, None] * idx[None, :]) % n) / n
    return np.cos(ang), np.sin(ang)


def _fft1_kernel(n2blk, u_ref, f1_ref, tc_ref, ts_ref, br_ref, bi_ref):
    l1 = u_ref.shape[0]
    rows = l1 * n2blk
    a = _dot(f1_ref[...], u_ref[...].reshape(rows, KEYW).astype(BF16))
    ar, ai = a[:rows], a[rows:]
    tc = jnp.concatenate([tc_ref[...]] * (KEYW // HEAD_DIM), axis=1)
    ts = jnp.concatenate([ts_ref[...]] * (KEYW // HEAD_DIM), axis=1)
    br_ref[...] = (ar * tc + ai * ts).reshape(l1, n2blk, KEYW)
    bi_ref[...] = (ai * tc - ar * ts).reshape(l1, n2blk, KEYW)


def _fft2_kernel(k1blk, nlen, br_ref, bi_ref, f2_ref, wc_ref, y_ref):
    rows = k1blk * nlen
    x = _dot(f2_ref[...], jnp.concatenate([br_ref[...], bi_ref[...]], axis=0).astype(BF16))
    xr, xi = x[:rows].astype(BF16), x[rows:].astype(BF16)
    y = jnp.concatenate(
        [_dot(jnp.concatenate([xr[:, g * HEAD_DIM:(g + 1) * HEAD_DIM], xi[:, g * HEAD_DIM:(g + 1) * HEAD_DIM]],
                              axis=1), wc_ref[...]) for g in range(KEYW // HEAD_DIM)], axis=1)
    y_ref[...] = y.reshape(nlen, k1blk, KEYW)


def _dftctx_kernel(nlen, u_ref, f_ref, wc_ref, y_ref):
    x = _dot(f_ref[...], u_ref[...].astype(BF16))
    xr, xi = x[:nlen].astype(BF16), x[nlen:].astype(BF16)
    for g in range(KEYW // HEAD_DIM):
        gs = slice(g * HEAD_DIM, (g + 1) * HEAD_DIM)
        y_ref[:, gs] = _dot(jnp.concatenate([xr[:, gs], xi[:, gs]], axis=1), wc_ref[...])


def _chan_mat(total_len):
    cc, sc = _dft_mats(HEAD_DIM)
    return jnp.asarray(np.concatenate([cc, sc], axis=0) / math.sqrt(total_len * HEAD_DIM), BF16)


def _fourier_lat_call(fn_cat, l_lat):
    bsz, ltot, _ = fn_cat.shape
    l1 = l_lat // CHUNK
    n2blk = 8
    k1blk = 8 if l1 % 8 == 0 else l1
    c1, s1 = _dft_mats(l1)
    eye_n2 = np.eye(n2blk)
    f1 = jnp.asarray(np.concatenate([np.kron(c1, eye_n2), np.kron(-s1, eye_n2)], axis=0), BF16)
    blk = np.arange(CHUNK // n2blk)[:, None, None]
    k1 = np.arange(l1)[None, :, None]
    n2p = np.arange(n2blk)[None, None, :]
    ang = (2.0 * np.pi * (k1 * (blk * n2blk + n2p)) / l_lat).reshape(CHUNK // n2blk, l1 * n2blk, 1) * np.ones((1, 1, HEAD_DIM))
    tc, ts = jnp.asarray(np.cos(ang), F32), jnp.asarray(np.sin(ang), F32)
    c2, s2 = _dft_mats(CHUNK)
    eye_k1 = np.eye(k1blk)
    kron2 = lambda f: np.einsum("kn,ab->kabn", f, eye_k1).reshape(CHUNK * k1blk, k1blk * CHUNK)
    f2 = jnp.asarray(np.block([[kron2(c2), kron2(s2)], [kron2(-s2), kron2(c2)]]), BF16)
    wc = _chan_mat(l_lat)

    u_view = fn_cat.reshape(bsz, ltot // CHUNK, CHUNK, KEYW)
    bspec = pl.BlockSpec((None, l1, n2blk, KEYW), lambda b, i: (b, 0, i, 0))
    br, bi = pl.pallas_call(
        functools.partial(_fft1_kernel, n2blk),
        grid=(bsz, CHUNK // n2blk),
        in_specs=[bspec,
                  pl.BlockSpec(f1.shape, lambda b, i: (0, 0)),
                  pl.BlockSpec((None, l1 * n2blk, HEAD_DIM), lambda b, i: (i, 0, 0)),
                  pl.BlockSpec((None, l1 * n2blk, HEAD_DIM), lambda b, i: (i, 0, 0))],
        out_specs=[bspec, bspec],
        out_shape=[jax.ShapeDtypeStruct((bsz, l1, CHUNK, KEYW), F32)] * 2,
        compiler_params=_cparams(2),
        name="fourier_stage1",
    )(u_view, f1, tc, ts)

    rspec = pl.BlockSpec((None, CHUNK * k1blk, KEYW), lambda b, i: (b, i, 0))
    y = pl.pallas_call(
        functools.partial(_fft2_kernel, k1blk, CHUNK),
        grid=(bsz, l1 // k1blk),
        in_specs=[rspec, rspec,
                  pl.BlockSpec(f2.shape, lambda b, i: (0, 0)),
                  pl.BlockSpec(wc.shape, lambda b, i: (0, 0))],
        out_specs=pl.BlockSpec((None, CHUNK, k1blk, KEYW), lambda b, i: (b, 0, i, 0)),
        out_shape=jax.ShapeDtypeStruct((bsz, CHUNK, l1, KEYW), F32),
        compiler_params=_cparams(2),
        name="fourier_stage2",
    )(br.reshape(bsz, l_lat, KEYW), bi.reshape(bsz, l_lat, KEYW), f2, wc)
    return y.reshape(bsz, l_lat, KEYW)


def _fourier_ctx_call(fn_cat, l_lat):
    bsz, ltot, _ = fn_cat.shape
    lc = ltot - l_lat
    c, s = _dft_mats(lc)
    f = jnp.asarray(np.concatenate([c, -s], axis=0), BF16)
    wc = _chan_mat(lc)
    return pl.pallas_call(
        functools.partial(_dftctx_kernel, lc),
        grid=(bsz,),
        in_specs=[pl.BlockSpec((None, lc, KEYW), lambda b: (b, l_lat // lc, 0)),
                  pl.BlockSpec(f.shape, lambda b: (0, 0)),
                  pl.BlockSpec(wc.shape, lambda b: (0, 0))],
        out_specs=pl.BlockSpec((None, lc, KEYW), lambda b: (b, 0, 0)),
        out_shape=jax.ShapeDtypeStruct((bsz, lc, KEYW), F32),
        compiler_params=_cparams(1),
        name="fourier_ctx",
    )(fn_cat, f, wc)


def _merge_kernel(nt_lat, x_ref, ctx_ref, mod_ref, of_ref, ob_ref, z_ref, sc_ref, fl_ref, fc_ref,
                  g_ref, wgate_ref, bgate_ref, og_ref, wdn_ref, wsc_ref, wfn_ref, wo_ref, o_ref):
    is_lat = pl.program_id(1) < nt_lat
    n_rows = x_ref.shape[0]
    xs = [jnp.where(is_lat, x_ref[s], ctx_ref[s]) for s in range(n_rows)]
    hs = [_modulated(xs[s], g_ref, mod_ref[s, 0:1, :], mod_ref[s, 1:2, :]).astype(BF16) for s in range(n_rows)]
    logits = [_dot(h, wgate_ref[...]) for h in hs]
    branch_in = []
    for s in range(n_rows):
        o = of_ref[s].astype(F32) + ob_ref[s].astype(F32)
        z = z_ref[s]
        dn_parts = []
        for hd in range(HEADS):
            sl = slice(hd * HEAD_DIM, (hd + 1) * HEAD_DIM)
            dn_parts.append(_rms(o[:, sl], HEAD_DIM) * og_ref[...] * _silu(z[:, sl]))
        fn = jnp.where(is_lat, fl_ref[s], fc_ref[s])
        branch_in.append((jnp.concatenate(dn_parts, axis=1).astype(BF16), sc_ref[s].astype(BF16), fn.astype(BF16)))
    ys = [(_dot(dn, wdn_ref[...]), _dot(sc, wsc_ref[...]), _dot(fn, wfn_ref[...])) for dn, sc, fn in branch_in]
    merged = []
    for s in range(n_rows):
        gates = jax.nn.sigmoid(logits[s] + bgate_ref[...])
        y_dn, y_sc, y_fn = ys[s]
        merged.append((gates[:, :D_MODEL] * y_dn + gates[:, D_MODEL:2 * D_MODEL] * y_sc
                       + gates[:, 2 * D_MODEL:] * y_fn).astype(BF16))
    mixes = [_dot(m, wo_ref[...]) for m in merged]
    for s in range(n_rows):
        o_ref[s] = xs[s] + mod_ref[s, 2:3, :] * mixes[s]


def _merge_call(x_src, ctx_src, ctx_blk, mod_sel, n1g, o_f, o_b, z, sc, fn_lat, fn_ctx, wgate, bgate, og, wdn, wsc,
                wfn, wo, nt_lat, nt_run):
    bsz, _, d = x_src.shape
    ltot = (nt_lat + 1) * TILE
    bb = ROW_BATCH if bsz % ROW_BATCH == 0 else 1
    tok = lambda w: pl.BlockSpec((bb, TILE, w), lambda b, j: (b, j, 0))
    consts = (n1g, wgate, bgate, og, wdn, wsc, wfn, wo)
    return pl.pallas_call(
        functools.partial(_merge_kernel, nt_lat),
        grid=(bsz // bb, nt_run),
        in_specs=_x_specs(bb, nt_lat, ctx_blk, d) + [_mod_spec(bb, nt_lat, d),
                  tok(KEYW), tok(KEYW), tok(KEYW), tok(KEYW),
                  pl.BlockSpec((bb, TILE, KEYW), lambda b, j: (b, jnp.minimum(j, nt_lat - 1), 0)),
                  pl.BlockSpec((bb, TILE, KEYW), lambda b, j: (b, 0, 0))] + [_const_spec(a) for a in consts],
        out_specs=tok(d),
        out_shape=jax.ShapeDtypeStruct((bsz, ltot, d), F32),
        compiler_params=_cparams(2),
        name="branch_merge",
    )(x_src, ctx_src, mod_sel, o_f, o_b, z, sc, fn_lat, fn_ctx, *consts)


def _ffn_kernel(final, x_ref, mod_ref, g_ref, win_ref, wout_ref, fg_ref, o_ref):
    n_rows = x_ref.shape[0]
    hs = [_modulated(x_ref[s], g_ref, mod_ref[s, 3:4, :], mod_ref[s, 4:5, :]).astype(BF16) for s in range(n_rows)]
    abs_ = [_dot(h, win_ref[...]) for h in hs]
    acts = [(_silu(ab[:, :D_FF]) * ab[:, D_FF:]).astype(BF16) for ab in abs_]
    downs = [_dot(act, wout_ref[...]) for act in acts]
    for s in range(n_rows):
        y = x_ref[s] + mod_ref[s, 5:6, :] * downs[s]
        if final:
            y = _rms(y, D_MODEL) * fg_ref[...]
        o_ref[s] = y


def _ffn_call(x_cat, mod_sel, n2g, win, wout, fg, nt_lat, nt_run, final):
    bsz, ltot, d = x_cat.shape
    out_len = nt_run * TILE if final else ltot
    bb = ROW_BATCH if bsz % ROW_BATCH == 0 else 1
    tok = pl.BlockSpec((bb, TILE, d), lambda b, j: (b, j, 0))
    consts = (n2g, win, wout, fg)
    return pl.pallas_call(
        functools.partial(_ffn_kernel, final),
        grid=(bsz // bb, nt_run),
        in_specs=[tok, _mod_spec(bb, nt_lat, d)] + [_const_spec(a) for a in consts],
        out_specs=tok,
        out_shape=jax.ShapeDtypeStruct((bsz, out_len, d), F32),
        compiler_params=_cparams(2),
        name="swiglu_final" if final else "swiglu",
    )(x_cat, mod_sel, *consts)


def kernel(x, c, ctx, c_ctx, w_mod, b_mod, norm1_g, w_in, dn_conv_w, dn_a_log, dn_dt_bias, dn_onorm_g, w_dn_out,
           sc_conv_w, w_sc_out, w_fn_out, w_gate, b_gate, w_o, norm2_g, w_ffn_in, w_ffn_out, final_g):
    bsz, l_lat, d = x.shape
    lc = ctx.shape[1]
    assert d == D_MODEL and lc == TILE and l_lat % TILE == 0 and w_ffn_in.shape[-1] == 2 * D_FF
    depth = w_mod.shape[0]
    nt_lat = l_lat // TILE
    nt_all = nt_lat + 1
    n_lat_chunks = l_lat // CHUNK

    rows = ((bsz + 1 + 7) // 8) * 8
    c_rows = jnp.zeros((rows, d), F32).at[:bsz].set(c).at[bsz].set(c_ctx)
    mod = _mod_call(c_rows, w_mod, b_mod).reshape(depth, rows, 6, d)
    mod_sel = jnp.stack([mod[:, :bsz], jnp.broadcast_to(mod[:, bsz:bsz + 1], (depth, bsz, 6, d))], axis=2)

    x_src, ctx_src, ctx_blk = x, ctx, 0
    row = lambda a: a.reshape(1, -1)
    lane_pad = lambda a: jnp.zeros((1, HEAD_DIM), F32).at[0, N_DH:2 * N_DH].set(a.reshape(-1))
    for layer in range(depth):
        last = layer == depth - 1
        wl = w_in[layer]
        qkv_w = 3 * KEYW
        o_z, o_ba, o_sc, o_fn = qkv_w, qkv_w + KEYW, qkv_w + KEYW + 2 * N_DH, qkv_w + KEYW + 2 * N_DH + 3 * KEYW
        wba = jnp.zeros((d, HEAD_DIM), F32).at[:, :2 * N_DH].set(wl[:, o_ba:o_sc])
        wparts = tuple(a.astype(BF16) for a in (wl[:, :o_z], wl[:, o_z:o_ba], wba, wl[:, o_sc:o_fn], wl[:, o_fn:]))
        q, k, v, z, bg, sc, fn = _inproj_call(
            x_src, ctx_src, ctx_blk, mod_sel[layer], row(norm1_g[layer]), wparts, dn_conv_w[layer], sc_conv_w[layer],
            lane_pad(dn_a_log[layer]), lane_pad(dn_dt_bias[layer]), nt_lat)
        u, w, qg, kd, attn, adec = _dnprep_call(q, k, v, bg)
        o_f, o_b = _dnscan_call(u, w, qg, kd, attn, adec, n_lat_chunks)
        fn_lat = _fourier_lat_call(fn, l_lat)
        nt_run = nt_lat if last else nt_all
        fn_ctx = jnp.zeros((bsz, lc, KEYW), F32) if last else _fourier_ctx_call(fn, l_lat)
        x_cat = _merge_call(
            x_src, ctx_src, ctx_blk, mod_sel[layer], row(norm1_g[layer]), o_f, o_b, z, sc, fn_lat, fn_ctx,
            w_gate[layer].astype(BF16), row(b_gate[layer]), row(dn_onorm_g[layer]),
            w_dn_out[layer].astype(BF16), w_sc_out[layer].astype(BF16), w_fn_out[layer].astype(BF16),
            w_o[layer].astype(BF16), nt_lat, nt_run)
        x_cat = _ffn_call(x_cat, mod_sel[layer], row(norm2_g[layer]), w_ffn_in[layer].astype(BF16),
                          w_ffn_out[layer].astype(BF16), row(final_g), nt_lat, nt_run, last)
        x_src, ctx_src, ctx_blk = x_cat, x_cat, nt_lat
    return x_cat
```

```python
import functools
import math

import numpy as np
import jax
import jax.numpy as jnp
from jax import lax
from jax.experimental import pallas as pl
from jax.experimental.pallas import tpu as pltpu

F32 = jnp.float32
BF16 = jnp.bfloat16
HIGHEST = lax.Precision.HIGHEST

EPS = 1e-6
D_MODEL = 1024
DEPTH = 2
HEADS = 4
HEAD_DIM = 128
KEYW = HEADS * HEAD_DIM
CHUNK = 64
D_FF = 2816
TILE = 256
N_DH = 2 * HEADS
PACKW = HEADS * CHUNK
PREP_CHUNKS = 6
SCAN_BATCH = 8
ROW_BATCH = 2
VMEM_LIMIT = 56 * 1024 * 1024


def _cparams(n_axes):
    return pltpu.CompilerParams(dimension_semantics=("arbitrary",) * n_axes, vmem_limit_bytes=VMEM_LIMIT)


def _dot(a, b):
    return jnp.dot(a, b, preferred_element_type=F32)


def _dot_hi(a, b):
    return jnp.dot(a, b, preferred_element_type=F32, precision=HIGHEST)


def _dot_nt(a, b):
    return lax.dot_general(a, b, (((1,), (1,)), ((), ())), preferred_element_type=F32)


def _dot_tn(a, b):
    return lax.dot_general(a, b, (((0,), (0,)), ((), ())), preferred_element_type=F32)


def _silu(t):
    return t * jax.nn.sigmoid(t)


def _rms(t, axis_size):
    return t * lax.rsqrt(jnp.sum(t * t, axis=-1, keepdims=True) * (1.0 / axis_size) + EPS)


def _mod_kernel(c_ref, w_ref, b_ref, o_ref):
    o_ref[...] = _dot_hi(_silu(c_ref[...]), w_ref[...]) + b_ref[...]


def _mod_call(c_rows, w_mod, b_mod):
    depth, d, n = w_mod.shape
    rows = c_rows.shape[0]
    tn = 1536
    return pl.pallas_call(
        _mod_kernel,
        grid=(depth, n // tn),
        in_specs=[
            pl.BlockSpec((rows, d), lambda l, j: (0, 0)),
            pl.BlockSpec((None, d, tn), lambda l, j: (l, 0, j)),
            pl.BlockSpec((None, 1, tn), lambda l, j: (l, 0, j)),
        ],
        out_specs=pl.BlockSpec((None, rows, tn), lambda l, j: (l, 0, j)),
        out_shape=jax.ShapeDtypeStruct((depth, rows, n), F32),
        compiler_params=_cparams(2),
        name="mod_vectors",
    )(c_rows, w_mod, b_mod.reshape(depth, 1, n))


def _row_masks(is_lat):
    pos = lax.broadcasted_iota(jnp.int32, (TILE, 1), 0)
    row_mask = jnp.where(is_lat, CHUNK - 1, TILE - 1)
    in_row = pos & row_mask
    return jnp.where(in_row == 0, 0.0, 1.0).astype(F32), jnp.where(in_row == row_mask, 0.0, 1.0).astype(F32)


def _mask_rows(t, mask, first):
    pieces = []
    for g in range(TILE // CHUNK):
        lo, hi = g * CHUNK, (g + 1) * CHUNK
        if first:
            pieces += [t[lo:lo + 8] * mask[lo:lo + 8], t[lo + 8:hi]]
        else:
            pieces += [t[lo:hi - 8], t[hi - 8:hi] * mask[hi - 8:hi]]
    return jnp.concatenate(pieces, axis=0)


def _conv3(t, w_ref, m_prev, m_next):
    prev = _mask_rows(pltpu.roll(t, 1, 0), m_prev, True)
    nxt = _mask_rows(pltpu.roll(t, TILE - 1, 0), m_next, False)
    return prev * w_ref[0:1, :] + t * w_ref[1:2, :] + nxt * w_ref[2:3, :]


def _modulated(x, g_ref, shift, scale):
    return _rms(x, D_MODEL) * (g_ref[...] * (1.0 + scale)) + shift


def _inproj_kernel(nt_lat, x_ref, ctx_ref, mod_ref, g_ref, wqkv_ref, wz_ref, wba_ref, wsc_ref, wfn_ref,
                   dnconv_ref, scconv_ref, alog_ref, dtb_ref,
                   q_ref, k_ref, v_ref, z_ref, bg_ref, sc_ref, fn_ref):
    is_lat = pl.program_id(1) < nt_lat
    m_prev, m_next = _row_masks(is_lat)
    n_rows = x_ref.shape[0]
    hs = []
    for s in range(n_rows):
        x = jnp.where(is_lat, x_ref[s], ctx_ref[s])
        hs.append(_modulated(x, g_ref, mod_ref[s, 0:1, :], mod_ref[s, 1:2, :]).astype(BF16))
    projs = [tuple(_dot(h, w_ref[...]) for w_ref in (wqkv_ref, wz_ref, wba_ref, wsc_ref, wfn_ref)) for h in hs]
    for s in range(n_rows):
        p_qkv, p_z, ba, scp, p_fn = projs[s]
        qkv = _silu(_conv3(p_qkv, dnconv_ref, m_prev, m_next))
        for hd in range(HEADS):
            lo = hd * HEAD_DIM
            qh = qkv[:, lo:lo + HEAD_DIM]
            kh = qkv[:, KEYW + lo:KEYW + lo + HEAD_DIM]
            q_ref[s, :, lo:lo + HEAD_DIM] = (qh * lax.rsqrt(jnp.sum(qh * qh, axis=-1, keepdims=True) + EPS)
                                             * (HEAD_DIM ** -0.5))
            k_ref[s, :, lo:lo + HEAD_DIM] = kh * lax.rsqrt(jnp.sum(kh * kh, axis=-1, keepdims=True) + EPS)
        v_ref[s] = qkv[:, 2 * KEYW:]
        z_ref[s] = p_z

        sp_in = ba + dtb_ref[...]
        softplus = jnp.maximum(sp_in, 0.0) + jnp.log1p(jnp.exp(-jnp.abs(sp_in)))
        lane = lax.broadcasted_iota(jnp.int32, ba.shape, 1)
        g = -jnp.exp(alog_ref[...]) * softplus
        in_chunk = lax.broadcasted_iota(jnp.int32, (TILE, 1), 0) & (CHUNK - 1)
        pre, suf = g, g
        step = 1
        while step < CHUNK:
            pre = pre + jnp.where(in_chunk >= step, pltpu.roll(pre, step, 0), 0.0)
            suf = suf + jnp.where(in_chunk < CHUNK - step, pltpu.roll(suf, TILE - step, 0), 0.0)
            step *= 2
        gc = jnp.where(lane < N_DH + HEADS, pre, suf)
        bg_ref[s] = jnp.where(lane < N_DH, jax.nn.sigmoid(ba), gc)

        sc_ref[s] = scp[:, :KEYW] * _conv3(scp[:, KEYW:2 * KEYW] * scp[:, 2 * KEYW:], scconv_ref, m_prev, m_next)
        fn_ref[s] = p_fn


def _const_spec(a):
    return pl.BlockSpec(a.shape, lambda b, j: (0,) * a.ndim, pipeline_mode=pl.Buffered(1))


def _mod_spec(bb, nt_lat, d):
    return pl.BlockSpec((bb, None, 6, d), lambda b, j: (b, jnp.where(j < nt_lat, 0, 1), 0, 0))


def _x_specs(bb, nt_lat, ctx_blk, d):
    return [pl.BlockSpec((bb, TILE, d), lambda b, j: (b, jnp.minimum(j, nt_lat - 1), 0)),
            pl.BlockSpec((bb, TILE, d), lambda b, j: (b, ctx_blk, 0))]


def _inproj_call(x_src, ctx_src, ctx_blk, mod_sel, n1g, wparts, dnconv, scconv, alog_row, dtb_row, nt_lat):
    bsz, _, d = x_src.shape
    nt = nt_lat + 1
    ltot = nt * TILE
    bb = ROW_BATCH if bsz % ROW_BATCH == 0 else 1
    wqkv, wz, wba, wsc, wfn = wparts
    tok = lambda w: pl.BlockSpec((bb, TILE, w), lambda b, j: (b, j, 0))
    out_w = (KEYW, KEYW, KEYW, KEYW, HEAD_DIM, KEYW, KEYW)
    consts = (n1g, wqkv, wz, wba, wsc, wfn, dnconv, scconv, alog_row, dtb_row)
    return pl.pallas_call(
        functools.partial(_inproj_kernel, nt_lat),
        grid=(bsz // bb, nt),
        in_specs=_x_specs(bb, nt_lat, ctx_blk, d) + [_mod_spec(bb, nt_lat, d)] + [_const_spec(a) for a in consts],
        out_specs=[tok(w) for w in out_w],
        out_shape=[jax.ShapeDtypeStruct((bsz, ltot, w), F32) for w in out_w],
        compiler_params=_cparams(2),
        name="in_proj",
    )(x_src, ctx_src, mod_sel, *consts)


def _split2(t):
    hi = t.astype(BF16)
    return hi, (t - hi.astype(F32)).astype(BF16)


def _mm3(lhs, rhs):
    m = lhs.shape[0]
    lh, ll = _split2(lhs)
    rh, rl = _split2(rhs)
    top = _dot(jnp.concatenate([lh, ll], axis=0), rh)
    return top[:m] + top[m:] + _dot(lh, rl)


def _dnprep_kernel(cps, q_ref, k_ref, v_ref, bg_ref, u_ref, w_ref, qg_ref, kd_ref, attn_ref, adec_ref):
    ri = lax.broadcasted_iota(jnp.int32, (CHUNK, PACKW), 0)
    cj = lax.broadcasted_iota(jnp.int32, (CHUNK, PACKW), 1) & (CHUNK - 1)
    eye_p = jnp.where(ri == cj, 1.0, 0.0).astype(F32)
    incl_m = ((ri >= cj), (ri <= cj))
    strict_m = ((ri > cj), (ri < cj))

    def half_masked(t, hd):
        lane = lax.broadcasted_iota(jnp.int32, t.shape, 1)
        keep = (lane < CHUNK) if hd % 2 == 0 else (lane >= CHUNK)
        return jnp.where(keep, t, jnp.zeros((), BF16))

    def blockdiag(t):
        z = jnp.zeros((CHUNK, HEAD_DIM), BF16)
        blocks = []
        for hd in range(HEADS):
            piece = half_masked(t[:, (hd // 2) * HEAD_DIM:(hd // 2 + 1) * HEAD_DIM], hd)
            blocks.append(jnp.concatenate([piece, z] if hd < 2 else [z, piece], axis=1))
        return jnp.concatenate(blocks, axis=0)

    def mm3_all(lhs_list, rhs_list):
        lsp = [_split2(l) for l in lhs_list]
        rsp = [tuple(blockdiag(p) for p in _split2(r)) for r in rhs_list]
        tops = [_dot(jnp.concatenate([lh, ll], axis=0), rh) for (lh, ll), (rh, _) in zip(lsp, rsp)]
        lows = [_dot(lh, rl) for (lh, _), (_, rl) in zip(lsp, rsp)]
        return [t[:l.shape[0]] + t[l.shape[0]:] + lo for t, lo, l in zip(tops, lows, lhs_list)]

    chunk_rows = [slice(ci * CHUNK, (ci + 1) * CHUNK) for ci in range(cps)]
    qs = [q_ref[r, :] for r in chunk_rows]
    ks = [k_ref[r, :] for r in chunk_rows]
    vs = [v_ref[r, :] for r in chunk_rows]

    lo_half = lax.broadcasted_iota(jnp.int32, (CHUNK, HEAD_DIM), 1) < CHUNK
    expanded = {}
    for ci, r in enumerate(chunk_rows):
        bg = bg_ref[r, :]
        for dr in range(2):
            cols = [(jnp.broadcast_to(bg[:, N_DH + dr * HEADS + hd:N_DH + dr * HEADS + hd + 1], (CHUNK, HEAD_DIM)),
                     jnp.broadcast_to(bg[:, dr * HEADS + hd:dr * HEADS + hd + 1], (CHUNK, HEAD_DIM)))
                    for hd in range(HEADS)]
            gcol = jnp.concatenate([c[0] for c in cols], axis=1)
            beta = jnp.concatenate([c[1] for c in cols], axis=1)
            gcol_p = jnp.concatenate([jnp.where(lo_half, cols[0][0], cols[1][0]),
                                      jnp.where(lo_half, cols[2][0], cols[3][0])], axis=1)
            expanded[ci, dr] = (gcol, beta, gcol_p)

    def parts(ci, dr):
        return expanded[ci, dr]

    m1s = []
    for ci in range(cps):
        k = ks[ci]
        k_t = jnp.concatenate([k, k], axis=0).T.astype(BF16)
        zk = jnp.zeros((HEAD_DIM, HEAD_DIM), BF16)
        bd_rows = []
        for hd in range(HEADS):
            piece = half_masked(k_t[hd * HEAD_DIM:(hd + 1) * HEAD_DIM, :], hd)
            bd_rows.append(jnp.concatenate([piece, zk] if hd < 2 else [zk, piece], axis=1))
        bd_k = jnp.concatenate(bd_rows, axis=0)
        kbs = [k * parts(ci, dr)[1] for dr in range(2)]
        m1s.append(_dot(jnp.concatenate(kbs + [qs[ci]], axis=0).astype(BF16), bd_k))

    chains = [(ci, dr) for ci in range(cps) for dr in range(2)]
    decays, neg_as = [], []
    for ci, dr in chains:
        gcol_p = parts(ci, dr)[2]
        grow_p = jnp.sum(gcol_p * eye_p, axis=0, keepdims=True)
        decay = jnp.exp(jnp.where(incl_m[dr], gcol_p - grow_p, -jnp.inf))
        decays.append(decay)
        neg_as.append(jnp.where(strict_m[dr], -(m1s[ci][dr * CHUNK:(dr + 1) * CHUNK] * decay), 0.0))

    p_accs = [eye_p + a for a in neg_as]
    q_pows = mm3_all(neg_as, neg_as)
    for _ in range(4):
        boths = mm3_all([jnp.concatenate([qp, pa], axis=0) for qp, pa in zip(q_pows, p_accs)], q_pows)
        q_pows = [b[:CHUNK] for b in boths]
        p_accs = [pa + b[CHUNK:] for pa, b in zip(p_accs, boths)]
    lasts = mm3_all(p_accs, q_pows)
    p_accs = [pa + la for pa, la in zip(p_accs, lasts)]

    uws, egs = [], []
    for (ci, dr), t_inv in zip(chains, p_accs):
        gcol, beta, _ = parts(ci, dr)
        eg = jnp.exp(gcol)
        egs.append(eg)
        rhs = jnp.concatenate([vs[ci] * beta, ks[ci] * (beta * eg)], axis=1).astype(BF16)
        zr = jnp.zeros((CHUNK, HEAD_DIM), BF16)
        n_tiles = 2 * KEYW // HEAD_DIM
        bd_r = jnp.concatenate([
            jnp.concatenate([rhs[:, t * HEAD_DIM:(t + 1) * HEAD_DIM] if t % HEADS == hd else zr
                             for t in range(n_tiles)], axis=1)
            for hd in range(HEADS)], axis=0)
        uws.append(_dot(t_inv.astype(BF16), bd_r))

    for (ci, dr), uw, eg, decay in zip(chains, uws, egs, decays):
        rows = chunk_rows[ci]
        gcol = parts(ci, dr)[0]
        g_last = gcol[CHUNK - 1:CHUNK, :] if dr == 0 else gcol[0:1, :]
        u_ref[dr, rows, :] = uw[:, :KEYW].astype(BF16)
        w_ref[dr, rows, :] = uw[:, KEYW:].astype(BF16)
        qg_ref[dr, rows, :] = (qs[ci] * eg).astype(BF16)
        kd_ref[dr, rows, :] = (ks[ci] * jnp.exp(g_last - gcol)).astype(BF16)
        attn_ref[dr, rows, :] = (m1s[ci][2 * CHUNK:] * decay).astype(BF16)
        a_dec = jnp.exp(g_last)
        for hd in range(HEADS):
            unit = dr * HEADS + hd
            adec_ref[ci, unit:unit + 1, :] = a_dec[:, hd * HEAD_DIM:(hd + 1) * HEAD_DIM]


def _dnprep_call(q, k, v, bg):
    bsz, ltot, _ = q.shape
    nch = ltot // CHUNK
    cps = PREP_CHUNKS
    assert nch % cps == 0
    tok = lambda w: pl.BlockSpec((None, cps * CHUNK, w), lambda b, n: (b, n, 0))
    dtok = lambda w: pl.BlockSpec((None, 2, cps * CHUNK, w), lambda b, n: (b, 0, n, 0))
    return pl.pallas_call(
        functools.partial(_dnprep_kernel, cps),
        grid=(bsz, nch // cps),
        in_specs=[tok(KEYW), tok(KEYW), tok(KEYW), tok(HEAD_DIM)],
        out_specs=[dtok(KEYW), dtok(KEYW), dtok(KEYW), dtok(KEYW), dtok(PACKW),
                   pl.BlockSpec((None, cps, N_DH, HEAD_DIM), lambda b, n: (b, n, 0, 0))],
        out_shape=[jax.ShapeDtypeStruct((bsz, 2, ltot, KEYW), BF16),
                   jax.ShapeDtypeStruct((bsz, 2, ltot, KEYW), BF16),
                   jax.ShapeDtypeStruct((bsz, 2, ltot, KEYW), BF16),
                   jax.ShapeDtypeStruct((bsz, 2, ltot, KEYW), BF16),
                   jax.ShapeDtypeStruct((bsz, 2, ltot, PACKW), BF16),
                   jax.ShapeDtypeStruct((bsz, nch, N_DH, HEAD_DIM), F32)],
        compiler_params=_cparams(2),
        name="dn_prep",
    )(q, k, v, bg)


def _dnscan_kernel(uf_ref, wf_ref, qgf_ref, kdf_ref, af_ref, df_ref,
                   ub_ref, wb_ref, qgb_ref, kdb_ref, ab_ref, db_ref,
                   of_ref, ob_ref, s_ref):
    @pl.when(pl.program_id(1) == 0)
    def _():
        s_ref[...] = jnp.zeros_like(s_ref)

    per_dir = ((uf_ref, wf_ref, qgf_ref, kdf_ref, af_ref, df_ref, of_ref),
               (ub_ref, wb_ref, qgb_ref, kdb_ref, ab_ref, db_ref, ob_ref))
    units = [(bi, dr, hd) for bi in range(uf_ref.shape[0]) for dr in range(2) for hd in range(HEADS)]
    first = []
    for bi, dr, hd in units:
        u_ref, w_ref, qg_ref = per_dir[dr][:3]
        lo = hd * HEAD_DIM
        s_b = s_ref[bi, dr * HEADS + hd].astype(BF16)
        first.append(_dot(jnp.concatenate([w_ref[bi, :, lo:lo + HEAD_DIM], qg_ref[bi, :, lo:lo + HEAD_DIM]], axis=0), s_b))
    for (bi, dr, hd), both in zip(units, first):
        u_ref, _, _, kd_ref, a_ref, d_ref, o_ref = per_dir[dr]
        unit = dr * HEADS + hd
        lo = hd * HEAD_DIM
        v_new_b = (u_ref[bi, :, lo:lo + HEAD_DIM].astype(F32) - both[:CHUNK]).astype(BF16)
        o_ref[bi, :, lo:lo + HEAD_DIM] = (both[CHUNK:] + _dot(a_ref[bi, :, hd * CHUNK:(hd + 1) * CHUNK], v_new_b)
                                          ).astype(BF16)
        s_ref[bi, unit] = (s_ref[bi, unit] * d_ref[bi, unit:unit + 1, :]
                           + _dot_tn(kd_ref[bi, :, lo:lo + HEAD_DIM], v_new_b))


def _dnscan_call(u, w, qg, kd, attn, adec, n_lat_chunks):
    bsz, _, ltot, _ = u.shape
    nch = ltot // CHUNK
    bb = SCAN_BATCH if bsz % SCAN_BATCH == 0 else 1
    fwd = lambda n: (n + n_lat_chunks) % nch
    bwd = lambda n: nch - 1 - n

    def dspec(wd, dr, order):
        return pl.BlockSpec((bb, None, CHUNK, wd), lambda b, n: (b, dr, order(n), 0))

    def aspec(order):
        return pl.BlockSpec((bb, None, N_DH, HEAD_DIM), lambda b, n: (b, order(n), 0, 0))

    def ospec(order):
        return pl.BlockSpec((bb, CHUNK, KEYW), lambda b, n: (b, order(n), 0))

    in_specs = []
    for dr, order in ((0, fwd), (1, bwd)):
        in_specs += [dspec(KEYW, dr, order), dspec(KEYW, dr, order), dspec(KEYW, dr, order), dspec(KEYW, dr, order),
                     dspec(PACKW, dr, order), aspec(order)]
    return pl.pallas_call(
        _dnscan_kernel,
        grid=(bsz // bb, nch),
        in_specs=in_specs,
        out_specs=[ospec(fwd), ospec(bwd)],
        out_shape=[jax.ShapeDtypeStruct((bsz, ltot, KEYW), BF16)] * 2,
        scratch_shapes=[pltpu.VMEM((bb, N_DH, HEAD_DIM, HEAD_DIM), F32)],
        compiler_params=_cparams(2),
        name="dn_scan",
    )(u, w, qg, kd, attn, adec, u, w, qg, kd, attn, adec)


def _dft_mats(n):
    idx = np.arange(n)
    ang = 2.0 * np.pi * ((idx[:, None] * idx[None, :]) % n) / n
    return np.cos(ang), np.sin(ang)


def _fft1_kernel(n2blk, u_ref, f1_ref, tc_ref, ts_ref, br_ref, bi_ref):
    l1 = u_ref.shape[0]
    rows = l1 * n2blk
    a = _dot(f1_ref[...], u_ref[...].reshape(rows, KEYW).astype(BF16))
    ar, ai = a[:rows], a[rows:]
    tc = jnp.concatenate([tc_ref[...].reshape(rows, HEAD_DIM)] * (KEYW // HEAD_DIM), axis=1)
    ts = jnp.concatenate([ts_ref[...].reshape(rows, HEAD_DIM)] * (KEYW // HEAD_DIM), axis=1)
    br_ref[...] = (ar * tc + ai * ts).reshape(n2blk, l1, KEYW)
    bi_ref[...] = (ai * tc - ar * ts).reshape(n2blk, l1, KEYW)


def _fft2_kernel(k1blk, nlen, br_ref, bi_ref, f2_ref, wc_ref, y_ref):
    rows = k1blk * nlen
    x = _dot(f2_ref[...], jnp.concatenate([br_ref[...].reshape(rows, KEYW), bi_ref[...].reshape(rows, KEYW)],
                                          axis=0).astype(BF16))
    xr, xi = x[:rows].astype(BF16), x[rows:].astype(BF16)
    y = jnp.concatenate(
        [_dot(jnp.concatenate([xr[:, g * HEAD_DIM:(g + 1) * HEAD_DIM], xi[:, g * HEAD_DIM:(g + 1) * HEAD_DIM]],
                              axis=1), wc_ref[...]) for g in range(KEYW // HEAD_DIM)], axis=1)
    y_ref[...] = y.reshape(nlen, k1blk, KEYW)


def _dftctx_kernel(nlen, u_ref, f_ref, wc_ref, y_ref):
    x = _dot(f_ref[...], u_ref[...].astype(BF16))
    xr, xi = x[:nlen].astype(BF16), x[nlen:].astype(BF16)
    for g in range(KEYW // HEAD_DIM):
        gs = slice(g * HEAD_DIM, (g + 1) * HEAD_DIM)
        y_ref[:, gs] = _dot(jnp.concatenate([xr[:, gs], xi[:, gs]], axis=1), wc_ref[...])


def _chan_mat(total_len):
    cc, sc = _dft_mats(HEAD_DIM)
    return jnp.asarray(np.concatenate([cc, sc], axis=0) / math.sqrt(total_len * HEAD_DIM), BF16)


def _fourier_lat_call(fn_cat, l_lat):
    bsz, ltot, _ = fn_cat.shape
    l1 = l_lat // CHUNK
    n2blk = 8
    k1blk = 8 if l1 % 8 == 0 else l1
    c1, s1 = _dft_mats(l1)
    eye_n2 = np.eye(n2blk)
    kron1 = lambda f: np.einsum("kn,ab->aknb", f, eye_n2).reshape(n2blk * l1, l1 * n2blk)
    f1 = jnp.asarray(np.concatenate([kron1(c1), kron1(-s1)], axis=0), BF16)
    n2 = np.arange(CHUNK)[:, None, None]
    k1 = np.arange(l1)[None, :, None]
    ang = 2.0 * np.pi * (n2 * k1) / l_lat * np.ones((1, 1, HEAD_DIM))
    tc, ts = jnp.asarray(np.cos(ang), F32), jnp.asarray(np.sin(ang), F32)
    c2, s2 = _dft_mats(CHUNK)
    eye_k1 = np.eye(k1blk)
    kron2 = lambda f: np.kron(f, eye_k1)
    f2 = jnp.asarray(np.block([[kron2(c2), kron2(s2)], [kron2(-s2), kron2(c2)]]), BF16)
    wc = _chan_mat(l_lat)

    u_view = fn_cat.reshape(bsz, ltot // CHUNK, CHUNK, KEYW)
    mid_spec = pl.BlockSpec((None, n2blk, l1, KEYW), lambda b, i: (b, i, 0, 0))
    br, bi = pl.pallas_call(
        functools.partial(_fft1_kernel, n2blk),
        grid=(bsz, CHUNK // n2blk),
        in_specs=[pl.BlockSpec((None, l1, n2blk, KEYW), lambda b, i: (b, 0, i, 0)),
                  pl.BlockSpec(f1.shape, lambda b, i: (0, 0)),
                  pl.BlockSpec((n2blk, l1, HEAD_DIM), lambda b, i: (i, 0, 0)),
                  pl.BlockSpec((n2blk, l1, HEAD_DIM), lambda b, i: (i, 0, 0))],
        out_specs=[mid_spec, mid_spec],
        out_shape=[jax.ShapeDtypeStruct((bsz, CHUNK, l1, KEYW), F32)] * 2,
        compiler_params=_cparams(2),
        name="fourier_stage1",
    )(u_view, f1, tc, ts)

    rspec = pl.BlockSpec((None, CHUNK, k1blk, KEYW), lambda b, i: (b, 0, i, 0))
    y = pl.pallas_call(
        functools.partial(_fft2_kernel, k1blk, CHUNK),
        grid=(bsz, l1 // k1blk),
        in_specs=[rspec, rspec,
                  pl.BlockSpec(f2.shape, lambda b, i: (0, 0)),
                  pl.BlockSpec(wc.shape, lambda b, i: (0, 0))],
        out_specs=rspec,
        out_shape=jax.ShapeDtypeStruct((bsz, CHUNK, l1, KEYW), F32),
        compiler_params=_cparams(2),
        name="fourier_stage2",
    )(br, bi, f2, wc)
    return y.reshape(bsz, l_lat, KEYW)


def _fourier_ctx_call(fn_cat, l_lat):
    bsz, ltot, _ = fn_cat.shape
    lc = ltot - l_lat
    c, s = _dft_mats(lc)
    f = jnp.asarray(np.concatenate([c, -s], axis=0), BF16)
    wc = _chan_mat(lc)
    return pl.pallas_call(
        functools.partial(_dftctx_kernel, lc),
        grid=(bsz,),
        in_specs=[pl.BlockSpec((None, lc, KEYW), lambda b: (b, l_lat // lc, 0)),
                  pl.BlockSpec(f.shape, lambda b: (0, 0)),
                  pl.BlockSpec(wc.shape, lambda b: (0, 0))],
        out_specs=pl.BlockSpec((None, lc, KEYW), lambda b: (b, 0, 0)),
        out_shape=jax.ShapeDtypeStruct((bsz, lc, KEYW), F32),
        compiler_params=_cparams(1),
        name="fourier_ctx",
    )(fn_cat, f, wc)


def _merge_kernel(nt_lat, x_ref, ctx_ref, mod_ref, of_ref, ob_ref, z_ref, sc_ref, fl_ref, fc_ref,
                  g_ref, wgate_ref, bgate_ref, og_ref, wdn_ref, wsc_ref, wfn_ref, wo_ref, o_ref):
    is_lat = pl.program_id(1) < nt_lat
    n_rows = x_ref.shape[0]
    xs = [jnp.where(is_lat, x_ref[s], ctx_ref[s]) for s in range(n_rows)]
    hs = [_modulated(xs[s], g_ref, mod_ref[s, 0:1, :], mod_ref[s, 1:2, :]).astype(BF16) for s in range(n_rows)]
    logits = [_dot(h, wgate_ref[...]) for h in hs]
    branch_in = []
    for s in range(n_rows):
        o = of_ref[s].astype(F32) + ob_ref[s].astype(F32)
        z = z_ref[s]
        dn_parts = []
        for hd in range(HEADS):
            sl = slice(hd * HEAD_DIM, (hd + 1) * HEAD_DIM)
            dn_parts.append(_rms(o[:, sl], HEAD_DIM) * og_ref[...] * _silu(z[:, sl]))
        fn = jnp.where(is_lat, fl_ref[s], fc_ref[s])
        branch_in.append((jnp.concatenate(dn_parts, axis=1).astype(BF16), sc_ref[s].astype(BF16), fn.astype(BF16)))
    ys = [(_dot(dn, wdn_ref[...]), _dot(sc, wsc_ref[...]), _dot(fn, wfn_ref[...])) for dn, sc, fn in branch_in]
    merged = []
    for s in range(n_rows):
        gates = jax.nn.sigmoid(logits[s] + bgate_ref[...])
        y_dn, y_sc, y_fn = ys[s]
        merged.append((gates[:, :D_MODEL] * y_dn + gates[:, D_MODEL:2 * D_MODEL] * y_sc
                       + gates[:, 2 * D_MODEL:] * y_fn).astype(BF16))
    mixes = [_dot(m, wo_ref[...]) for m in merged]
    for s in range(n_rows):
        o_ref[s] = xs[s] + mod_ref[s, 2:3, :] * mixes[s]


def _merge_call(x_src, ctx_src, ctx_blk, mod_sel, n1g, o_f, o_b, z, sc, fn_lat, fn_ctx, wgate, bgate, og, wdn, wsc,
                wfn, wo, nt_lat, nt_run):
    bsz, _, d = x_src.shape
    ltot = (nt_lat + 1) * TILE
    bb = ROW_BATCH if bsz % ROW_BATCH == 0 else 1
    tok = lambda w: pl.BlockSpec((bb, TILE, w), lambda b, j: (b, j, 0))
    consts = (n1g, wgate, bgate, og, wdn, wsc, wfn, wo)
    return pl.pallas_call(
        functools.partial(_merge_kernel, nt_lat),
        grid=(bsz // bb, nt_run),
        in_specs=_x_specs(bb, nt_lat, ctx_blk, d) + [_mod_spec(bb, nt_lat, d),
                  tok(KEYW), tok(KEYW), tok(KEYW), tok(KEYW),
                  pl.BlockSpec((bb, TILE, KEYW), lambda b, j: (b, jnp.minimum(j, nt_lat - 1), 0)),
                  pl.BlockSpec((bb, TILE, KEYW), lambda b, j: (b, 0, 0))] + [_const_spec(a) for a in consts],
        out_specs=tok(d),
        out_shape=jax.ShapeDtypeStruct((bsz, ltot, d), F32),
        compiler_params=_cparams(2),
        name="branch_merge",
    )(x_src, ctx_src, mod_sel, o_f, o_b, z, sc, fn_lat, fn_ctx, *consts)


def _ffn_kernel(final, x_ref, mod_ref, g_ref, win_ref, wout_ref, fg_ref, o_ref):
    n_rows = x_ref.shape[0]
    hs = [_modulated(x_ref[s], g_ref, mod_ref[s, 3:4, :], mod_ref[s, 4:5, :]).astype(BF16) for s in range(n_rows)]
    abs_ = [_dot(h, win_ref[...]) for h in hs]
    acts = [(_silu(ab[:, :D_FF]) * ab[:, D_FF:]).astype(BF16) for ab in abs_]
    downs = [_dot(act, wout_ref[...]) for act in acts]
    for s in range(n_rows):
        y = x_ref[s] + mod_ref[s, 5:6, :] * downs[s]
        if final:
            y = _rms(y, D_MODEL) * fg_ref[...]
        o_ref[s] = y


def _ffn_call(x_cat, mod_sel, n2g, win, wout, fg, nt_lat, nt_run, final):
    bsz, ltot, d = x_cat.shape
    out_len = nt_run * TILE if final else ltot
    bb = ROW_BATCH if bsz % ROW_BATCH == 0 else 1
    tok = pl.BlockSpec((bb, TILE, d), lambda b, j: (b, j, 0))
    consts = (n2g, win, wout, fg)
    return pl.pallas_call(
        functools.partial(_ffn_kernel, final),
        grid=(bsz // bb, nt_run),
        in_specs=[tok, _mod_spec(bb, nt_lat, d)] + [_const_spec(a) for a in consts],
        out_specs=tok,
        out_shape=jax.ShapeDtypeStruct((bsz, out_len, d), F32),
        compiler_params=_cparams(2),
        name="swiglu_final" if final else "swiglu",
    )(x_cat, mod_sel, *consts)


def kernel(x, c, ctx, c_ctx, w_mod, b_mod, norm1_g, w_in, dn_conv_w, dn_a_log, dn_dt_bias, dn_onorm_g, w_dn_out,
           sc_conv_w, w_sc_out, w_fn_out, w_gate, b_gate, w_o, norm2_g, w_ffn_in, w_ffn_out, final_g):
    bsz, l_lat, d = x.shape
    lc = ctx.shape[1]
    assert d == D_MODEL and lc == TILE and l_lat % TILE == 0 and w_ffn_in.shape[-1] == 2 * D_FF
    depth = w_mod.shape[0]
    nt_lat = l_lat // TILE
    nt_all = nt_lat + 1
    n_lat_chunks = l_lat // CHUNK

    rows = ((bsz + 1 + 7) // 8) * 8
    c_rows = jnp.zeros((rows, d), F32).at[:bsz].set(c).at[bsz].set(c_ctx)
    mod = _mod_call(c_rows, w_mod, b_mod).reshape(depth, rows, 6, d)
    mod_sel = jnp.stack([mod[:, :bsz], jnp.broadcast_to(mod[:, bsz:bsz + 1], (depth, bsz, 6, d))], axis=2)

    x_src, ctx_src, ctx_blk = x, ctx, 0
    row = lambda a: a.reshape(1, -1)
    lane_pad = lambda a: jnp.zeros((1, HEAD_DIM), F32).at[0, N_DH:2 * N_DH].set(a.reshape(-1))
    for layer in range(depth):
        last = layer == depth - 1
        wl = w_in[layer]
        qkv_w = 3 * KEYW
        o_z, o_ba, o_sc, o_fn = qkv_w, qkv_w + KEYW, qkv_w + KEYW + 2 * N_DH, qkv_w + KEYW + 2 * N_DH + 3 * KEYW
        wba = jnp.zeros((d, HEAD_DIM), F32).at[:, :2 * N_DH].set(wl[:, o_ba:o_sc])
        wparts = tuple(a.astype(BF16) for a in (wl[:, :o_z], wl[:, o_z:o_ba], wba, wl[:, o_sc:o_fn], wl[:, o_fn:]))
        q, k, v, z, bg, sc, fn = _inproj_call(
            x_src, ctx_src, ctx_blk, mod_sel[layer], row(norm1_g[layer]), wparts, dn_conv_w[layer], sc_conv_w[layer],
            lane_pad(dn_a_log[layer]), lane_pad(dn_dt_bias[layer]), nt_lat)
        u, w, qg, kd, attn, adec = _dnprep_call(q, k, v, bg)
        o_f, o_b = _dnscan_call(u, w, qg, kd, attn, adec, n_lat_chunks)
        fn_lat = _fourier_lat_call(fn, l_lat)
        nt_run = nt_lat if last else nt_all
        fn_ctx = jnp.zeros((bsz, lc, KEYW), F32) if last else _fourier_ctx_call(fn, l_lat)
        x_cat = _merge_call(
            x_src, ctx_src, ctx_blk, mod_sel[layer], row(norm1_g[layer]), o_f, o_b, z, sc, fn_lat, fn_ctx,
            w_gate[layer].astype(BF16), row(b_gate[layer]), row(dn_onorm_g[layer]),
            w_dn_out[layer].astype(BF16), w_sc_out[layer].astype(BF16), w_fn_out[layer].astype(BF16),
            w_o[layer].astype(BF16), nt_lat, nt_run)
        x_cat = _ffn_call(x_cat, mod_sel[layer], row(norm2_g[layer]), w_ffn_in[layer].astype(BF16),
                          w_ffn_out[layer].astype(BF16), row(final_g), nt_lat, nt_run, last)
        x_src, ctx_src, ctx_blk = x_cat, x_cat, nt_lat
    return x_cat
```

```python
import functools
import math

import numpy as np
import jax
import jax.numpy as jnp
from jax import lax
from jax.experimental import pallas as pl
from jax.experimental.pallas import tpu as pltpu

F32 = jnp.float32
BF16 = jnp.bfloat16
HIGHEST = lax.Precision.HIGHEST

EPS = 1e-6
D_MODEL = 1024
DEPTH = 2
HEADS = 4
HEAD_DIM = 128
KEYW = HEADS * HEAD_DIM
CHUNK = 64
D_FF = 2816
TILE = 256
N_DH = 2 * HEADS
PACKW = HEADS * CHUNK
PREP_CHUNKS = 12
SCAN_BATCH = 8
SCAN_CHUNKS = 2
ROW_BATCH = 2
VMEM_LIMIT = 56 * 1024 * 1024


def _cparams(n_axes):
    return pltpu.CompilerParams(dimension_semantics=("arbitrary",) * n_axes, vmem_limit_bytes=VMEM_LIMIT)


def _dot(a, b):
    return jnp.dot(a, b, preferred_element_type=F32)


def _dot_hi(a, b):
    return jnp.dot(a, b, preferred_element_type=F32, precision=HIGHEST)


def _dot_nt(a, b):
    return lax.dot_general(a, b, (((1,), (1,)), ((), ())), preferred_element_type=F32)


def _dot_tn(a, b):
    return lax.dot_general(a, b, (((0,), (0,)), ((), ())), preferred_element_type=F32)


def _silu(t):
    return t * jax.nn.sigmoid(t)


def _rms(t, axis_size):
    return t * lax.rsqrt(jnp.sum(t * t, axis=-1, keepdims=True) * (1.0 / axis_size) + EPS)


def _mod_kernel(c_ref, w_ref, b_ref, o_ref):
    o_ref[...] = _dot_hi(_silu(c_ref[...]), w_ref[...]) + b_ref[...]


def _mod_call(c_rows, w_mod, b_mod):
    depth, d, n = w_mod.shape
    rows = c_rows.shape[0]
    tn = 1536
    return pl.pallas_call(
        _mod_kernel,
        grid=(depth, n // tn),
        in_specs=[
            pl.BlockSpec((rows, d), lambda l, j: (0, 0)),
            pl.BlockSpec((None, d, tn), lambda l, j: (l, 0, j)),
            pl.BlockSpec((None, 1, tn), lambda l, j: (l, 0, j)),
        ],
        out_specs=pl.BlockSpec((None, rows, tn), lambda l, j: (l, 0, j)),
        out_shape=jax.ShapeDtypeStruct((depth, rows, n), F32),
        compiler_params=_cparams(2),
        name="mod_vectors",
    )(c_rows, w_mod, b_mod.reshape(depth, 1, n))


def _row_masks(is_lat):
    pos = lax.broadcasted_iota(jnp.int32, (TILE, 1), 0)
    row_mask = jnp.where(is_lat, CHUNK - 1, TILE - 1)
    in_row = pos & row_mask
    return jnp.where(in_row == 0, 0.0, 1.0).astype(F32), jnp.where(in_row == row_mask, 0.0, 1.0).astype(F32)


def _mask_rows(t, mask, first):
    pieces = []
    for g in range(TILE // CHUNK):
        lo, hi = g * CHUNK, (g + 1) * CHUNK
        if first:
            pieces += [t[lo:lo + 8] * mask[lo:lo + 8], t[lo + 8:hi]]
        else:
            pieces += [t[lo:hi - 8], t[hi - 8:hi] * mask[hi - 8:hi]]
    return jnp.concatenate(pieces, axis=0)


def _conv3(t, w_ref, m_prev, m_next):
    prev = _mask_rows(pltpu.roll(t, 1, 0), m_prev, True)
    nxt = _mask_rows(pltpu.roll(t, TILE - 1, 0), m_next, False)
    return prev * w_ref[0:1, :] + t * w_ref[1:2, :] + nxt * w_ref[2:3, :]


def _modulated(x, g_ref, shift, scale):
    return _rms(x, D_MODEL) * (g_ref[...] * (1.0 + scale)) + shift


def _inproj_kernel(nt_lat, x_ref, ctx_ref, mod_ref, g_ref, wqkv_ref, wz_ref, wba_ref, wsc_ref, wfn_ref,
                   dnconv_ref, scconv_ref, alog_ref, dtb_ref,
                   q_ref, k_ref, v_ref, z_ref, bg_ref, sc_ref, fn_ref):
    is_lat = pl.program_id(1) < nt_lat
    m_prev, m_next = _row_masks(is_lat)
    n_rows = x_ref.shape[0]
    hs = []
    for s in range(n_rows):
        x = jnp.where(is_lat, x_ref[s], ctx_ref[s])
        hs.append(_modulated(x, g_ref, mod_ref[s, 0:1, :], mod_ref[s, 1:2, :]).astype(BF16))
    projs = [tuple(_dot(h, w_ref[...]) for w_ref in (wqkv_ref, wz_ref, wba_ref, wsc_ref, wfn_ref)) for h in hs]
    for s in range(n_rows):
        p_qkv, p_z, ba, scp, p_fn = projs[s]
        qkv = _silu(_conv3(p_qkv, dnconv_ref, m_prev, m_next))
        for hd in range(HEADS):
            lo = hd * HEAD_DIM
            qh = qkv[:, lo:lo + HEAD_DIM]
            kh = qkv[:, KEYW + lo:KEYW + lo + HEAD_DIM]
            q_ref[s, :, lo:lo + HEAD_DIM] = (qh * lax.rsqrt(jnp.sum(qh * qh, axis=-1, keepdims=True) + EPS)
                                             * (HEAD_DIM ** -0.5))
            k_ref[s, :, lo:lo + HEAD_DIM] = kh * lax.rsqrt(jnp.sum(kh * kh, axis=-1, keepdims=True) + EPS)
        v_ref[s] = qkv[:, 2 * KEYW:]
        z_ref[s] = p_z

        sp_in = ba + dtb_ref[...]
        softplus = jnp.maximum(sp_in, 0.0) + jnp.log1p(jnp.exp(-jnp.abs(sp_in)))
        lane = lax.broadcasted_iota(jnp.int32, ba.shape, 1)
        g = -jnp.exp(alog_ref[...]) * softplus
        in_chunk = lax.broadcasted_iota(jnp.int32, (TILE, 1), 0) & (CHUNK - 1)
        pre = g
        step = 1
        while step < CHUNK:
            pre = pre + jnp.where(in_chunk >= step, pltpu.roll(pre, step, 0), 0.0)
            step *= 2
        total = jnp.concatenate(
            [jnp.broadcast_to(pre[c * CHUNK + CHUNK - 1:(c + 1) * CHUNK, :], (CHUNK, HEAD_DIM))
             for c in range(TILE // CHUNK)], axis=0)
        suf = (total - pre) + g
        gc = jnp.where(lane < N_DH + HEADS, pre, suf)
        bg_ref[s] = jnp.where(lane < N_DH, jax.nn.sigmoid(ba), gc)

        sc_ref[s] = scp[:, :KEYW] * _conv3(scp[:, KEYW:2 * KEYW] * scp[:, 2 * KEYW:], scconv_ref, m_prev, m_next)
        fn_ref[s] = p_fn


def _const_spec(a):
    return pl.BlockSpec(a.shape, lambda b, j: (0,) * a.ndim, pipeline_mode=pl.Buffered(1))


def _mod_spec(bb, nt_lat, d):
    return pl.BlockSpec((bb, None, 6, d), lambda b, j: (b, jnp.where(j < nt_lat, 0, 1), 0, 0))


def _x_specs(bb, nt_lat, ctx_blk, d):
    return [pl.BlockSpec((bb, TILE, d), lambda b, j: (b, jnp.minimum(j, nt_lat - 1), 0)),
            pl.BlockSpec((bb, TILE, d), lambda b, j: (b, ctx_blk, 0))]


def _inproj_call(x_src, ctx_src, ctx_blk, mod_sel, n1g, wparts, dnconv, scconv, alog_row, dtb_row, nt_lat):
    bsz, _, d = x_src.shape
    nt = nt_lat + 1
    ltot = nt * TILE
    bb = ROW_BATCH if bsz % ROW_BATCH == 0 else 1
    wqkv, wz, wba, wsc, wfn = wparts
    tok = lambda w: pl.BlockSpec((bb, TILE, w), lambda b, j: (b, j, 0))
    out_w = (KEYW, KEYW, KEYW, KEYW, HEAD_DIM, KEYW, KEYW)
    consts = (n1g, wqkv, wz, wba, wsc, wfn, dnconv, scconv, alog_row, dtb_row)
    return pl.pallas_call(
        functools.partial(_inproj_kernel, nt_lat),
        grid=(bsz // bb, nt),
        in_specs=_x_specs(bb, nt_lat, ctx_blk, d) + [_mod_spec(bb, nt_lat, d)] + [_const_spec(a) for a in consts],
        out_specs=[tok(w) for w in out_w],
        out_shape=[jax.ShapeDtypeStruct((bsz, ltot, w), F32) for w in out_w],
        compiler_params=_cparams(2),
        name="in_proj",
    )(x_src, ctx_src, mod_sel, *consts)


def _split2(t):
    hi = t.astype(BF16)
    return hi, (t - hi.astype(F32)).astype(BF16)


def _mm3(lhs, rhs):
    m = lhs.shape[0]
    lh, ll = _split2(lhs)
    rh, rl = _split2(rhs)
    top = _dot(jnp.concatenate([lh, ll], axis=0), rh)
    return top[:m] + top[m:] + _dot(lh, rl)


def _dnprep_kernel(cps, q_ref, k_ref, v_ref, bg_ref, u_ref, w_ref, qg_ref, kd_ref, attn_ref, adec_ref):
    ri = lax.broadcasted_iota(jnp.int32, (CHUNK, PACKW), 0)
    cj = lax.broadcasted_iota(jnp.int32, (CHUNK, PACKW), 1) & (CHUNK - 1)
    eye_p = jnp.where(ri == cj, 1.0, 0.0).astype(F32)
    incl_m = ((ri >= cj), (ri <= cj))
    strict_m = ((ri > cj), (ri < cj))

    def half_masked(t, hd):
        lane = lax.broadcasted_iota(jnp.int32, t.shape, 1)
        keep = (lane < CHUNK) if hd % 2 == 0 else (lane >= CHUNK)
        return jnp.where(keep, t, jnp.zeros((), BF16))

    def blockdiag(t):
        z = jnp.zeros((CHUNK, HEAD_DIM), BF16)
        blocks = []
        for hd in range(HEADS):
            piece = half_masked(t[:, (hd // 2) * HEAD_DIM:(hd // 2 + 1) * HEAD_DIM], hd)
            blocks.append(jnp.concatenate([piece, z] if hd < 2 else [z, piece], axis=1))
        return jnp.concatenate(blocks, axis=0)

    def mm3_all(lhs_list, rhs_list):
        lsp = [_split2(l) for l in lhs_list]
        rsp = [tuple(blockdiag(p) for p in _split2(r)) for r in rhs_list]
        tops = [_dot(jnp.concatenate([lh, ll], axis=0), rh) for (lh, ll), (rh, _) in zip(lsp, rsp)]
        lows = [_dot(lh, rl) for (lh, _), (_, rl) in zip(lsp, rsp)]
        return [t[:l.shape[0]] + t[l.shape[0]:] + lo for t, lo, l in zip(tops, lows, lhs_list)]

    chunk_rows = [slice(ci * CHUNK, (ci + 1) * CHUNK) for ci in range(cps)]
    qs = [q_ref[r, :] for r in chunk_rows]
    ks = [k_ref[r, :] for r in chunk_rows]
    vs = [v_ref[r, :] for r in chunk_rows]

    lo_half = lax.broadcasted_iota(jnp.int32, (CHUNK, HEAD_DIM), 1) < CHUNK
    expanded = {}
    for ci, r in enumerate(chunk_rows):
        bg = bg_ref[r, :]
        for dr in range(2):
            cols = [(jnp.broadcast_to(bg[:, N_DH + dr * HEADS + hd:N_DH + dr * HEADS + hd + 1], (CHUNK, HEAD_DIM)),
                     jnp.broadcast_to(bg[:, dr * HEADS + hd:dr * HEADS + hd + 1], (CHUNK, HEAD_DIM)))
                    for hd in range(HEADS)]
            gcol = jnp.concatenate([c[0] for c in cols], axis=1)
            beta = jnp.concatenate([c[1] for c in cols], axis=1)
            gcol_p = jnp.concatenate([jnp.where(lo_half, cols[0][0], cols[1][0]),
                                      jnp.where(lo_half, cols[2][0], cols[3][0])], axis=1)
            expanded[ci, dr] = (gcol, beta, gcol_p)

    def parts(ci, dr):
        return expanded[ci, dr]

    m1s = []
    for ci in range(cps):
        k = ks[ci]
        k_t = jnp.concatenate([k, k], axis=0).T.astype(BF16)
        zk = jnp.zeros((HEAD_DIM, HEAD_DIM), BF16)
        bd_rows = []
        for hd in range(HEADS):
            piece = half_masked(k_t[hd * HEAD_DIM:(hd + 1) * HEAD_DIM, :], hd)
            bd_rows.append(jnp.concatenate([piece, zk] if hd < 2 else [zk, piece], axis=1))
        bd_k = jnp.concatenate(bd_rows, axis=0)
        kbs = [k * parts(ci, dr)[1] for dr in range(2)]
        m1s.append(_dot(jnp.concatenate(kbs + [qs[ci]], axis=0).astype(BF16), bd_k))

    chains = [(ci, dr) for ci in range(cps) for dr in range(2)]
    decays, neg_as = [], []
    for ci, dr in chains:
        gcol_p = parts(ci, dr)[2]
        grow_p = jnp.sum(gcol_p * eye_p, axis=0, keepdims=True)
        decay = jnp.exp(jnp.where(incl_m[dr], gcol_p - grow_p, -jnp.inf))
        decays.append(decay)
        neg_as.append(jnp.where(strict_m[dr], -(m1s[ci][dr * CHUNK:(dr + 1) * CHUNK] * decay), 0.0))

    p_accs = [eye_p + a for a in neg_as]
    q_pows = mm3_all(neg_as, neg_as)
    for _ in range(4):
        boths = mm3_all([jnp.concatenate([qp, pa], axis=0) for qp, pa in zip(q_pows, p_accs)], q_pows)
        q_pows = [b[:CHUNK] for b in boths]
        p_accs = [pa + b[CHUNK:] for pa, b in zip(p_accs, boths)]
    lasts = mm3_all(p_accs, q_pows)
    p_accs = [pa + la for pa, la in zip(p_accs, lasts)]

    uws, egs = [], []
    for (ci, dr), t_inv in zip(chains, p_accs):
        gcol, beta, _ = parts(ci, dr)
        eg = jnp.exp(gcol)
        egs.append(eg)
        rhs = jnp.concatenate([vs[ci] * beta, ks[ci] * (beta * eg)], axis=1).astype(BF16)
        zr = jnp.zeros((CHUNK, HEAD_DIM), BF16)
        n_tiles = 2 * KEYW // HEAD_DIM
        bd_r = jnp.concatenate([
            jnp.concatenate([rhs[:, t * HEAD_DIM:(t + 1) * HEAD_DIM] if t % HEADS == hd else zr
                             for t in range(n_tiles)], axis=1)
            for hd in range(HEADS)], axis=0)
        uws.append(_dot(t_inv.astype(BF16), bd_r))

    for (ci, dr), uw, eg, decay in zip(chains, uws, egs, decays):
        rows = chunk_rows[ci]
        gcol = parts(ci, dr)[0]
        g_last = gcol[CHUNK - 1:CHUNK, :] if dr == 0 else gcol[0:1, :]
        u_ref[dr, rows, :] = uw[:, :KEYW].astype(BF16)
        w_ref[dr, rows, :] = uw[:, KEYW:].astype(BF16)
        qg_ref[dr, rows, :] = (qs[ci] * eg).astype(BF16)
        kd_ref[dr, rows, :] = (ks[ci] * jnp.exp(g_last - gcol)).astype(BF16)
        attn_ref[dr, rows, :] = (m1s[ci][2 * CHUNK:] * decay).astype(BF16)
        a_dec = jnp.exp(g_last)
        for hd in range(HEADS):
            unit = dr * HEADS + hd
            adec_ref[ci, unit:unit + 1, :] = a_dec[:, hd * HEAD_DIM:(hd + 1) * HEAD_DIM]


def _dnprep_call(q, k, v, bg):
    bsz, ltot, _ = q.shape
    nch = ltot // CHUNK
    cps = PREP_CHUNKS
    assert nch % cps == 0
    tok = lambda w: pl.BlockSpec((None, cps * CHUNK, w), lambda b, n: (b, n, 0))
    dtok = lambda w: pl.BlockSpec((None, 2, cps * CHUNK, w), lambda b, n: (b, 0, n, 0))
    return pl.pallas_call(
        functools.partial(_dnprep_kernel, cps),
        grid=(bsz, nch // cps),
        in_specs=[tok(KEYW), tok(KEYW), tok(KEYW), tok(HEAD_DIM)],
        out_specs=[dtok(KEYW), dtok(KEYW), dtok(KEYW), dtok(KEYW), dtok(PACKW),
                   pl.BlockSpec((None, cps, N_DH, HEAD_DIM), lambda b, n: (b, n, 0, 0))],
        out_shape=[jax.ShapeDtypeStruct((bsz, 2, ltot, KEYW), BF16),
                   jax.ShapeDtypeStruct((bsz, 2, ltot, KEYW), BF16),
                   jax.ShapeDtypeStruct((bsz, 2, ltot, KEYW), BF16),
                   jax.ShapeDtypeStruct((bsz, 2, ltot, KEYW), BF16),
                   jax.ShapeDtypeStruct((bsz, 2, ltot, PACKW), BF16),
                   jax.ShapeDtypeStruct((bsz, nch, N_DH, HEAD_DIM), F32)],
        compiler_params=_cparams(2),
        name="dn_prep",
    )(q, k, v, bg)


def _dnscan_kernel(uf_ref, wf_ref, qgf_ref, kdf_ref, af_ref, df_ref,
                   ub_ref, wb_ref, qgb_ref, kdb_ref, ab_ref, db_ref,
                   of_ref, ob_ref, s_ref):
    @pl.when(pl.program_id(1) == 0)
    def _():
        s_ref[...] = jnp.zeros_like(s_ref)

    per_dir = ((uf_ref, wf_ref, qgf_ref, kdf_ref, af_ref, df_ref, of_ref),
               (ub_ref, wb_ref, qgb_ref, kdb_ref, ab_ref, db_ref, ob_ref))
    units = [(bi, dr, hd) for bi in range(uf_ref.shape[0]) for dr in range(2) for hd in range(HEADS)]
    states = [s_ref[bi, dr * HEADS + hd] for bi, dr, hd in units]
    n_sub = uf_ref.shape[1] // CHUNK
    for sub in range(n_sub):
        pos = (sub, n_sub - 1 - sub)
        rows = [slice(p * CHUNK, (p + 1) * CHUNK) for p in pos]
        first = []
        for (bi, dr, hd), s in zip(units, states):
            _, w_ref, qg_ref = per_dir[dr][:3]
            lo = hd * HEAD_DIM
            first.append(_dot(jnp.concatenate([w_ref[bi, rows[dr], lo:lo + HEAD_DIM],
                                               qg_ref[bi, rows[dr], lo:lo + HEAD_DIM]], axis=0), s.astype(BF16)))
        new_states = []
        for (bi, dr, hd), both, s in zip(units, first, states):
            u_ref, _, _, kd_ref, a_ref, d_ref, o_ref = per_dir[dr]
            unit = dr * HEADS + hd
            lo = hd * HEAD_DIM
            v_new_b = (u_ref[bi, rows[dr], lo:lo + HEAD_DIM].astype(F32) - both[:CHUNK]).astype(BF16)
            o_ref[bi, rows[dr], lo:lo + HEAD_DIM] = (
                both[CHUNK:] + _dot(a_ref[bi, rows[dr], hd * CHUNK:(hd + 1) * CHUNK], v_new_b)).astype(BF16)
            new_states.append(s * d_ref[bi, pos[dr], unit:unit + 1, :]
                              + _dot_tn(kd_ref[bi, rows[dr], lo:lo + HEAD_DIM], v_new_b))
        states = new_states
    for (bi, dr, hd), s in zip(units, states):
        s_ref[bi, dr * HEADS + hd] = s


def _dnscan_call(u, w, qg, kd, attn, adec, n_lat_chunks):
    bsz, _, ltot, _ = u.shape
    nch = ltot // CHUNK
    bb = SCAN_BATCH if bsz % SCAN_BATCH == 0 else 1
    cpg = SCAN_CHUNKS if (nch % SCAN_CHUNKS == 0 and n_lat_chunks % SCAN_CHUNKS == 0) else 1
    nblk = nch // cpg
    fwd = lambda n: (n + n_lat_chunks // cpg) % nblk
    bwd = lambda n: nblk - 1 - n

    def dspec(wd, dr, order):
        return pl.BlockSpec((bb, None, cpg * CHUNK, wd), lambda b, n: (b, dr, order(n), 0))

    def aspec(order):
        return pl.BlockSpec((bb, cpg, N_DH, HEAD_DIM), lambda b, n: (b, order(n), 0, 0))

    def ospec(order):
        return pl.BlockSpec((bb, cpg * CHUNK, KEYW), lambda b, n: (b, order(n), 0))

    in_specs = []
    for dr, order in ((0, fwd), (1, bwd)):
        in_specs += [dspec(KEYW, dr, order), dspec(KEYW, dr, order), dspec(KEYW, dr, order), dspec(KEYW, dr, order),
                     dspec(PACKW, dr, order), aspec(order)]
    return pl.pallas_call(
        _dnscan_kernel,
        grid=(bsz // bb, nblk),
        in_specs=in_specs,
        out_specs=[ospec(fwd), ospec(bwd)],
        out_shape=[jax.ShapeDtypeStruct((bsz, ltot, KEYW), BF16)] * 2,
        scratch_shapes=[pltpu.VMEM((bb, N_DH, HEAD_DIM, HEAD_DIM), F32)],
        compiler_params=_cparams(2),
        name="dn_scan",
    )(u, w, qg, kd, attn, adec, u, w, qg, kd, attn, adec)


def _dft_mats(n):
    idx = np.arange(n)
    ang = 2.0 * np.pi * ((idx[:, None] * idx[None, :]) % n) / n
    return np.cos(ang), np.sin(ang)


def _fft1_kernel(n2blk, u_ref, f1_ref, tc_ref, ts_ref, br_ref, bi_ref):
    l1 = u_ref.shape[1]
    rows = l1 * n2blk
    prods = [_dot(f1_ref[...], u_ref[s].reshape(rows, KEYW).astype(BF16)) for s in range(u_ref.shape[0])]
    tc = jnp.concatenate([tc_ref[...].reshape(rows, HEAD_DIM)] * (KEYW // HEAD_DIM), axis=1)
    ts = jnp.concatenate([ts_ref[...].reshape(rows, HEAD_DIM)] * (KEYW // HEAD_DIM), axis=1)
    for s, a in enumerate(prods):
        ar, ai = a[:rows], a[rows:]
        br_ref[s] = (ar * tc + ai * ts).reshape(n2blk, l1, KEYW)
        bi_ref[s] = (ai * tc - ar * ts).reshape(n2blk, l1, KEYW)


def _fft2_kernel(k1blk, nlen, br_ref, bi_ref, f2_ref, wc_ref, y_ref):
    rows = k1blk * nlen
    n_rows = br_ref.shape[0]
    xs = [_dot(f2_ref[...], jnp.concatenate([br_ref[s].reshape(rows, KEYW), bi_ref[s].reshape(rows, KEYW)],
                                            axis=0).astype(BF16)) for s in range(n_rows)]
    for s, x in enumerate(xs):
        xr, xi = x[:rows].astype(BF16), x[rows:].astype(BF16)
        y = jnp.concatenate(
            [_dot(jnp.concatenate([xr[:, g * HEAD_DIM:(g + 1) * HEAD_DIM], xi[:, g * HEAD_DIM:(g + 1) * HEAD_DIM]],
                                  axis=1), wc_ref[...]) for g in range(KEYW // HEAD_DIM)], axis=1)
        y_ref[s] = y.reshape(nlen, k1blk, KEYW)


def _dftctx_kernel(nlen, u_ref, f_ref, wc_ref, y_ref):
    x = _dot(f_ref[...], u_ref[...].astype(BF16))
    xr, xi = x[:nlen].astype(BF16), x[nlen:].astype(BF16)
    for g in range(KEYW // HEAD_DIM):
        gs = slice(g * HEAD_DIM, (g + 1) * HEAD_DIM)
        y_ref[:, gs] = _dot(jnp.concatenate([xr[:, gs], xi[:, gs]], axis=1), wc_ref[...])


def _chan_mat(total_len):
    cc, sc = _dft_mats(HEAD_DIM)
    return jnp.asarray(np.concatenate([cc, sc], axis=0) / math.sqrt(total_len * HEAD_DIM), BF16)


def _fourier_lat_call(fn_cat, l_lat):
    bsz, ltot, _ = fn_cat.shape
    l1 = l_lat // CHUNK
    n2blk = 8
    k1blk = 8 if l1 % 8 == 0 else l1
    c1, s1 = _dft_mats(l1)
    eye_n2 = np.eye(n2blk)
    kron1 = lambda f: np.einsum("kn,ab->aknb", f, eye_n2).reshape(n2blk * l1, l1 * n2blk)
    f1 = jnp.asarray(np.concatenate([kron1(c1), kron1(-s1)], axis=0), BF16)
    n2 = np.arange(CHUNK)[:, None, None]
    k1 = np.arange(l1)[None, :, None]
    ang = 2.0 * np.pi * (n2 * k1) / l_lat * np.ones((1, 1, HEAD_DIM))
    tc, ts = jnp.asarray(np.cos(ang), F32), jnp.asarray(np.sin(ang), F32)
    c2, s2 = _dft_mats(CHUNK)
    eye_k1 = np.eye(k1blk)
    kron2 = lambda f: np.kron(f, eye_k1)
    f2 = jnp.asarray(np.block([[kron2(c2), kron2(s2)], [kron2(-s2), kron2(c2)]]), BF16)
    wc = _chan_mat(l_lat)

    u_view = fn_cat.reshape(bsz, ltot // CHUNK, CHUNK, KEYW)
    bb = ROW_BATCH if bsz % ROW_BATCH == 0 else 1
    mid_spec = pl.BlockSpec((bb, n2blk, l1, KEYW), lambda b, i: (b, i, 0, 0))
    br, bi = pl.pallas_call(
        functools.partial(_fft1_kernel, n2blk),
        grid=(bsz // bb, CHUNK // n2blk),
        in_specs=[pl.BlockSpec((bb, l1, n2blk, KEYW), lambda b, i: (b, 0, i, 0)),
                  pl.BlockSpec(f1.shape, lambda b, i: (0, 0)),
                  pl.BlockSpec((n2blk, l1, HEAD_DIM), lambda b, i: (i, 0, 0)),
                  pl.BlockSpec((n2blk, l1, HEAD_DIM), lambda b, i: (i, 0, 0))],
        out_specs=[mid_spec, mid_spec],
        out_shape=[jax.ShapeDtypeStruct((bsz, CHUNK, l1, KEYW), F32)] * 2,
        compiler_params=_cparams(2),
        name="fourier_stage1",
    )(u_view, f1, tc, ts)

    rspec = pl.BlockSpec((bb, CHUNK, k1blk, KEYW), lambda b, i: (b, 0, i, 0))
    y = pl.pallas_call(
        functools.partial(_fft2_kernel, k1blk, CHUNK),
        grid=(bsz // bb, l1 // k1blk),
        in_specs=[rspec, rspec,
                  pl.BlockSpec(f2.shape, lambda b, i: (0, 0)),
                  pl.BlockSpec(wc.shape, lambda b, i: (0, 0))],
        out_specs=rspec,
        out_shape=jax.ShapeDtypeStruct((bsz, CHUNK, l1, KEYW), F32),
        compiler_params=_cparams(2),
        name="fourier_stage2",
    )(br, bi, f2, wc)
    return y.reshape(bsz, l_lat, KEYW)


def _fourier_ctx_call(fn_cat, l_lat):
    bsz, ltot, _ = fn_cat.shape
    lc = ltot - l_lat
    c, s = _dft_mats(lc)
    f = jnp.asarray(np.concatenate([c, -s], axis=0), BF16)
    wc = _chan_mat(lc)
    return pl.pallas_call(
        functools.partial(_dftctx_kernel, lc),
        grid=(bsz,),
        in_specs=[pl.BlockSpec((None, lc, KEYW), lambda b: (b, l_lat // lc, 0)),
                  pl.BlockSpec(f.shape, lambda b: (0, 0)),
                  pl.BlockSpec(wc.shape, lambda b: (0, 0))],
        out_specs=pl.BlockSpec((None, lc, KEYW), lambda b: (b, 0, 0)),
        out_shape=jax.ShapeDtypeStruct((bsz, lc, KEYW), F32),
        compiler_params=_cparams(1),
        name="fourier_ctx",
    )(fn_cat, f, wc)


def _merge_kernel(nt_lat, x_ref, ctx_ref, mod_ref, of_ref, ob_ref, z_ref, sc_ref, fl_ref, fc_ref,
                  g_ref, wgate_ref, bgate_ref, og_ref, wdn_ref, wsc_ref, wfn_ref, wo_ref, o_ref):
    is_lat = pl.program_id(1) < nt_lat
    n_rows = x_ref.shape[0]
    xs = [jnp.where(is_lat, x_ref[s], ctx_ref[s]) for s in range(n_rows)]
    hs = [_modulated(xs[s], g_ref, mod_ref[s, 0:1, :], mod_ref[s, 1:2, :]).astype(BF16) for s in range(n_rows)]
    logits = [_dot(h, wgate_ref[...]) for h in hs]
    branch_in = []
    for s in range(n_rows):
        o = of_ref[s].astype(F32) + ob_ref[s].astype(F32)
        z = z_ref[s]
        dn_parts = []
        for hd in range(HEADS):
            sl = slice(hd * HEAD_DIM, (hd + 1) * HEAD_DIM)
            dn_parts.append(_rms(o[:, sl], HEAD_DIM) * og_ref[...] * _silu(z[:, sl]))
        fn = jnp.where(is_lat, fl_ref[s], fc_ref[s])
        branch_in.append((jnp.concatenate(dn_parts, axis=1).astype(BF16), sc_ref[s].astype(BF16), fn.astype(BF16)))
    ys = [(_dot(dn, wdn_ref[...]), _dot(sc, wsc_ref[...]), _dot(fn, wfn_ref[...])) for dn, sc, fn in branch_in]
    merged = []
    for s in range(n_rows):
        gates = jax.nn.sigmoid(logits[s] + bgate_ref[...])
        y_dn, y_sc, y_fn = ys[s]
        merged.append((gates[:, :D_MODEL] * y_dn + gates[:, D_MODEL:2 * D_MODEL] * y_sc
                       + gates[:, 2 * D_MODEL:] * y_fn).astype(BF16))
    mixes = [_dot(m, wo_ref[...]) for m in merged]
    for s in range(n_rows):
        o_ref[s] = xs[s] + mod_ref[s, 2:3, :] * mixes[s]


def _merge_call(x_src, ctx_src, ctx_blk, mod_sel, n1g, o_f, o_b, z, sc, fn_lat, fn_ctx, wgate, bgate, og, wdn, wsc,
                wfn, wo, nt_lat, nt_run):
    bsz, _, d = x_src.shape
    ltot = (nt_lat + 1) * TILE
    bb = ROW_BATCH if bsz % ROW_BATCH == 0 else 1
    tok = lambda w: pl.BlockSpec((bb, TILE, w), lambda b, j: (b, j, 0))
    consts = (n1g, wgate, bgate, og, wdn, wsc, wfn, wo)
    return pl.pallas_call(
        functools.partial(_merge_kernel, nt_lat),
        grid=(bsz // bb, nt_run),
        in_specs=_x_specs(bb, nt_lat, ctx_blk, d) + [_mod_spec(bb, nt_lat, d),
                  tok(KEYW), tok(KEYW), tok(KEYW), tok(KEYW),
                  pl.BlockSpec((bb, TILE, KEYW), lambda b, j: (b, jnp.minimum(j, nt_lat - 1), 0)),
                  pl.BlockSpec((bb, TILE, KEYW), lambda b, j: (b, 0, 0))] + [_const_spec(a) for a in consts],
        out_specs=tok(d),
        out_shape=jax.ShapeDtypeStruct((bsz, ltot, d), F32),
        compiler_params=_cparams(2),
        name="branch_merge",
    )(x_src, ctx_src, mod_sel, o_f, o_b, z, sc, fn_lat, fn_ctx, *consts)


def _ffn_kernel(final, x_ref, mod_ref, g_ref, win_ref, wout_ref, fg_ref, o_ref):
    n_rows = x_ref.shape[0]
    hs = [_modulated(x_ref[s], g_ref, mod_ref[s, 3:4, :], mod_ref[s, 4:5, :]).astype(BF16) for s in range(n_rows)]
    abs_ = [_dot(h, win_ref[...]) for h in hs]
    acts = [(_silu(ab[:, :D_FF]) * ab[:, D_FF:]).astype(BF16) for ab in abs_]
    downs = [_dot(act, wout_ref[...]) for act in acts]
    for s in range(n_rows):
        y = x_ref[s] + mod_ref[s, 5:6, :] * downs[s]
        if final:
            y = _rms(y, D_MODEL) * fg_ref[...]
        o_ref[s] = y


def _ffn_call(x_cat, mod_sel, n2g, win, wout, fg, nt_lat, nt_run, final):
    bsz, ltot, d = x_cat.shape
    out_len = nt_run * TILE if final else ltot
    bb = ROW_BATCH if bsz % ROW_BATCH == 0 else 1
    tok = pl.BlockSpec((bb, TILE, d), lambda b, j: (b, j, 0))
    consts = (n2g, win, wout, fg)
    return pl.pallas_call(
        functools.partial(_ffn_kernel, final),
        grid=(bsz // bb, nt_run),
        in_specs=[tok, _mod_spec(bb, nt_lat, d)] + [_const_spec(a) for a in consts],
        out_specs=tok,
        out_shape=jax.ShapeDtypeStruct((bsz, out_len, d), F32),
        compiler_params=_cparams(2),
        name="swiglu_final" if final else "swiglu",
    )(x_cat, mod_sel, *consts)


def kernel(x, c, ctx, c_ctx, w_mod, b_mod, norm1_g, w_in, dn_conv_w, dn_a_log, dn_dt_bias, dn_onorm_g, w_dn_out,
           sc_conv_w, w_sc_out, w_fn_out, w_gate, b_gate, w_o, norm2_g, w_ffn_in, w_ffn_out, final_g):
    bsz, l_lat, d = x.shape
    lc = ctx.shape[1]
    assert d == D_MODEL and lc == TILE and l_lat % TILE == 0 and w_ffn_in.shape[-1] == 2 * D_FF
    depth = w_mod.shape[0]
    nt_lat = l_lat // TILE
    nt_all = nt_lat + 1
    n_lat_chunks = l_lat // CHUNK

    rows = ((bsz + 1 + 7) // 8) * 8
    c_rows = jnp.zeros((rows, d), F32).at[:bsz].set(c).at[bsz].set(c_ctx)
    mod = _mod_call(c_rows, w_mod, b_mod).reshape(depth, rows, 6, d)
    mod_sel = jnp.stack([mod[:, :bsz], jnp.broadcast_to(mod[:, bsz:bsz + 1], (depth, bsz, 6, d))], axis=2)

    x_src, ctx_src, ctx_blk = x, ctx, 0
    row = lambda a: a.reshape(1, -1)
    lane_pad = lambda a: jnp.zeros((1, HEAD_DIM), F32).at[0, N_DH:2 * N_DH].set(a.reshape(-1))
    for layer in range(depth):
        last = layer == depth - 1
        wl = w_in[layer]
        qkv_w = 3 * KEYW
        o_z, o_ba, o_sc, o_fn = qkv_w, qkv_w + KEYW, qkv_w + KEYW + 2 * N_DH, qkv_w + KEYW + 2 * N_DH + 3 * KEYW
        wba = jnp.zeros((d, HEAD_DIM), F32).at[:, :2 * N_DH].set(wl[:, o_ba:o_sc])
        wparts = tuple(a.astype(BF16) for a in (wl[:, :o_z], wl[:, o_z:o_ba], wba, wl[:, o_sc:o_fn], wl[:, o_fn:]))
        q, k, v, z, bg, sc, fn = _inproj_call(
            x_src, ctx_src, ctx_blk, mod_sel[layer], row(norm1_g[layer]), wparts, dn_conv_w[layer], sc_conv_w[layer],
            lane_pad(dn_a_log[layer]), lane_pad(dn_dt_bias[layer]), nt_lat)
        u, w, qg, kd, attn, adec = _dnprep_call(q, k, v, bg)
        o_f, o_b = _dnscan_call(u, w, qg, kd, attn, adec, n_lat_chunks)
        fn_lat = _fourier_lat_call(fn, l_lat)
        nt_run = nt_lat if last else nt_all
        fn_ctx = jnp.zeros((bsz, lc, KEYW), F32) if last else _fourier_ctx_call(fn, l_lat)
        x_cat = _merge_call(
            x_src, ctx_src, ctx_blk, mod_sel[layer], row(norm1_g[layer]), o_f, o_b, z, sc, fn_lat, fn_ctx,
            w_gate[layer].astype(BF16), row(b_gate[layer]), row(dn_onorm_g[layer]),
            w_dn_out[layer].astype(BF16), w_sc_out[layer].astype(BF16), w_fn_out[layer].astype(BF16),
            w_o[layer].astype(BF16), nt_lat, nt_run)
        x_cat = _ffn_call(x_cat, mod_sel[layer], row(norm2_g[layer]), w_ffn_in[layer].astype(BF16),
                          w_ffn_out[layer].astype(BF16), row(final_g), nt_lat, nt_run, last)
        x_src, ctx_src, ctx_blk = x_cat, x_cat, nt_lat
    return x_cat
```

```python
import functools
import math

import numpy as np
import jax
import jax.numpy as jnp
from jax import lax
from jax.experimental import pallas as pl
from jax.experimental.pallas import tpu as pltpu

F32 = jnp.float32
BF16 = jnp.bfloat16
HIGHEST = lax.Precision.HIGHEST

EPS = 1e-6
D_MODEL = 1024
DEPTH = 2
HEADS = 4
HEAD_DIM = 128
KEYW = HEADS * HEAD_DIM
CHUNK = 64
D_FF = 2816
TILE = 256
N_DH = 2 * HEADS
PACKW = HEADS * CHUNK
PREP_CHUNKS = 12
PREP_GROUP = 6
SCAN_BATCH = 8
SCAN_CHUNKS = 2
ROW_BATCH = 2
VMEM_LIMIT = 56 * 1024 * 1024


def _cparams(n_axes):
    return pltpu.CompilerParams(dimension_semantics=("arbitrary",) * n_axes, vmem_limit_bytes=VMEM_LIMIT)


def _dot(a, b):
    return jnp.dot(a, b, preferred_element_type=F32)


def _dot_hi(a, b):
    return jnp.dot(a, b, preferred_element_type=F32, precision=HIGHEST)


def _dot_nt(a, b):
    return lax.dot_general(a, b, (((1,), (1,)), ((), ())), preferred_element_type=F32)


def _dot_tn(a, b):
    return lax.dot_general(a, b, (((0,), (0,)), ((), ())), preferred_element_type=F32)


def _silu(t):
    return t * jax.nn.sigmoid(t)


def _rms(t, axis_size):
    return t * lax.rsqrt(jnp.sum(t * t, axis=-1, keepdims=True) * (1.0 / axis_size) + EPS)


def _mod_kernel(c_ref, w_ref, b_ref, o_ref):
    o_ref[...] = _dot_hi(_silu(c_ref[...]), w_ref[...]) + b_ref[...]


def _mod_call(c_rows, w_mod, b_mod):
    depth, d, n = w_mod.shape
    rows = c_rows.shape[0]
    tn = 1536
    return pl.pallas_call(
        _mod_kernel,
        grid=(depth, n // tn),
        in_specs=[
            pl.BlockSpec((rows, d), lambda l, j: (0, 0)),
            pl.BlockSpec((None, d, tn), lambda l, j: (l, 0, j)),
            pl.BlockSpec((None, 1, tn), lambda l, j: (l, 0, j)),
        ],
        out_specs=pl.BlockSpec((None, rows, tn), lambda l, j: (l, 0, j)),
        out_shape=jax.ShapeDtypeStruct((depth, rows, n), F32),
        compiler_params=_cparams(2),
        name="mod_vectors",
    )(c_rows, w_mod, b_mod.reshape(depth, 1, n))


def _row_masks(is_lat):
    pos = lax.broadcasted_iota(jnp.int32, (TILE, 1), 0)
    row_mask = jnp.where(is_lat, CHUNK - 1, TILE - 1)
    in_row = pos & row_mask
    return jnp.where(in_row == 0, 0.0, 1.0).astype(F32), jnp.where(in_row == row_mask, 0.0, 1.0).astype(F32)


def _mask_rows(t, mask, first):
    pieces = []
    for g in range(TILE // CHUNK):
        lo, hi = g * CHUNK, (g + 1) * CHUNK
        if first:
            pieces += [t[lo:lo + 8] * mask[lo:lo + 8], t[lo + 8:hi]]
        else:
            pieces += [t[lo:hi - 8], t[hi - 8:hi] * mask[hi - 8:hi]]
    return jnp.concatenate(pieces, axis=0)


def _conv3(t, w_ref, m_prev, m_next):
    prev = _mask_rows(pltpu.roll(t, 1, 0), m_prev, True)
    nxt = _mask_rows(pltpu.roll(t, TILE - 1, 0), m_next, False)
    return prev * w_ref[0:1, :] + t * w_ref[1:2, :] + nxt * w_ref[2:3, :]


def _modulated(x, g_ref, shift, scale):
    return _rms(x, D_MODEL) * (g_ref[...] * (1.0 + scale)) + shift


def _inproj_kernel(nt_lat, x_ref, ctx_ref, mod_ref, g_ref, wqkv_ref, wz_ref, wba_ref, wsc_ref, wfn_ref,
                   dnconv_ref, scconv_ref, alog_ref, dtb_ref,
                   q_ref, k_ref, v_ref, z_ref, bg_ref, sc_ref, fn_ref):
    is_lat = pl.program_id(1) < nt_lat
    m_prev, m_next = _row_masks(is_lat)
    n_rows = x_ref.shape[0]
    hs = []
    for s in range(n_rows):
        x = jnp.where(is_lat, x_ref[s], ctx_ref[s])
        hs.append(_modulated(x, g_ref, mod_ref[s, 0:1, :], mod_ref[s, 1:2, :]).astype(BF16))
    projs = [tuple(_dot(h, w_ref[...]) for w_ref in (wqkv_ref, wz_ref, wba_ref, wsc_ref, wfn_ref)) for h in hs]
    for s in range(n_rows):
        p_qkv, p_z, ba, scp, p_fn = projs[s]
        qkv = _silu(_conv3(p_qkv, dnconv_ref, m_prev, m_next))
        for hd in range(HEADS):
            lo = hd * HEAD_DIM
            qh = qkv[:, lo:lo + HEAD_DIM]
            kh = qkv[:, KEYW + lo:KEYW + lo + HEAD_DIM]
            q_ref[s, :, lo:lo + HEAD_DIM] = (qh * lax.rsqrt(jnp.sum(qh * qh, axis=-1, keepdims=True) + EPS)
                                             * (HEAD_DIM ** -0.5))
            k_ref[s, :, lo:lo + HEAD_DIM] = kh * lax.rsqrt(jnp.sum(kh * kh, axis=-1, keepdims=True) + EPS)
        v_ref[s] = qkv[:, 2 * KEYW:]
        z_ref[s] = p_z

        sp_in = ba + dtb_ref[...]
        softplus = jnp.maximum(sp_in, 0.0) + jnp.log1p(jnp.exp(-jnp.abs(sp_in)))
        lane = lax.broadcasted_iota(jnp.int32, ba.shape, 1)
        g = -jnp.exp(alog_ref[...]) * softplus
        in_chunk = lax.broadcasted_iota(jnp.int32, (TILE, 1), 0) & (CHUNK - 1)
        pre = g
        step = 1
        while step < CHUNK:
            pre = pre + jnp.where(in_chunk >= step, pltpu.roll(pre, step, 0), 0.0)
            step *= 2
        total = jnp.concatenate(
            [jnp.broadcast_to(pre[c * CHUNK + CHUNK - 1:(c + 1) * CHUNK, :], (CHUNK, HEAD_DIM))
             for c in range(TILE // CHUNK)], axis=0)
        suf = (total - pre) + g
        gc = jnp.where(lane < N_DH + HEADS, pre, suf)
        bg_ref[s] = jnp.where(lane < N_DH, jax.nn.sigmoid(ba), gc)

        sc_ref[s] = scp[:, :KEYW] * _conv3(scp[:, KEYW:2 * KEYW] * scp[:, 2 * KEYW:], scconv_ref, m_prev, m_next)
        fn_ref[s] = p_fn


def _const_spec(a):
    return pl.BlockSpec(a.shape, lambda b, j: (0,) * a.ndim, pipeline_mode=pl.Buffered(1))


def _mod_spec(bb, nt_lat, d):
    return pl.BlockSpec((bb, None, 6, d), lambda b, j: (b, jnp.where(j < nt_lat, 0, 1), 0, 0))


def _x_specs(bb, nt_lat, ctx_blk, d):
    return [pl.BlockSpec((bb, TILE, d), lambda b, j: (b, jnp.minimum(j, nt_lat - 1), 0)),
            pl.BlockSpec((bb, TILE, d), lambda b, j: (b, ctx_blk, 0))]


def _inproj_call(x_src, ctx_src, ctx_blk, mod_sel, n1g, wparts, dnconv, scconv, alog_row, dtb_row, nt_lat):
    bsz, _, d = x_src.shape
    nt = nt_lat + 1
    ltot = nt * TILE
    bb = ROW_BATCH if bsz % ROW_BATCH == 0 else 1
    wqkv, wz, wba, wsc, wfn = wparts
    tok = lambda w: pl.BlockSpec((bb, TILE, w), lambda b, j: (b, j, 0))
    out_w = (KEYW, KEYW, KEYW, KEYW, HEAD_DIM, KEYW, KEYW)
    consts = (n1g, wqkv, wz, wba, wsc, wfn, dnconv, scconv, alog_row, dtb_row)
    return pl.pallas_call(
        functools.partial(_inproj_kernel, nt_lat),
        grid=(bsz // bb, nt),
        in_specs=_x_specs(bb, nt_lat, ctx_blk, d) + [_mod_spec(bb, nt_lat, d)] + [_const_spec(a) for a in consts],
        out_specs=[tok(w) for w in out_w],
        out_shape=[jax.ShapeDtypeStruct((bsz, ltot, w), F32) for w in out_w],
        compiler_params=_cparams(2),
        name="in_proj",
    )(x_src, ctx_src, mod_sel, *consts)


def _split2(t):
    hi = t.astype(BF16)
    return hi, (t - hi.astype(F32)).astype(BF16)


def _mm3(lhs, rhs):
    m = lhs.shape[0]
    lh, ll = _split2(lhs)
    rh, rl = _split2(rhs)
    top = _dot(jnp.concatenate([lh, ll], axis=0), rh)
    return top[:m] + top[m:] + _dot(lh, rl)


def _dnprep_kernel(cps, q_ref, k_ref, v_ref, bg_ref, u_ref, w_ref, qg_ref, kd_ref, attn_ref, adec_ref):
    ri = lax.broadcasted_iota(jnp.int32, (CHUNK, PACKW), 0)
    cj = lax.broadcasted_iota(jnp.int32, (CHUNK, PACKW), 1) & (CHUNK - 1)
    eye_p = jnp.where(ri == cj, 1.0, 0.0).astype(F32)
    incl_m = ((ri >= cj), (ri <= cj))
    strict_m = ((ri > cj), (ri < cj))

    def half_masked(t, hd):
        lane = lax.broadcasted_iota(jnp.int32, t.shape, 1)
        keep = (lane < CHUNK) if hd % 2 == 0 else (lane >= CHUNK)
        return jnp.where(keep, t, jnp.zeros((), BF16))

    def blockdiag(t):
        z = jnp.zeros((CHUNK, HEAD_DIM), BF16)
        blocks = []
        for hd in range(HEADS):
            piece = half_masked(t[:, (hd // 2) * HEAD_DIM:(hd // 2 + 1) * HEAD_DIM], hd)
            blocks.append(jnp.concatenate([piece, z] if hd < 2 else [z, piece], axis=1))
        return jnp.concatenate(blocks, axis=0)

    def mm3_all(lhs_list, rhs_list):
        lsp = [_split2(l) for l in lhs_list]
        rsp = [tuple(blockdiag(p) for p in _split2(r)) for r in rhs_list]
        tops = [_dot(jnp.concatenate([lh, ll], axis=0), rh) for (lh, ll), (rh, _) in zip(lsp, rsp)]
        lows = [_dot(lh, rl) for (lh, _), (_, rl) in zip(lsp, rsp)]
        return [t[:l.shape[0]] + t[l.shape[0]:] + lo for t, lo, l in zip(tops, lows, lhs_list)]

    for first in range(0, cps, PREP_GROUP):
        _dnprep_group(first, min(PREP_GROUP, cps - first), q_ref, k_ref, v_ref, bg_ref,
                      u_ref, w_ref, qg_ref, kd_ref, attn_ref, adec_ref,
                      eye_p, incl_m, strict_m, half_masked, blockdiag, mm3_all)


def _dnprep_group(first, cps, q_ref, k_ref, v_ref, bg_ref, u_ref, w_ref, qg_ref, kd_ref, attn_ref, adec_ref,
                  eye_p, incl_m, strict_m, half_masked, blockdiag, mm3_all):
    chunk_rows = [slice((first + ci) * CHUNK, (first + ci + 1) * CHUNK) for ci in range(cps)]
    qs = [q_ref[r, :] for r in chunk_rows]
    ks = [k_ref[r, :] for r in chunk_rows]
    vs = [v_ref[r, :] for r in chunk_rows]

    lo_half = lax.broadcasted_iota(jnp.int32, (CHUNK, HEAD_DIM), 1) < CHUNK
    expanded = {}
    for ci, r in enumerate(chunk_rows):
        bg = bg_ref[r, :]
        for dr in range(2):
            cols = [(jnp.broadcast_to(bg[:, N_DH + dr * HEADS + hd:N_DH + dr * HEADS + hd + 1], (CHUNK, HEAD_DIM)),
                     jnp.broadcast_to(bg[:, dr * HEADS + hd:dr * HEADS + hd + 1], (CHUNK, HEAD_DIM)))
                    for hd in range(HEADS)]
            gcol = jnp.concatenate([c[0] for c in cols], axis=1)
            beta = jnp.concatenate([c[1] for c in cols], axis=1)
            gcol_p = jnp.concatenate([jnp.where(lo_half, cols[0][0], cols[1][0]),
                                      jnp.where(lo_half, cols[2][0], cols[3][0])], axis=1)
            expanded[ci, dr] = (gcol, beta, gcol_p)

    def parts(ci, dr):
        return expanded[ci, dr]

    m1s = []
    for ci in range(cps):
        k = ks[ci]
        k_t = jnp.concatenate([k, k], axis=0).T.astype(BF16)
        zk = jnp.zeros((HEAD_DIM, HEAD_DIM), BF16)
        bd_rows = []
        for hd in range(HEADS):
            piece = half_masked(k_t[hd * HEAD_DIM:(hd + 1) * HEAD_DIM, :], hd)
            bd_rows.append(jnp.concatenate([piece, zk] if hd < 2 else [zk, piece], axis=1))
        bd_k = jnp.concatenate(bd_rows, axis=0)
        kbs = [k * parts(ci, dr)[1] for dr in range(2)]
        m1s.append(_dot(jnp.concatenate(kbs + [qs[ci]], axis=0).astype(BF16), bd_k))

    chains = [(ci, dr) for ci in range(cps) for dr in range(2)]
    decays, neg_as = [], []
    for ci, dr in chains:
        gcol_p = parts(ci, dr)[2]
        grow_p = jnp.sum(gcol_p * eye_p, axis=0, keepdims=True)
        decay = jnp.exp(jnp.where(incl_m[dr], gcol_p - grow_p, -jnp.inf))
        decays.append(decay)
        neg_as.append(jnp.where(strict_m[dr], -(m1s[ci][dr * CHUNK:(dr + 1) * CHUNK] * decay), 0.0))

    p_accs = [eye_p + a for a in neg_as]
    q_pows = mm3_all(neg_as, neg_as)
    for _ in range(4):
        boths = mm3_all([jnp.concatenate([qp, pa], axis=0) for qp, pa in zip(q_pows, p_accs)], q_pows)
        q_pows = [b[:CHUNK] for b in boths]
        p_accs = [pa + b[CHUNK:] for pa, b in zip(p_accs, boths)]
    lasts = mm3_all(p_accs, q_pows)
    p_accs = [pa + la for pa, la in zip(p_accs, lasts)]

    uws, egs = [], []
    for (ci, dr), t_inv in zip(chains, p_accs):
        gcol, beta, _ = parts(ci, dr)
        eg = jnp.exp(gcol)
        egs.append(eg)
        vb = (vs[ci] * beta).astype(BF16)
        kbe = (ks[ci] * (beta * eg)).astype(BF16)
        rhs = jnp.concatenate(
            [jnp.concatenate([vb[:, hd * HEAD_DIM:(hd + 1) * HEAD_DIM], kbe[:, hd * HEAD_DIM:(hd + 1) * HEAD_DIM]], axis=1)
             for hd in range(HEADS)], axis=0)
        res = _dot(blockdiag(t_inv.astype(BF16)), rhs)
        uws.append(jnp.concatenate(
            [res[hd * CHUNK:(hd + 1) * CHUNK, :HEAD_DIM] for hd in range(HEADS)]
            + [res[hd * CHUNK:(hd + 1) * CHUNK, HEAD_DIM:] for hd in range(HEADS)], axis=1))

    for (ci, dr), uw, eg, decay in zip(chains, uws, egs, decays):
        rows = chunk_rows[ci]
        gcol = parts(ci, dr)[0]
        g_last = gcol[CHUNK - 1:CHUNK, :] if dr == 0 else gcol[0:1, :]
        u_ref[dr, rows, :] = uw[:, :KEYW].astype(BF16)
        w_ref[dr, rows, :] = uw[:, KEYW:].astype(BF16)
        qg_ref[dr, rows, :] = (qs[ci] * eg).astype(BF16)
        kd_ref[dr, rows, :] = (ks[ci] * jnp.exp(g_last - gcol)).astype(BF16)
        attn_ref[dr, rows, :] = (m1s[ci][2 * CHUNK:] * decay).astype(BF16)
        a_dec = jnp.exp(g_last)
        for hd in range(HEADS):
            unit = dr * HEADS + hd
            adec_ref[first + ci, unit:unit + 1, :] = a_dec[:, hd * HEAD_DIM:(hd + 1) * HEAD_DIM]


def _dnprep_call(q, k, v, bg):
    bsz, ltot, _ = q.shape
    nch = ltot // CHUNK
    cps = PREP_CHUNKS
    assert nch % cps == 0
    tok = lambda w: pl.BlockSpec((None, cps * CHUNK, w), lambda b, n: (b, n, 0))
    dtok = lambda w: pl.BlockSpec((None, 2, cps * CHUNK, w), lambda b, n: (b, 0, n, 0))
    return pl.pallas_call(
        functools.partial(_dnprep_kernel, cps),
        grid=(bsz, nch // cps),
        in_specs=[tok(KEYW), tok(KEYW), tok(KEYW), tok(HEAD_DIM)],
        out_specs=[dtok(KEYW), dtok(KEYW), dtok(KEYW), dtok(KEYW), dtok(PACKW),
                   pl.BlockSpec((None, cps, N_DH, HEAD_DIM), lambda b, n: (b, n, 0, 0))],
        out_shape=[jax.ShapeDtypeStruct((bsz, 2, ltot, KEYW), BF16),
                   jax.ShapeDtypeStruct((bsz, 2, ltot, KEYW), BF16),
                   jax.ShapeDtypeStruct((bsz, 2, ltot, KEYW), BF16),
                   jax.ShapeDtypeStruct((bsz, 2, ltot, KEYW), BF16),
                   jax.ShapeDtypeStruct((bsz, 2, ltot, PACKW), BF16),
                   jax.ShapeDtypeStruct((bsz, nch, N_DH, HEAD_DIM), F32)],
        compiler_params=_cparams(2),
        name="dn_prep",
    )(q, k, v, bg)


def _dnscan_kernel(uf_ref, wf_ref, qgf_ref, kdf_ref, af_ref, df_ref,
                   ub_ref, wb_ref, qgb_ref, kdb_ref, ab_ref, db_ref,
                   of_ref, ob_ref, s_ref):
    @pl.when(pl.program_id(1) == 0)
    def _():
        s_ref[...] = jnp.zeros_like(s_ref)

    per_dir = ((uf_ref, wf_ref, qgf_ref, kdf_ref, af_ref, df_ref, of_ref),
               (ub_ref, wb_ref, qgb_ref, kdb_ref, ab_ref, db_ref, ob_ref))
    units = [(bi, dr, hd) for bi in range(uf_ref.shape[0]) for dr in range(2) for hd in range(HEADS)]
    states = [s_ref[bi, dr * HEADS + hd] for bi, dr, hd in units]
    n_sub = uf_ref.shape[1] // CHUNK
    for sub in range(n_sub):
        pos = (sub, n_sub - 1 - sub)
        rows = [slice(p * CHUNK, (p + 1) * CHUNK) for p in pos]
        first = []
        for (bi, dr, hd), s in zip(units, states):
            _, w_ref, qg_ref = per_dir[dr][:3]
            lo = hd * HEAD_DIM
            first.append(_dot(jnp.concatenate([w_ref[bi, rows[dr], lo:lo + HEAD_DIM],
                                               qg_ref[bi, rows[dr], lo:lo + HEAD_DIM]], axis=0), s.astype(BF16)))
        new_states = []
        for (bi, dr, hd), both, s in zip(units, first, states):
            u_ref, _, _, kd_ref, a_ref, d_ref, o_ref = per_dir[dr]
            unit = dr * HEADS + hd
            lo = hd * HEAD_DIM
            v_new_b = (u_ref[bi, rows[dr], lo:lo + HEAD_DIM].astype(F32) - both[:CHUNK]).astype(BF16)
            o_ref[bi, rows[dr], lo:lo + HEAD_DIM] = (
                both[CHUNK:] + _dot(a_ref[bi, rows[dr], hd * CHUNK:(hd + 1) * CHUNK], v_new_b)).astype(BF16)
            new_states.append(s * d_ref[bi, pos[dr], unit:unit + 1, :]
                              + _dot_tn(kd_ref[bi, rows[dr], lo:lo + HEAD_DIM], v_new_b))
        states = new_states
    for (bi, dr, hd), s in zip(units, states):
        s_ref[bi, dr * HEADS + hd] = s


def _dnscan_call(u, w, qg, kd, attn, adec, n_lat_chunks):
    bsz, _, ltot, _ = u.shape
    nch = ltot // CHUNK
    bb = SCAN_BATCH if bsz % SCAN_BATCH == 0 else 1
    cpg = SCAN_CHUNKS if (nch % SCAN_CHUNKS == 0 and n_lat_chunks % SCAN_CHUNKS == 0) else 1
    nblk = nch // cpg
    fwd = lambda n: (n + n_lat_chunks // cpg) % nblk
    bwd = lambda n: nblk - 1 - n

    def dspec(wd, dr, order):
        return pl.BlockSpec((bb, None, cpg * CHUNK, wd), lambda b, n: (b, dr, order(n), 0))

    def aspec(order):
        return pl.BlockSpec((bb, cpg, N_DH, HEAD_DIM), lambda b, n: (b, order(n), 0, 0))

    def ospec(order):
        return pl.BlockSpec((bb, cpg * CHUNK, KEYW), lambda b, n: (b, order(n), 0))

    in_specs = []
    for dr, order in ((0, fwd), (1, bwd)):
        in_specs += [dspec(KEYW, dr, order), dspec(KEYW, dr, order), dspec(KEYW, dr, order), dspec(KEYW, dr, order),
                     dspec(PACKW, dr, order), aspec(order)]
    return pl.pallas_call(
        _dnscan_kernel,
        grid=(bsz // bb, nblk),
        in_specs=in_specs,
        out_specs=[ospec(fwd), ospec(bwd)],
        out_shape=[jax.ShapeDtypeStruct((bsz, ltot, KEYW), BF16)] * 2,
        scratch_shapes=[pltpu.VMEM((bb, N_DH, HEAD_DIM, HEAD_DIM), F32)],
        compiler_params=_cparams(2),
        name="dn_scan",
    )(u, w, qg, kd, attn, adec, u, w, qg, kd, attn, adec)


def _dft_mats(n):
    idx = np.arange(n)
    ang = 2.0 * np.pi * ((idx[:, None] * idx[None, :]) % n) / n
    return np.cos(ang), np.sin(ang)


def _fft1_kernel(n2blk, u_ref, f1_ref, tc_ref, ts_ref, br_ref, bi_ref):
    l1 = u_ref.shape[1]
    rows = l1 * n2blk
    prods = [_dot(f1_ref[...], u_ref[s].reshape(rows, KEYW).astype(BF16)) for s in range(u_ref.shape[0])]
    tc = jnp.concatenate([tc_ref[...].reshape(rows, HEAD_DIM)] * (KEYW // HEAD_DIM), axis=1)
    ts = jnp.concatenate([ts_ref[...].reshape(rows, HEAD_DIM)] * (KEYW // HEAD_DIM), axis=1)
    for s, a in enumerate(prods):
        ar, ai = a[:rows], a[rows:]
        br_ref[s] = (ar * tc + ai * ts).reshape(n2blk, l1, KEYW)
        bi_ref[s] = (ai * tc - ar * ts).reshape(n2blk, l1, KEYW)


def _fft2_kernel(k1blk, nlen, br_ref, bi_ref, f2_ref, wc_ref, y_ref):
    rows = k1blk * nlen
    n_rows = br_ref.shape[0]
    xs = [_dot(f2_ref[...], jnp.concatenate([br_ref[s].reshape(rows, KEYW), bi_ref[s].reshape(rows, KEYW)],
                                            axis=0).astype(BF16)) for s in range(n_rows)]
    for s, x in enumerate(xs):
        xr, xi = x[:rows].astype(BF16), x[rows:].astype(BF16)
        y = jnp.concatenate(
            [_dot(jnp.concatenate([xr[:, g * HEAD_DIM:(g + 1) * HEAD_DIM], xi[:, g * HEAD_DIM:(g + 1) * HEAD_DIM]],
                                  axis=1), wc_ref[...]) for g in range(KEYW // HEAD_DIM)], axis=1)
        y_ref[s] = y.reshape(nlen, k1blk, KEYW)


def _dftctx_kernel(nlen, u_ref, f_ref, wc_ref, y_ref):
    x = _dot(f_ref[...], u_ref[...].astype(BF16))
    xr, xi = x[:nlen].astype(BF16), x[nlen:].astype(BF16)
    for g in range(KEYW // HEAD_DIM):
        gs = slice(g * HEAD_DIM, (g + 1) * HEAD_DIM)
        y_ref[:, gs] = _dot(jnp.concatenate([xr[:, gs], xi[:, gs]], axis=1), wc_ref[...])


def _chan_mat(total_len):
    cc, sc = _dft_mats(HEAD_DIM)
    return jnp.asarray(np.concatenate([cc, sc], axis=0) / math.sqrt(total_len * HEAD_DIM), BF16)


def _fourier_lat_call(fn_cat, l_lat):
    bsz, ltot, _ = fn_cat.shape
    l1 = l_lat // CHUNK
    n2blk = 8
    k1blk = 8 if l1 % 8 == 0 else l1
    c1, s1 = _dft_mats(l1)
    eye_n2 = np.eye(n2blk)
    kron1 = lambda f: np.einsum("kn,ab->aknb", f, eye_n2).reshape(n2blk * l1, l1 * n2blk)
    f1 = jnp.asarray(np.concatenate([kron1(c1), kron1(-s1)], axis=0), BF16)
    n2 = np.arange(CHUNK)[:, None, None]
    k1 = np.arange(l1)[None, :, None]
    ang = 2.0 * np.pi * (n2 * k1) / l_lat * np.ones((1, 1, HEAD_DIM))
    tc, ts = jnp.asarray(np.cos(ang), F32), jnp.asarray(np.sin(ang), F32)
    c2, s2 = _dft_mats(CHUNK)
    eye_k1 = np.eye(k1blk)
    kron2 = lambda f: np.kron(f, eye_k1)
    f2 = jnp.asarray(np.block([[kron2(c2), kron2(s2)], [kron2(-s2), kron2(c2)]]), BF16)
    wc = _chan_mat(l_lat)

    u_view = fn_cat.reshape(bsz, ltot // CHUNK, CHUNK, KEYW)
    bb = ROW_BATCH if bsz % ROW_BATCH == 0 else 1
    mid_spec = pl.BlockSpec((bb, n2blk, l1, KEYW), lambda b, i: (b, i, 0, 0))
    br, bi = pl.pallas_call(
        functools.partial(_fft1_kernel, n2blk),
        grid=(bsz // bb, CHUNK // n2blk),
        in_specs=[pl.BlockSpec((bb, l1, n2blk, KEYW), lambda b, i: (b, 0, i, 0)),
                  pl.BlockSpec(f1.shape, lambda b, i: (0, 0)),
                  pl.BlockSpec((n2blk, l1, HEAD_DIM), lambda b, i: (i, 0, 0)),
                  pl.BlockSpec((n2blk, l1, HEAD_DIM), lambda b, i: (i, 0, 0))],
        out_specs=[mid_spec, mid_spec],
        out_shape=[jax.ShapeDtypeStruct((bsz, CHUNK, l1, KEYW), F32)] * 2,
        compiler_params=_cparams(2),
        name="fourier_stage1",
    )(u_view, f1, tc, ts)

    rspec = pl.BlockSpec((bb, CHUNK, k1blk, KEYW), lambda b, i: (b, 0, i, 0))
    y = pl.pallas_call(
        functools.partial(_fft2_kernel, k1blk, CHUNK),
        grid=(bsz // bb, l1 // k1blk),
        in_specs=[rspec, rspec,
                  pl.BlockSpec(f2.shape, lambda b, i: (0, 0)),
                  pl.BlockSpec(wc.shape, lambda b, i: (0, 0))],
        out_specs=rspec,
        out_shape=jax.ShapeDtypeStruct((bsz, CHUNK, l1, KEYW), F32),
        compiler_params=_cparams(2),
        name="fourier_stage2",
    )(br, bi, f2, wc)
    return y.reshape(bsz, l_lat, KEYW)


def _fourier_ctx_call(fn_cat, l_lat):
    bsz, ltot, _ = fn_cat.shape
    lc = ltot - l_lat
    c, s = _dft_mats(lc)
    f = jnp.asarray(np.concatenate([c, -s], axis=0), BF16)
    wc = _chan_mat(lc)
    return pl.pallas_call(
        functools.partial(_dftctx_kernel, lc),
        grid=(bsz,),
        in_specs=[pl.BlockSpec((None, lc, KEYW), lambda b: (b, l_lat // lc, 0)),
                  pl.BlockSpec(f.shape, lambda b: (0, 0)),
                  pl.BlockSpec(wc.shape, lambda b: (0, 0))],
        out_specs=pl.BlockSpec((None, lc, KEYW), lambda b: (b, 0, 0)),
        out_shape=jax.ShapeDtypeStruct((bsz, lc, KEYW), F32),
        compiler_params=_cparams(1),
        name="fourier_ctx",
    )(fn_cat, f, wc)


def _merge_kernel(nt_lat, x_ref, ctx_ref, mod_ref, of_ref, ob_ref, z_ref, sc_ref, fl_ref, fc_ref,
                  g_ref, wgate_ref, bgate_ref, og_ref, wdn_ref, wsc_ref, wfn_ref, wo_ref, o_ref):
    is_lat = pl.program_id(1) < nt_lat
    n_rows = x_ref.shape[0]
    xs = [jnp.where(is_lat, x_ref[s], ctx_ref[s]) for s in range(n_rows)]
    hs = [_modulated(xs[s], g_ref, mod_ref[s, 0:1, :], mod_ref[s, 1:2, :]).astype(BF16) for s in range(n_rows)]
    logits = [_dot(h, wgate_ref[...]) for h in hs]
    branch_in = []
    for s in range(n_rows):
        o = of_ref[s].astype(F32) + ob_ref[s].astype(F32)
        z = z_ref[s]
        dn_parts = []
        for hd in range(HEADS):
            sl = slice(hd * HEAD_DIM, (hd + 1) * HEAD_DIM)
            dn_parts.append(_rms(o[:, sl], HEAD_DIM) * og_ref[...] * _silu(z[:, sl]))
        fn = jnp.where(is_lat, fl_ref[s], fc_ref[s])
        branch_in.append((jnp.concatenate(dn_parts, axis=1).astype(BF16), sc_ref[s].astype(BF16), fn.astype(BF16)))
    ys = [(_dot(dn, wdn_ref[...]), _dot(sc, wsc_ref[...]), _dot(fn, wfn_ref[...])) for dn, sc, fn in branch_in]
    merged = []
    for s in range(n_rows):
        gates = jax.nn.sigmoid(logits[s] + bgate_ref[...])
        y_dn, y_sc, y_fn = ys[s]
        merged.append((gates[:, :D_MODEL] * y_dn + gates[:, D_MODEL:2 * D_MODEL] * y_sc
                       + gates[:, 2 * D_MODEL:] * y_fn).astype(BF16))
    mixes = [_dot(m, wo_ref[...]) for m in merged]
    for s in range(n_rows):
        o_ref[s] = xs[s] + mod_ref[s, 2:3, :] * mixes[s]


def _merge_call(x_src, ctx_src, ctx_blk, mod_sel, n1g, o_f, o_b, z, sc, fn_lat, fn_ctx, wgate, bgate, og, wdn, wsc,
                wfn, wo, nt_lat, nt_run):
    bsz, _, d = x_src.shape
    ltot = (nt_lat + 1) * TILE
    bb = ROW_BATCH if bsz % ROW_BATCH == 0 else 1
    tok = lambda w: pl.BlockSpec((bb, TILE, w), lambda b, j: (b, j, 0))
    consts = (n1g, wgate, bgate, og, wdn, wsc, wfn, wo)
    return pl.pallas_call(
        functools.partial(_merge_kernel, nt_lat),
        grid=(bsz // bb, nt_run),
        in_specs=_x_specs(bb, nt_lat, ctx_blk, d) + [_mod_spec(bb, nt_lat, d),
                  tok(KEYW), tok(KEYW), tok(KEYW), tok(KEYW),
                  pl.BlockSpec((bb, TILE, KEYW), lambda b, j: (b, jnp.minimum(j, nt_lat - 1), 0)),
                  pl.BlockSpec((bb, TILE, KEYW), lambda b, j: (b, 0, 0))] + [_const_spec(a) for a in consts],
        out_specs=tok(d),
        out_shape=jax.ShapeDtypeStruct((bsz, ltot, d), F32),
        compiler_params=_cparams(2),
        name="branch_merge",
    )(x_src, ctx_src, mod_sel, o_f, o_b, z, sc, fn_lat, fn_ctx, *consts)


def _ffn_kernel(final, x_ref, mod_ref, g_ref, win_ref, wout_ref, fg_ref, o_ref):
    n_rows = x_ref.shape[0]
    hs = [_modulated(x_ref[s], g_ref, mod_ref[s, 3:4, :], mod_ref[s, 4:5, :]).astype(BF16) for s in range(n_rows)]
    abs_ = [_dot(h, win_ref[...]) for h in hs]
    acts = [(_silu(ab[:, :D_FF]) * ab[:, D_FF:]).astype(BF16) for ab in abs_]
    downs = [_dot(act, wout_ref[...]) for act in acts]
    for s in range(n_rows):
        y = x_ref[s] + mod_ref[s, 5:6, :] * downs[s]
        if final:
            y = _rms(y, D_MODEL) * fg_ref[...]
        o_ref[s] = y


def _ffn_call(x_cat, mod_sel, n2g, win, wout, fg, nt_lat, nt_run, final):
    bsz, ltot, d = x_cat.shape
    out_len = nt_run * TILE if final else ltot
    bb = ROW_BATCH if bsz % ROW_BATCH == 0 else 1
    tok = pl.BlockSpec((bb, TILE, d), lambda b, j: (b, j, 0))
    consts = (n2g, win, wout, fg)
    return pl.pallas_call(
        functools.partial(_ffn_kernel, final),
        grid=(bsz // bb, nt_run),
        in_specs=[tok, _mod_spec(bb, nt_lat, d)] + [_const_spec(a) for a in consts],
        out_specs=tok,
        out_shape=jax.ShapeDtypeStruct((bsz, out_len, d), F32),
        compiler_params=_cparams(2),
        name="swiglu_final" if final else "swiglu",
    )(x_cat, mod_sel, *consts)


def kernel(x, c, ctx, c_ctx, w_mod, b_mod, norm1_g, w_in, dn_conv_w, dn_a_log, dn_dt_bias, dn_onorm_g, w_dn_out,
           sc_conv_w, w_sc_out, w_fn_out, w_gate, b_gate, w_o, norm2_g, w_ffn_in, w_ffn_out, final_g):
    bsz, l_lat, d = x.shape
    lc = ctx.shape[1]
    assert d == D_MODEL and lc == TILE and l_lat % TILE == 0 and w_ffn_in.shape[-1] == 2 * D_FF
    depth = w_mod.shape[0]
    nt_lat = l_lat // TILE
    nt_all = nt_lat + 1
    n_lat_chunks = l_lat // CHUNK

    rows = ((bsz + 1 + 7) // 8) * 8
    c_rows = jnp.zeros((rows, d), F32).at[:bsz].set(c).at[bsz].set(c_ctx)
    mod = _mod_call(c_rows, w_mod, b_mod).reshape(depth, rows, 6, d)
    mod_sel = jnp.stack([mod[:, :bsz], jnp.broadcast_to(mod[:, bsz:bsz + 1], (depth, bsz, 6, d))], axis=2)

    x_src, ctx_src, ctx_blk = x, ctx, 0
    row = lambda a: a.reshape(1, -1)
    lane_pad = lambda a: jnp.zeros((1, HEAD_DIM), F32).at[0, N_DH:2 * N_DH].set(a.reshape(-1))
    for layer in range(depth):
        last = layer == depth - 1
        wl = w_in[layer]
        qkv_w = 3 * KEYW
        o_z, o_ba, o_sc, o_fn = qkv_w, qkv_w + KEYW, qkv_w + KEYW + 2 * N_DH, qkv_w + KEYW + 2 * N_DH + 3 * KEYW
        wba = jnp.zeros((d, HEAD_DIM), F32).at[:, :2 * N_DH].set(wl[:, o_ba:o_sc])
        wparts = tuple(a.astype(BF16) for a in (wl[:, :o_z], wl[:, o_z:o_ba], wba, wl[:, o_sc:o_fn], wl[:, o_fn:]))
        q, k, v, z, bg, sc, fn = _inproj_call(
            x_src, ctx_src, ctx_blk, mod_sel[layer], row(norm1_g[layer]), wparts, dn_conv_w[layer], sc_conv_w[layer],
            lane_pad(dn_a_log[layer]), lane_pad(dn_dt_bias[layer]), nt_lat)
        u, w, qg, kd, attn, adec = _dnprep_call(q, k, v, bg)
        o_f, o_b = _dnscan_call(u, w, qg, kd, attn, adec, n_lat_chunks)
        fn_lat = _fourier_lat_call(fn, l_lat)
        nt_run = nt_lat if last else nt_all
        fn_ctx = jnp.zeros((bsz, lc, KEYW), F32) if last else _fourier_ctx_call(fn, l_lat)
        x_cat = _merge_call(
            x_src, ctx_src, ctx_blk, mod_sel[layer], row(norm1_g[layer]), o_f, o_b, z, sc, fn_lat, fn_ctx,
            w_gate[layer].astype(BF16), row(b_gate[layer]), row(dn_onorm_g[layer]),
            w_dn_out[layer].astype(BF16), w_sc_out[layer].astype(BF16), w_fn_out[layer].astype(BF16),
            w_o[layer].astype(BF16), nt_lat, nt_run)
        x_cat = _ffn_call(x_cat, mod_sel[layer], row(norm2_g[layer]), w_ffn_in[layer].astype(BF16),
                          w_ffn_out[layer].astype(BF16), row(final_g), nt_lat, nt_run, last)
        x_src, ctx_src, ctx_blk = x_cat, x_cat, nt_lat
    return x_cat
```

```python
import functools
import math

import numpy as np
import jax
import jax.numpy as jnp
from jax import lax
from jax.experimental import pallas as pl
from jax.experimental.pallas import tpu as pltpu

F32 = jnp.float32
BF16 = jnp.bfloat16
HIGHEST = lax.Precision.HIGHEST

EPS = 1e-6
D_MODEL = 1024
DEPTH = 2
HEADS = 4
HEAD_DIM = 128
KEYW = HEADS * HEAD_DIM
CHUNK = 64
D_FF = 2816
TILE = 256
N_DH = 2 * HEADS
PACKW = HEADS * CHUNK
PREP_CHUNKS = 12
PREP_GROUPS = (2, 5, 5)
SCAN_BATCH = 8
SCAN_CHUNKS = 4
ROW_BATCH = 2
VMEM_LIMIT = 56 * 1024 * 1024


def _cparams(n_axes):
    return pltpu.CompilerParams(dimension_semantics=("arbitrary",) * n_axes, vmem_limit_bytes=VMEM_LIMIT)


def _dot(a, b):
    return jnp.dot(a, b, preferred_element_type=F32)


def _dot_hi(a, b):
    return jnp.dot(a, b, preferred_element_type=F32, precision=HIGHEST)


def _dot_nt(a, b):
    return lax.dot_general(a, b, (((1,), (1,)), ((), ())), preferred_element_type=F32)


def _dot_tn(a, b):
    return lax.dot_general(a, b, (((0,), (0,)), ((), ())), preferred_element_type=F32)


def _silu(t):
    return t * jax.nn.sigmoid(t)


def _rms(t, axis_size):
    return t * lax.rsqrt(jnp.sum(t * t, axis=-1, keepdims=True) * (1.0 / axis_size) + EPS)


def _mod_kernel(c_ref, w_ref, b_ref, o_ref):
    o_ref[...] = _dot_hi(_silu(c_ref[...]), w_ref[...]) + b_ref[...]


def _mod_call(c_rows, w_mod, b_mod):
    depth, d, n = w_mod.shape
    rows = c_rows.shape[0]
    tn = 1536
    return pl.pallas_call(
        _mod_kernel,
        grid=(depth, n // tn),
        in_specs=[
            pl.BlockSpec((rows, d), lambda l, j: (0, 0)),
            pl.BlockSpec((None, d, tn), lambda l, j: (l, 0, j)),
            pl.BlockSpec((None, 1, tn), lambda l, j: (l, 0, j)),
        ],
        out_specs=pl.BlockSpec((None, rows, tn), lambda l, j: (l, 0, j)),
        out_shape=jax.ShapeDtypeStruct((depth, rows, n), F32),
        compiler_params=_cparams(2),
        name="mod_vectors",
    )(c_rows, w_mod, b_mod.reshape(depth, 1, n))


def _row_masks(is_lat):
    pos = lax.broadcasted_iota(jnp.int32, (TILE, 1), 0)
    row_mask = jnp.where(is_lat, CHUNK - 1, TILE - 1)
    in_row = pos & row_mask
    return jnp.where(in_row == 0, 0.0, 1.0).astype(F32), jnp.where(in_row == row_mask, 0.0, 1.0).astype(F32)


def _mask_rows(t, mask, first):
    pieces = []
    for g in range(TILE // CHUNK):
        lo, hi = g * CHUNK, (g + 1) * CHUNK
        if first:
            pieces += [t[lo:lo + 8] * mask[lo:lo + 8], t[lo + 8:hi]]
        else:
            pieces += [t[lo:hi - 8], t[hi - 8:hi] * mask[hi - 8:hi]]
    return jnp.concatenate(pieces, axis=0)


def _conv3(t, w_ref, m_prev, m_next):
    prev = _mask_rows(pltpu.roll(t, 1, 0), m_prev, True)
    nxt = _mask_rows(pltpu.roll(t, TILE - 1, 0), m_next, False)
    return prev * w_ref[0:1, :] + t * w_ref[1:2, :] + nxt * w_ref[2:3, :]


def _modulated(x, g_ref, shift, scale):
    return _rms(x, D_MODEL) * (g_ref[...] * (1.0 + scale)) + shift


def _inproj_kernel(nt_lat, x_ref, ctx_ref, mod_ref, g_ref, wqkv_ref, wz_ref, wba_ref, wsc_ref, wfn_ref,
                   dnconv_ref, scconv_ref, alog_ref, dtb_ref,
                   q_ref, k_ref, v_ref, z_ref, bg_ref, sc_ref, fn_ref):
    is_lat = pl.program_id(1) < nt_lat
    m_prev, m_next = _row_masks(is_lat)
    n_rows = x_ref.shape[0]
    hs = []
    for s in range(n_rows):
        x = jnp.where(is_lat, x_ref[s], ctx_ref[s])
        hs.append(_modulated(x, g_ref, mod_ref[s, 0:1, :], mod_ref[s, 1:2, :]).astype(BF16))
    projs = [tuple(_dot(h, w_ref[...]) for w_ref in (wqkv_ref, wz_ref, wba_ref, wsc_ref, wfn_ref)) for h in hs]
    for s in range(n_rows):
        p_qkv, p_z, ba, scp, p_fn = projs[s]
        qkv = _silu(_conv3(p_qkv, dnconv_ref, m_prev, m_next))
        for hd in range(HEADS):
            lo = hd * HEAD_DIM
            qh = qkv[:, lo:lo + HEAD_DIM]
            kh = qkv[:, KEYW + lo:KEYW + lo + HEAD_DIM]
            q_ref[s, :, lo:lo + HEAD_DIM] = (qh * lax.rsqrt(jnp.sum(qh * qh, axis=-1, keepdims=True) + EPS)
                                             * (HEAD_DIM ** -0.5))
            k_ref[s, :, lo:lo + HEAD_DIM] = kh * lax.rsqrt(jnp.sum(kh * kh, axis=-1, keepdims=True) + EPS)
        v_ref[s] = qkv[:, 2 * KEYW:]
        z_ref[s] = p_z

        sp_in = ba + dtb_ref[...]
        softplus = jnp.maximum(sp_in, 0.0) + jnp.log1p(jnp.exp(-jnp.abs(sp_in)))
        lane = lax.broadcasted_iota(jnp.int32, ba.shape, 1)
        g = -jnp.exp(alog_ref[...]) * softplus
        in_chunk = lax.broadcasted_iota(jnp.int32, (TILE, 1), 0) & (CHUNK - 1)
        pre = g
        step = 1
        while step < CHUNK:
            pre = pre + jnp.where(in_chunk >= step, pltpu.roll(pre, step, 0), 0.0)
            step *= 2
        total = jnp.concatenate(
            [jnp.broadcast_to(pre[c * CHUNK + CHUNK - 1:(c + 1) * CHUNK, :], (CHUNK, HEAD_DIM))
             for c in range(TILE // CHUNK)], axis=0)
        suf = (total - pre) + g
        gc = jnp.where(lane < N_DH + HEADS, pre, suf)
        bg_ref[s] = jnp.where(lane < N_DH, jax.nn.sigmoid(ba), gc)

        sc_ref[s] = scp[:, :KEYW] * _conv3(scp[:, KEYW:2 * KEYW] * scp[:, 2 * KEYW:], scconv_ref, m_prev, m_next)
        fn_ref[s] = p_fn


def _const_spec(a):
    return pl.BlockSpec(a.shape, lambda b, j: (0,) * a.ndim, pipeline_mode=pl.Buffered(1))


def _mod_spec(bb, nt_lat, d):
    return pl.BlockSpec((bb, None, 6, d), lambda b, j: (b, jnp.where(j < nt_lat, 0, 1), 0, 0))


def _x_specs(bb, nt_lat, ctx_blk, d):
    return [pl.BlockSpec((bb, TILE, d), lambda b, j: (b, jnp.minimum(j, nt_lat - 1), 0)),
            pl.BlockSpec((bb, TILE, d), lambda b, j: (b, ctx_blk, 0))]


def _inproj_call(x_src, ctx_src, ctx_blk, mod_sel, n1g, wparts, dnconv, scconv, alog_row, dtb_row, nt_lat):
    bsz, _, d = x_src.shape
    nt = nt_lat + 1
    ltot = nt * TILE
    bb = ROW_BATCH if bsz % ROW_BATCH == 0 else 1
    wqkv, wz, wba, wsc, wfn = wparts
    tok = lambda w: pl.BlockSpec((bb, TILE, w), lambda b, j: (b, j, 0))
    out_w = (KEYW, KEYW, KEYW, KEYW, HEAD_DIM, KEYW, KEYW)
    consts = (n1g, wqkv, wz, wba, wsc, wfn, dnconv, scconv, alog_row, dtb_row)
    return pl.pallas_call(
        functools.partial(_inproj_kernel, nt_lat),
        grid=(bsz // bb, nt),
        in_specs=_x_specs(bb, nt_lat, ctx_blk, d) + [_mod_spec(bb, nt_lat, d)] + [_const_spec(a) for a in consts],
        out_specs=[tok(w) for w in out_w],
        out_shape=[jax.ShapeDtypeStruct((bsz, ltot, w), F32) for w in out_w],
        compiler_params=_cparams(2),
        name="in_proj",
    )(x_src, ctx_src, mod_sel, *consts)


def _split2(t):
    hi = t.astype(BF16)
    return hi, (t - hi.astype(F32)).astype(BF16)


def _mm3(lhs, rhs):
    m = lhs.shape[0]
    lh, ll = _split2(lhs)
    rh, rl = _split2(rhs)
    top = _dot(jnp.concatenate([lh, ll], axis=0), rh)
    return top[:m] + top[m:] + _dot(lh, rl)


def _dnprep_kernel(cps, q_ref, k_ref, v_ref, bg_ref, u_ref, w_ref, qg_ref, kd_ref, attn_ref, adec_ref):
    ri = lax.broadcasted_iota(jnp.int32, (CHUNK, PACKW), 0)
    cj = lax.broadcasted_iota(jnp.int32, (CHUNK, PACKW), 1) & (CHUNK - 1)
    eye_p = jnp.where(ri == cj, 1.0, 0.0).astype(F32)
    incl_m = ((ri >= cj), (ri <= cj))
    strict_m = ((ri > cj), (ri < cj))

    def half_masked(t, hd):
        lane = lax.broadcasted_iota(jnp.int32, t.shape, 1)
        keep = (lane < CHUNK) if hd % 2 == 0 else (lane >= CHUNK)
        return jnp.where(keep, t, jnp.zeros((), BF16))

    def blockdiag(t):
        z = jnp.zeros((CHUNK, HEAD_DIM), BF16)
        blocks = []
        for hd in range(HEADS):
            piece = half_masked(t[:, (hd // 2) * HEAD_DIM:(hd // 2 + 1) * HEAD_DIM], hd)
            blocks.append(jnp.concatenate([piece, z] if hd < 2 else [z, piece], axis=1))
        return jnp.concatenate(blocks, axis=0)

    def mm3_all(lhs_list, rhs_list):
        lsp = [_split2(l) for l in lhs_list]
        rsp = [tuple(blockdiag(p) for p in _split2(r)) for r in rhs_list]
        tops = [_dot(jnp.concatenate([lh, ll], axis=0), rh) for (lh, ll), (rh, _) in zip(lsp, rsp)]
        lows = [_dot(lh, rl) for (lh, _), (_, rl) in zip(lsp, rsp)]
        return [t[:l.shape[0]] + t[l.shape[0]:] + lo for t, lo, l in zip(tops, lows, lhs_list)]

    sizes = PREP_GROUPS if sum(PREP_GROUPS) == cps else (cps,)
    for first, size in zip(np.cumsum((0,) + sizes[:-1]), sizes):
        _dnprep_group(int(first), size, q_ref, k_ref, v_ref, bg_ref,
                      u_ref, w_ref, qg_ref, kd_ref, attn_ref, adec_ref,
                      eye_p, incl_m, strict_m, half_masked, blockdiag, mm3_all)


def _dnprep_group(first, cps, q_ref, k_ref, v_ref, bg_ref, u_ref, w_ref, qg_ref, kd_ref, attn_ref, adec_ref,
                  eye_p, incl_m, strict_m, half_masked, blockdiag, mm3_all):
    chunk_rows = [slice((first + ci) * CHUNK, (first + ci + 1) * CHUNK) for ci in range(cps)]
    qs = [q_ref[r, :] for r in chunk_rows]
    ks = [k_ref[r, :] for r in chunk_rows]
    vs = [v_ref[r, :] for r in chunk_rows]

    lo_half = lax.broadcasted_iota(jnp.int32, (CHUNK, HEAD_DIM), 1) < CHUNK
    expanded = {}
    for ci, r in enumerate(chunk_rows):
        bg = bg_ref[r, :]
        for dr in range(2):
            cols = [(jnp.broadcast_to(bg[:, N_DH + dr * HEADS + hd:N_DH + dr * HEADS + hd + 1], (CHUNK, HEAD_DIM)),
                     jnp.broadcast_to(bg[:, dr * HEADS + hd:dr * HEADS + hd + 1], (CHUNK, HEAD_DIM)))
                    for hd in range(HEADS)]
            gcol = jnp.concatenate([c[0] for c in cols], axis=1)
            beta = jnp.concatenate([c[1] for c in cols], axis=1)
            gcol_p = jnp.concatenate([jnp.where(lo_half, cols[0][0], cols[1][0]),
                                      jnp.where(lo_half, cols[2][0], cols[3][0])], axis=1)
            expanded[ci, dr] = (gcol, beta, gcol_p)

    def parts(ci, dr):
        return expanded[ci, dr]

    m1s = []
    for ci in range(cps):
        k = ks[ci]
        k_t = jnp.concatenate([k, k], axis=0).T.astype(BF16)
        zk = jnp.zeros((HEAD_DIM, HEAD_DIM), BF16)
        bd_rows = []
        for hd in range(HEADS):
            piece = half_masked(k_t[hd * HEAD_DIM:(hd + 1) * HEAD_DIM, :], hd)
            bd_rows.append(jnp.concatenate([piece, zk] if hd < 2 else [zk, piece], axis=1))
        bd_k = jnp.concatenate(bd_rows, axis=0)
        kbs = [k * parts(ci, dr)[1] for dr in range(2)]
        m1s.append(_dot(jnp.concatenate(kbs + [qs[ci]], axis=0).astype(BF16), bd_k))

    chains = [(ci, dr) for ci in range(cps) for dr in range(2)]
    decays, neg_as = [], []
    for ci, dr in chains:
        gcol_p = parts(ci, dr)[2]
        grow_p = jnp.sum(gcol_p * eye_p, axis=0, keepdims=True)
        decay = jnp.exp(jnp.where(incl_m[dr], gcol_p - grow_p, -jnp.inf))
        decays.append(decay)
        neg_as.append(jnp.where(strict_m[dr], -(m1s[ci][dr * CHUNK:(dr + 1) * CHUNK] * decay), 0.0))

    p_accs = [eye_p + a for a in neg_as]
    q_pows = mm3_all(neg_as, neg_as)
    for _ in range(4):
        boths = mm3_all([jnp.concatenate([qp, pa], axis=0) for qp, pa in zip(q_pows, p_accs)], q_pows)
        q_pows = [b[:CHUNK] for b in boths]
        p_accs = [pa + b[CHUNK:] for pa, b in zip(p_accs, boths)]
    lasts = mm3_all(p_accs, q_pows)
    p_accs = [pa + la for pa, la in zip(p_accs, lasts)]

    uws, egs = [], []
    for (ci, dr), t_inv in zip(chains, p_accs):
        gcol, beta, _ = parts(ci, dr)
        eg = jnp.exp(gcol)
        egs.append(eg)
        vb = (vs[ci] * beta).astype(BF16)
        kbe = (ks[ci] * (beta * eg)).astype(BF16)
        rhs = jnp.concatenate(
            [jnp.concatenate([vb[:, hd * HEAD_DIM:(hd + 1) * HEAD_DIM], kbe[:, hd * HEAD_DIM:(hd + 1) * HEAD_DIM]], axis=1)
             for hd in range(HEADS)], axis=0)
        res = _dot(blockdiag(t_inv.astype(BF16)), rhs)
        uws.append(jnp.concatenate(
            [res[hd * CHUNK:(hd + 1) * CHUNK, :HEAD_DIM] for hd in range(HEADS)]
            + [res[hd * CHUNK:(hd + 1) * CHUNK, HEAD_DIM:] for hd in range(HEADS)], axis=1))

    for (ci, dr), uw, eg, decay in zip(chains, uws, egs, decays):
        rows = chunk_rows[ci]
        gcol = parts(ci, dr)[0]
        g_last = gcol[CHUNK - 1:CHUNK, :] if dr == 0 else gcol[0:1, :]
        u_ref[dr, rows, :] = uw[:, :KEYW].astype(BF16)
        w_ref[dr, rows, :] = uw[:, KEYW:].astype(BF16)
        qg_ref[dr, rows, :] = (qs[ci] * eg).astype(BF16)
        kd_ref[dr, rows, :] = (ks[ci] * jnp.exp(g_last - gcol)).astype(BF16)
        attn_ref[dr, rows, :] = (m1s[ci][2 * CHUNK:] * decay).astype(BF16)
        a_dec = jnp.exp(g_last)
        for hd in range(HEADS):
            unit = dr * HEADS + hd
            adec_ref[first + ci, unit:unit + 1, :] = a_dec[:, hd * HEAD_DIM:(hd + 1) * HEAD_DIM]


def _dnprep_call(q, k, v, bg):
    bsz, ltot, _ = q.shape
    nch = ltot // CHUNK
    cps = PREP_CHUNKS
    assert nch % cps == 0
    tok = lambda w: pl.BlockSpec((None, cps * CHUNK, w), lambda b, n: (b, n, 0))
    dtok = lambda w: pl.BlockSpec((None, 2, cps * CHUNK, w), lambda b, n: (b, 0, n, 0))
    return pl.pallas_call(
        functools.partial(_dnprep_kernel, cps),
        grid=(bsz, nch // cps),
        in_specs=[tok(KEYW), tok(KEYW), tok(KEYW), tok(HEAD_DIM)],
        out_specs=[dtok(KEYW), dtok(KEYW), dtok(KEYW), dtok(KEYW), dtok(PACKW),
                   pl.BlockSpec((None, cps, N_DH, HEAD_DIM), lambda b, n: (b, n, 0, 0))],
        out_shape=[jax.ShapeDtypeStruct((bsz, 2, ltot, KEYW), BF16),
                   jax.ShapeDtypeStruct((bsz, 2, ltot, KEYW), BF16),
                   jax.ShapeDtypeStruct((bsz, 2, ltot, KEYW), BF16),
                   jax.ShapeDtypeStruct((bsz, 2, ltot, KEYW), BF16),
                   jax.ShapeDtypeStruct((bsz, 2, ltot, PACKW), BF16),
                   jax.ShapeDtypeStruct((bsz, nch, N_DH, HEAD_DIM), F32)],
        compiler_params=_cparams(2),
        name="dn_prep",
    )(q, k, v, bg)


def _dnscan_kernel(uf_ref, wf_ref, qgf_ref, kdf_ref, af_ref, df_ref,
                   ub_ref, wb_ref, qgb_ref, kdb_ref, ab_ref, db_ref,
                   of_ref, ob_ref, s_ref):
    @pl.when(pl.program_id(1) == 0)
    def _():
        s_ref[...] = jnp.zeros_like(s_ref)

    per_dir = ((uf_ref, wf_ref, qgf_ref, kdf_ref, af_ref, df_ref, of_ref),
               (ub_ref, wb_ref, qgb_ref, kdb_ref, ab_ref, db_ref, ob_ref))
    units = [(bi, dr, hd) for bi in range(uf_ref.shape[0]) for dr in range(2) for hd in range(HEADS)]
    states = [s_ref[bi, dr * HEADS + hd] for bi, dr, hd in units]
    n_sub = uf_ref.shape[1] // CHUNK
    for sub in range(n_sub):
        pos = (sub, n_sub - 1 - sub)
        rows = [slice(p * CHUNK, (p + 1) * CHUNK) for p in pos]
        first = []
        for (bi, dr, hd), s in zip(units, states):
            _, w_ref, qg_ref = per_dir[dr][:3]
            lo = hd * HEAD_DIM
            first.append(_dot(jnp.concatenate([w_ref[bi, rows[dr], lo:lo + HEAD_DIM],
                                               qg_ref[bi, rows[dr], lo:lo + HEAD_DIM]], axis=0), s.astype(BF16)))
        new_states = []
        for (bi, dr, hd), both, s in zip(units, first, states):
            u_ref, _, _, kd_ref, a_ref, d_ref, o_ref = per_dir[dr]
            unit = dr * HEADS + hd
            lo = hd * HEAD_DIM
            v_new_b = (u_ref[bi, rows[dr], lo:lo + HEAD_DIM].astype(F32) - both[:CHUNK]).astype(BF16)
            o_ref[bi, rows[dr], lo:lo + HEAD_DIM] = (
                both[CHUNK:] + _dot(a_ref[bi, rows[dr], hd * CHUNK:(hd + 1) * CHUNK], v_new_b)).astype(BF16)
            new_states.append(s * d_ref[bi, pos[dr], unit:unit + 1, :]
                              + _dot_tn(kd_ref[bi, rows[dr], lo:lo + HEAD_DIM], v_new_b))
        states = new_states
    for (bi, dr, hd), s in zip(units, states):
        s_ref[bi, dr * HEADS + hd] = s


def _dnscan_call(u, w, qg, kd, attn, adec, n_lat_chunks):
    bsz, _, ltot, _ = u.shape
    nch = ltot // CHUNK
    bb = SCAN_BATCH if bsz % SCAN_BATCH == 0 else 1
    cpg = SCAN_CHUNKS if (nch % SCAN_CHUNKS == 0 and n_lat_chunks % SCAN_CHUNKS == 0) else 1
    nblk = nch // cpg
    fwd = lambda n: (n + n_lat_chunks // cpg) % nblk
    bwd = lambda n: nblk - 1 - n

    def dspec(wd, dr, order):
        return pl.BlockSpec((bb, None, cpg * CHUNK, wd), lambda b, n: (b, dr, order(n), 0))

    def aspec(order):
        return pl.BlockSpec((bb, cpg, N_DH, HEAD_DIM), lambda b, n: (b, order(n), 0, 0))

    def ospec(order):
        return pl.BlockSpec((bb, cpg * CHUNK, KEYW), lambda b, n: (b, order(n), 0))

    in_specs = []
    for dr, order in ((0, fwd), (1, bwd)):
        in_specs += [dspec(KEYW, dr, order), dspec(KEYW, dr, order), dspec(KEYW, dr, order), dspec(KEYW, dr, order),
                     dspec(PACKW, dr, order), aspec(order)]
    return pl.pallas_call(
        _dnscan_kernel,
        grid=(bsz // bb, nblk),
        in_specs=in_specs,
        out_specs=[ospec(fwd), ospec(bwd)],
        out_shape=[jax.ShapeDtypeStruct((bsz, ltot, KEYW), BF16)] * 2,
        scratch_shapes=[pltpu.VMEM((bb, N_DH, HEAD_DIM, HEAD_DIM), F32)],
        compiler_params=_cparams(2),
        name="dn_scan",
    )(u, w, qg, kd, attn, adec, u, w, qg, kd, attn, adec)


def _dft_mats(n):
    idx = np.arange(n)
    ang = 2.0 * np.pi * ((idx[:, None] * idx[None, :]) % n) / n
    return np.cos(ang), np.sin(ang)


def _fft1_kernel(n2blk, u_ref, f1_ref, tc_ref, ts_ref, br_ref, bi_ref):
    l1 = u_ref.shape[1]
    xts = [pltpu.einshape("abc->bac", u_ref[s]) for s in range(u_ref.shape[0])]
    prods = [[_dot(f1_ref[...], xt[jj].astype(BF16)) for jj in range(n2blk)] for xt in xts]
    for s, per_n2 in enumerate(prods):
        for jj, a in enumerate(per_n2):
            ar, ai = a[:l1], a[l1:]
            tc = jnp.concatenate([tc_ref[jj]] * (KEYW // HEAD_DIM), axis=1)
            ts = jnp.concatenate([ts_ref[jj]] * (KEYW // HEAD_DIM), axis=1)
            br_ref[s, jj] = ar * tc + ai * ts
            bi_ref[s, jj] = ai * tc - ar * ts


def _fft2_kernel(k1blk, nlen, br_ref, bi_ref, f2_ref, wc_ref, y_ref):
    rows = k1blk * nlen
    n_rows = br_ref.shape[0]
    swap = lambda t: pltpu.einshape("abc->bac", t)
    brs = [swap(br_ref[s]).astype(BF16) for s in range(n_rows)]
    bis = [swap(bi_ref[s]).astype(BF16) for s in range(n_rows)]
    xs = [[_dot(f2_ref[...], jnp.concatenate([brs[s][jj], bis[s][jj]], axis=0)) for jj in range(k1blk)]
          for s in range(n_rows)]
    for s in range(n_rows):
        xr = jnp.concatenate([x[:nlen] for x in xs[s]], axis=0).astype(BF16)
        xi = jnp.concatenate([x[nlen:] for x in xs[s]], axis=0).astype(BF16)
        y = jnp.concatenate(
            [_dot(jnp.concatenate([xr[:, g * HEAD_DIM:(g + 1) * HEAD_DIM], xi[:, g * HEAD_DIM:(g + 1) * HEAD_DIM]],
                                  axis=1), wc_ref[...]) for g in range(KEYW // HEAD_DIM)], axis=1)
        y_ref[s] = swap(y.reshape(k1blk, nlen, KEYW))


def _dftctx_kernel(nlen, u_ref, f_ref, wc_ref, y_ref):
    x = _dot(f_ref[...], u_ref[...].astype(BF16))
    xr, xi = x[:nlen].astype(BF16), x[nlen:].astype(BF16)
    for g in range(KEYW // HEAD_DIM):
        gs = slice(g * HEAD_DIM, (g + 1) * HEAD_DIM)
        y_ref[:, gs] = _dot(jnp.concatenate([xr[:, gs], xi[:, gs]], axis=1), wc_ref[...])


def _chan_mat(total_len):
    cc, sc = _dft_mats(HEAD_DIM)
    return jnp.asarray(np.concatenate([cc, sc], axis=0) / math.sqrt(total_len * HEAD_DIM), BF16)


def _fourier_lat_call(fn_cat, l_lat):
    bsz, ltot, _ = fn_cat.shape
    l1 = l_lat // CHUNK
    n2blk = 8
    k1blk = 8 if l1 % 8 == 0 else l1
    c1, s1 = _dft_mats(l1)
    f1 = jnp.asarray(np.concatenate([c1, -s1], axis=0), BF16)
    n2 = np.arange(CHUNK)[:, None, None]
    k1 = np.arange(l1)[None, :, None]
    ang = 2.0 * np.pi * (n2 * k1) / l_lat * np.ones((1, 1, HEAD_DIM))
    tc, ts = jnp.asarray(np.cos(ang), F32), jnp.asarray(np.sin(ang), F32)
    c2, s2 = _dft_mats(CHUNK)
    f2 = jnp.asarray(np.block([[c2, s2], [-s2, c2]]), BF16)
    wc = _chan_mat(l_lat)

    u_view = fn_cat.reshape(bsz, ltot // CHUNK, CHUNK, KEYW)
    bb = ROW_BATCH if bsz % ROW_BATCH == 0 else 1
    mid_spec = pl.BlockSpec((bb, n2blk, l1, KEYW), lambda b, i: (b, i, 0, 0))
    br, bi = pl.pallas_call(
        functools.partial(_fft1_kernel, n2blk),
        grid=(bsz // bb, CHUNK // n2blk),
        in_specs=[pl.BlockSpec((bb, l1, n2blk, KEYW), lambda b, i: (b, 0, i, 0)),
                  pl.BlockSpec(f1.shape, lambda b, i: (0, 0)),
                  pl.BlockSpec((n2blk, l1, HEAD_DIM), lambda b, i: (i, 0, 0)),
                  pl.BlockSpec((n2blk, l1, HEAD_DIM), lambda b, i: (i, 0, 0))],
        out_specs=[mid_spec, mid_spec],
        out_shape=[jax.ShapeDtypeStruct((bsz, CHUNK, l1, KEYW), F32)] * 2,
        compiler_params=_cparams(2),
        name="fourier_stage1",
    )(u_view, f1, tc, ts)

    rspec = pl.BlockSpec((bb, CHUNK, k1blk, KEYW), lambda b, i: (b, 0, i, 0))
    y = pl.pallas_call(
        functools.partial(_fft2_kernel, k1blk, CHUNK),
        grid=(bsz // bb, l1 // k1blk),
        in_specs=[rspec, rspec,
                  pl.BlockSpec(f2.shape, lambda b, i: (0, 0)),
                  pl.BlockSpec(wc.shape, lambda b, i: (0, 0))],
        out_specs=rspec,
        out_shape=jax.ShapeDtypeStruct((bsz, CHUNK, l1, KEYW), F32),
        compiler_params=_cparams(2),
        name="fourier_stage2",
    )(br, bi, f2, wc)
    return y.reshape(bsz, l_lat, KEYW)


def _fourier_ctx_call(fn_cat, l_lat):
    bsz, ltot, _ = fn_cat.shape
    lc = ltot - l_lat
    c, s = _dft_mats(lc)
    f = jnp.asarray(np.concatenate([c, -s], axis=0), BF16)
    wc = _chan_mat(lc)
    return pl.pallas_call(
        functools.partial(_dftctx_kernel, lc),
        grid=(bsz,),
        in_specs=[pl.BlockSpec((None, lc, KEYW), lambda b: (b, l_lat // lc, 0)),
                  pl.BlockSpec(f.shape, lambda b: (0, 0)),
                  pl.BlockSpec(wc.shape, lambda b: (0, 0))],
        out_specs=pl.BlockSpec((None, lc, KEYW), lambda b: (b, 0, 0)),
        out_shape=jax.ShapeDtypeStruct((bsz, lc, KEYW), F32),
        compiler_params=_cparams(1),
        name="fourier_ctx",
    )(fn_cat, f, wc)


def _merge_kernel(nt_lat, x_ref, ctx_ref, mod_ref, of_ref, ob_ref, z_ref, sc_ref, fl_ref, fc_ref,
                  g_ref, wgate_ref, bgate_ref, og_ref, wdn_ref, wsc_ref, wfn_ref, wo_ref, o_ref):
    is_lat = pl.program_id(1) < nt_lat
    n_rows = x_ref.shape[0]
    xs = [jnp.where(is_lat, x_ref[s], ctx_ref[s]) for s in range(n_rows)]
    hs = [_modulated(xs[s], g_ref, mod_ref[s, 0:1, :], mod_ref[s, 1:2, :]).astype(BF16) for s in range(n_rows)]
    logits = [_dot(h, wgate_ref[...]) for h in hs]
    branch_in = []
    for s in range(n_rows):
        o = of_ref[s].astype(F32) + ob_ref[s].astype(F32)
        z = z_ref[s]
        dn_parts = []
        for hd in range(HEADS):
            sl = slice(hd * HEAD_DIM, (hd + 1) * HEAD_DIM)
            dn_parts.append(_rms(o[:, sl], HEAD_DIM) * og_ref[...] * _silu(z[:, sl]))
        fn = jnp.where(is_lat, fl_ref[s], fc_ref[s])
        branch_in.append((jnp.concatenate(dn_parts, axis=1).astype(BF16), sc_ref[s].astype(BF16), fn.astype(BF16)))
    ys = [(_dot(dn, wdn_ref[...]), _dot(sc, wsc_ref[...]), _dot(fn, wfn_ref[...])) for dn, sc, fn in branch_in]
    merged = []
    for s in range(n_rows):
        gates = jax.nn.sigmoid(logits[s] + bgate_ref[...])
        y_dn, y_sc, y_fn = ys[s]
        merged.append((gates[:, :D_MODEL] * y_dn + gates[:, D_MODEL:2 * D_MODEL] * y_sc
                       + gates[:, 2 * D_MODEL:] * y_fn).astype(BF16))
    mixes = [_dot(m, wo_ref[...]) for m in merged]
    for s in range(n_rows):
        o_ref[s] = xs[s] + mod_ref[s, 2:3, :] * mixes[s]


def _merge_call(x_src, ctx_src, ctx_blk, mod_sel, n1g, o_f, o_b, z, sc, fn_lat, fn_ctx, wgate, bgate, og, wdn, wsc,
                wfn, wo, nt_lat, nt_run):
    bsz, _, d = x_src.shape
    ltot = (nt_lat + 1) * TILE
    bb = ROW_BATCH if bsz % ROW_BATCH == 0 else 1
    tok = lambda w: pl.BlockSpec((bb, TILE, w), lambda b, j: (b, j, 0))
    consts = (n1g, wgate, bgate, og, wdn, wsc, wfn, wo)
    return pl.pallas_call(
        functools.partial(_merge_kernel, nt_lat),
        grid=(bsz // bb, nt_run),
        in_specs=_x_specs(bb, nt_lat, ctx_blk, d) + [_mod_spec(bb, nt_lat, d),
                  tok(KEYW), tok(KEYW), tok(KEYW), tok(KEYW),
                  pl.BlockSpec((bb, TILE, KEYW), lambda b, j: (b, jnp.minimum(j, nt_lat - 1), 0)),
                  pl.BlockSpec((bb, TILE, KEYW), lambda b, j: (b, 0, 0))] + [_const_spec(a) for a in consts],
        out_specs=tok(d),
        out_shape=jax.ShapeDtypeStruct((bsz, ltot, d), F32),
        compiler_params=_cparams(2),
        name="branch_merge",
    )(x_src, ctx_src, mod_sel, o_f, o_b, z, sc, fn_lat, fn_ctx, *consts)


def _ffn_kernel(final, x_ref, mod_ref, g_ref, win_ref, wout_ref, fg_ref, o_ref):
    n_rows = x_ref.shape[0]
    hs = [_modulated(x_ref[s], g_ref, mod_ref[s, 3:4, :], mod_ref[s, 4:5, :]).astype(BF16) for s in range(n_rows)]
    abs_ = [_dot(h, win_ref[...]) for h in hs]
    acts = [(_silu(ab[:, :D_FF]) * ab[:, D_FF:]).astype(BF16) for ab in abs_]
    downs = [_dot(act, wout_ref[...]) for act in acts]
    for s in range(n_rows):
        y = x_ref[s] + mod_ref[s, 5:6, :] * downs[s]
        if final:
            y = _rms(y, D_MODEL) * fg_ref[...]
        o_ref[s] = y


def _ffn_call(x_cat, mod_sel, n2g, win, wout, fg, nt_lat, nt_run, final):
    bsz, ltot, d = x_cat.shape
    out_len = nt_run * TILE if final else ltot
    bb = ROW_BATCH if bsz % ROW_BATCH == 0 else 1
    tok = pl.BlockSpec((bb, TILE, d), lambda b, j: (b, j, 0))
    consts = (n2g, win, wout, fg)
    return pl.pallas_call(
        functools.partial(_ffn_kernel, final),
        grid=(bsz // bb, nt_run),
        in_specs=[tok, _mod_spec(bb, nt_lat, d)] + [_const_spec(a) for a in consts],
        out_specs=tok,
        out_shape=jax.ShapeDtypeStruct((bsz, out_len, d), F32),
        compiler_params=_cparams(2),
        name="swiglu_final" if final else "swiglu",
    )(x_cat, mod_sel, *consts)


def kernel(x, c, ctx, c_ctx, w_mod, b_mod, norm1_g, w_in, dn_conv_w, dn_a_log, dn_dt_bias, dn_onorm_g, w_dn_out,
           sc_conv_w, w_sc_out, w_fn_out, w_gate, b_gate, w_o, norm2_g, w_ffn_in, w_ffn_out, final_g):
    bsz, l_lat, d = x.shape
    lc = ctx.shape[1]
    assert d == D_MODEL and lc == TILE and l_lat % TILE == 0 and w_ffn_in.shape[-1] == 2 * D_FF
    depth = w_mod.shape[0]
    nt_lat = l_lat // TILE
    nt_all = nt_lat + 1
    n_lat_chunks = l_lat // CHUNK

    rows = ((bsz + 1 + 7) // 8) * 8
    c_rows = jnp.zeros((rows, d), F32).at[:bsz].set(c).at[bsz].set(c_ctx)
    mod = _mod_call(c_rows, w_mod, b_mod).reshape(depth, rows, 6, d)
    mod_sel = jnp.stack([mod[:, :bsz], jnp.broadcast_to(mod[:, bsz:bsz + 1], (depth, bsz, 6, d))], axis=2)

    x_src, ctx_src, ctx_blk = x, ctx, 0
    row = lambda a: a.reshape(1, -1)
    lane_pad = lambda a: jnp.zeros((1, HEAD_DIM), F32).at[0, N_DH:2 * N_DH].set(a.reshape(-1))
    for layer in range(depth):
        last = layer == depth - 1
        wl = w_in[layer]
        qkv_w = 3 * KEYW
        o_z, o_ba, o_sc, o_fn = qkv_w, qkv_w + KEYW, qkv_w + KEYW + 2 * N_DH, qkv_w + KEYW + 2 * N_DH + 3 * KEYW
        wba = jnp.zeros((d, HEAD_DIM), F32).at[:, :2 * N_DH].set(wl[:, o_ba:o_sc])
        wparts = tuple(a.astype(BF16) for a in (wl[:, :o_z], wl[:, o_z:o_ba], wba, wl[:, o_sc:o_fn], wl[:, o_fn:]))
        q, k, v, z, bg, sc, fn = _inproj_call(
            x_src, ctx_src, ctx_blk, mod_sel[layer], row(norm1_g[layer]), wparts, dn_conv_w[layer], sc_conv_w[layer],
            lane_pad(dn_a_log[layer]), lane_pad(dn_dt_bias[layer]), nt_lat)
        u, w, qg, kd, attn, adec = _dnprep_call(q, k, v, bg)
        o_f, o_b = _dnscan_call(u, w, qg, kd, attn, adec, n_lat_chunks)
        fn_lat = _fourier_lat_call(fn, l_lat)
        nt_run = nt_lat if last else nt_all
        fn_ctx = jnp.zeros((bsz, lc, KEYW), F32) if last else _fourier_ctx_call(fn, l_lat)
        x_cat = _merge_call(
            x_src, ctx_src, ctx_blk, mod_sel[layer], row(norm1_g[layer]), o_f, o_b, z, sc, fn_lat, fn_ctx,
            w_gate[layer].astype(BF16), row(b_gate[layer]), row(dn_onorm_g[layer]),
            w_dn_out[layer].astype(BF16), w_sc_out[layer].astype(BF16), w_fn_out[layer].astype(BF16),
            w_o[layer].astype(BF16), nt_lat, nt_run)
        x_cat = _ffn_call(x_cat, mod_sel[layer], row(norm2_g[layer]), w_ffn_in[layer].astype(BF16),
                          w_ffn_out[layer].astype(BF16), row(final_g), nt_lat, nt_run, last)
        x_src, ctx_src, ctx_blk = x_cat, x_cat, nt_lat
    return x_cat
```

```python
import functools
import math

import numpy as np
import jax
import jax.numpy as jnp
from jax import lax
from jax.experimental import pallas as pl
from jax.experimental.pallas import tpu as pltpu

F32 = jnp.float32
BF16 = jnp.bfloat16
HIGHEST = lax.Precision.HIGHEST

EPS = 1e-6
D_MODEL = 1024
DEPTH = 2
HEADS = 4
HEAD_DIM = 128
KEYW = HEADS * HEAD_DIM
CHUNK = 64
D_FF = 2816
TILE = 256
N_DH = 2 * HEADS
PACKW = HEADS * CHUNK
PREP_CHUNKS = 12
PREP_GROUPS = (2, 5, 5)
SCAN_BATCH = 8
SCAN_CHUNKS = 4
ROW_BATCH = 2
VMEM_LIMIT = 56 * 1024 * 1024


def _cparams(n_axes):
    return pltpu.CompilerParams(dimension_semantics=("arbitrary",) * n_axes, vmem_limit_bytes=VMEM_LIMIT)


def _dot(a, b):
    return jnp.dot(a, b, preferred_element_type=F32)


def _dot_hi(a, b):
    return jnp.dot(a, b, preferred_element_type=F32, precision=HIGHEST)


def _dot_nt(a, b):
    return lax.dot_general(a, b, (((1,), (1,)), ((), ())), preferred_element_type=F32)


def _dot_tn(a, b):
    return lax.dot_general(a, b, (((0,), (0,)), ((), ())), preferred_element_type=F32)


def _silu(t):
    return t * jax.nn.sigmoid(t)


def _rms(t, axis_size):
    return t * lax.rsqrt(jnp.sum(t * t, axis=-1, keepdims=True) * (1.0 / axis_size) + EPS)


def _mod_kernel(c_ref, w_ref, b_ref, o_ref):
    o_ref[...] = _dot_hi(_silu(c_ref[...]), w_ref[...]) + b_ref[...]


def _mod_call(c_rows, w_mod, b_mod):
    depth, d, n = w_mod.shape
    rows = c_rows.shape[0]
    tn = 1536
    return pl.pallas_call(
        _mod_kernel,
        grid=(depth, n // tn),
        in_specs=[
            pl.BlockSpec((rows, d), lambda l, j: (0, 0)),
            pl.BlockSpec((None, d, tn), lambda l, j: (l, 0, j)),
            pl.BlockSpec((None, 1, tn), lambda l, j: (l, 0, j)),
        ],
        out_specs=pl.BlockSpec((None, rows, tn), lambda l, j: (l, 0, j)),
        out_shape=jax.ShapeDtypeStruct((depth, rows, n), F32),
        compiler_params=_cparams(2),
        name="mod_vectors",
    )(c_rows, w_mod, b_mod.reshape(depth, 1, n))


def _row_masks(is_lat):
    pos = lax.broadcasted_iota(jnp.int32, (TILE, 1), 0)
    row_mask = jnp.where(is_lat, CHUNK - 1, TILE - 1)
    in_row = pos & row_mask
    return jnp.where(in_row == 0, 0.0, 1.0).astype(F32), jnp.where(in_row == row_mask, 0.0, 1.0).astype(F32)


def _mask_rows(t, mask, first):
    pieces = []
    for g in range(TILE // CHUNK):
        lo, hi = g * CHUNK, (g + 1) * CHUNK
        if first:
            pieces += [t[lo:lo + 8] * mask[lo:lo + 8], t[lo + 8:hi]]
        else:
            pieces += [t[lo:hi - 8], t[hi - 8:hi] * mask[hi - 8:hi]]
    return jnp.concatenate(pieces, axis=0)


def _conv3(t, w_ref, m_prev, m_next):
    prev = _mask_rows(pltpu.roll(t, 1, 0), m_prev, True)
    nxt = _mask_rows(pltpu.roll(t, TILE - 1, 0), m_next, False)
    return prev * w_ref[0:1, :] + t * w_ref[1:2, :] + nxt * w_ref[2:3, :]


def _modulated(x, g_ref, shift, scale):
    return _rms(x, D_MODEL) * (g_ref[...] * (1.0 + scale)) + shift


def _inproj_kernel(nt_lat, x_ref, ctx_ref, mod_ref, g_ref, wqkv_ref, wz_ref, wba_ref, wsc_ref, wfn_ref,
                   dnconv_ref, scconv_ref, alog_ref, dtb_ref,
                   q_ref, k_ref, v_ref, z_ref, bg_ref, sc_ref, fn_ref):
    is_lat = pl.program_id(1) < nt_lat
    m_prev, m_next = _row_masks(is_lat)
    n_rows = x_ref.shape[0]
    hs = []
    for s in range(n_rows):
        x = jnp.where(is_lat, x_ref[s], ctx_ref[s])
        hs.append(_modulated(x, g_ref, mod_ref[s, 0:1, :], mod_ref[s, 1:2, :]).astype(BF16))
    projs = [tuple(_dot(h, w_ref[...]) for w_ref in (wqkv_ref, wz_ref, wba_ref, wsc_ref, wfn_ref)) for h in hs]
    for s in range(n_rows):
        p_qkv, p_z, ba, scp, p_fn = projs[s]
        qkv = _silu(_conv3(p_qkv, dnconv_ref, m_prev, m_next))
        for hd in range(HEADS):
            lo = hd * HEAD_DIM
            qh = qkv[:, lo:lo + HEAD_DIM]
            kh = qkv[:, KEYW + lo:KEYW + lo + HEAD_DIM]
            q_ref[s, :, lo:lo + HEAD_DIM] = (qh * lax.rsqrt(jnp.sum(qh * qh, axis=-1, keepdims=True) + EPS)
                                             * (HEAD_DIM ** -0.5))
            k_ref[s, :, lo:lo + HEAD_DIM] = kh * lax.rsqrt(jnp.sum(kh * kh, axis=-1, keepdims=True) + EPS)
        v_ref[s] = qkv[:, 2 * KEYW:]
        z_ref[s] = p_z

        sp_in = ba + dtb_ref[...]
        softplus = jnp.maximum(sp_in, 0.0) + jnp.log1p(jnp.exp(-jnp.abs(sp_in)))
        lane = lax.broadcasted_iota(jnp.int32, ba.shape, 1)
        g = -jnp.exp(alog_ref[...]) * softplus
        in_chunk = lax.broadcasted_iota(jnp.int32, (TILE, 1), 0) & (CHUNK - 1)
        pre = g
        step = 1
        while step < CHUNK:
            pre = pre + jnp.where(in_chunk >= step, pltpu.roll(pre, step, 0), 0.0)
            step *= 2
        total = jnp.concatenate(
            [jnp.broadcast_to(pre[c * CHUNK + CHUNK - 1:(c + 1) * CHUNK, :], (CHUNK, HEAD_DIM))
             for c in range(TILE // CHUNK)], axis=0)
        suf = (total - pre) + g
        gc = jnp.where(lane < N_DH + HEADS, pre, suf)
        bg_ref[s] = jnp.where(lane < N_DH, jax.nn.sigmoid(ba), gc)

        sc_ref[s] = scp[:, :KEYW] * _conv3(scp[:, KEYW:2 * KEYW] * scp[:, 2 * KEYW:], scconv_ref, m_prev, m_next)
        fn_ref[s] = p_fn.astype(BF16)


def _const_spec(a):
    return pl.BlockSpec(a.shape, lambda b, j: (0,) * a.ndim, pipeline_mode=pl.Buffered(1))


def _mod_spec(bb, nt_lat, d):
    return pl.BlockSpec((bb, None, 6, d), lambda b, j: (b, jnp.where(j < nt_lat, 0, 1), 0, 0))


def _x_specs(bb, nt_lat, ctx_blk, d):
    return [pl.BlockSpec((bb, TILE, d), lambda b, j: (b, jnp.minimum(j, nt_lat - 1), 0)),
            pl.BlockSpec((bb, TILE, d), lambda b, j: (b, ctx_blk, 0))]


def _inproj_call(x_src, ctx_src, ctx_blk, mod_sel, n1g, wparts, dnconv, scconv, alog_row, dtb_row, nt_lat):
    bsz, _, d = x_src.shape
    nt = nt_lat + 1
    ltot = nt * TILE
    bb = ROW_BATCH if bsz % ROW_BATCH == 0 else 1
    wqkv, wz, wba, wsc, wfn = wparts
    tok = lambda w: pl.BlockSpec((bb, TILE, w), lambda b, j: (b, j, 0))
    out_w = (KEYW, KEYW, KEYW, KEYW, HEAD_DIM, KEYW, KEYW)
    consts = (n1g, wqkv, wz, wba, wsc, wfn, dnconv, scconv, alog_row, dtb_row)
    return pl.pallas_call(
        functools.partial(_inproj_kernel, nt_lat),
        grid=(bsz // bb, nt),
        in_specs=_x_specs(bb, nt_lat, ctx_blk, d) + [_mod_spec(bb, nt_lat, d)] + [_const_spec(a) for a in consts],
        out_specs=[tok(w) for w in out_w],
        out_shape=[jax.ShapeDtypeStruct((bsz, ltot, w), F32) for w in out_w[:-1]]
        + [jax.ShapeDtypeStruct((bsz, ltot, out_w[-1]), BF16)],
        compiler_params=_cparams(2),
        name="in_proj",
    )(x_src, ctx_src, mod_sel, *consts)


def _split2(t):
    hi = t.astype(BF16)
    return hi, (t - hi.astype(F32)).astype(BF16)


def _mm3(lhs, rhs):
    m = lhs.shape[0]
    lh, ll = _split2(lhs)
    rh, rl = _split2(rhs)
    top = _dot(jnp.concatenate([lh, ll], axis=0), rh)
    return top[:m] + top[m:] + _dot(lh, rl)


def _dnprep_kernel(cps, q_ref, k_ref, v_ref, bg_ref, u_ref, w_ref, qg_ref, kd_ref, attn_ref, adec_ref):
    ri = lax.broadcasted_iota(jnp.int32, (CHUNK, PACKW), 0)
    cj = lax.broadcasted_iota(jnp.int32, (CHUNK, PACKW), 1) & (CHUNK - 1)
    eye_p = jnp.where(ri == cj, 1.0, 0.0).astype(F32)
    incl_m = ((ri >= cj), (ri <= cj))
    strict_m = ((ri > cj), (ri < cj))

    def half_masked(t, hd):
        lane = lax.broadcasted_iota(jnp.int32, t.shape, 1)
        keep = (lane < CHUNK) if hd % 2 == 0 else (lane >= CHUNK)
        return jnp.where(keep, t, jnp.zeros((), BF16))

    def blockdiag(t):
        z = jnp.zeros((CHUNK, HEAD_DIM), BF16)
        blocks = []
        for hd in range(HEADS):
            piece = half_masked(t[:, (hd // 2) * HEAD_DIM:(hd // 2 + 1) * HEAD_DIM], hd)
            blocks.append(jnp.concatenate([piece, z] if hd < 2 else [z, piece], axis=1))
        return jnp.concatenate(blocks, axis=0)

    def mm3_all(lhs_list, rhs_list):
        lsp = [_split2(l) for l in lhs_list]
        rsp = [tuple(blockdiag(p) for p in _split2(r)) for r in rhs_list]
        tops = [_dot(jnp.concatenate([lh, ll], axis=0), rh) for (lh, ll), (rh, _) in zip(lsp, rsp)]
        lows = [_dot(lh, rl) for (lh, _), (_, rl) in zip(lsp, rsp)]
        return [t[:l.shape[0]] + t[l.shape[0]:] + lo for t, lo, l in zip(tops, lows, lhs_list)]

    sizes = PREP_GROUPS if sum(PREP_GROUPS) == cps else (cps,)
    for first, size in zip(np.cumsum((0,) + sizes[:-1]), sizes):
        _dnprep_group(int(first), size, q_ref, k_ref, v_ref, bg_ref,
                      u_ref, w_ref, qg_ref, kd_ref, attn_ref, adec_ref,
                      eye_p, incl_m, strict_m, half_masked, blockdiag, mm3_all)


def _dnprep_group(first, cps, q_ref, k_ref, v_ref, bg_ref, u_ref, w_ref, qg_ref, kd_ref, attn_ref, adec_ref,
                  eye_p, incl_m, strict_m, half_masked, blockdiag, mm3_all):
    chunk_rows = [slice((first + ci) * CHUNK, (first + ci + 1) * CHUNK) for ci in range(cps)]
    qs = [q_ref[r, :] for r in chunk_rows]
    ks = [k_ref[r, :] for r in chunk_rows]
    vs = [v_ref[r, :] for r in chunk_rows]

    lo_half = lax.broadcasted_iota(jnp.int32, (CHUNK, HEAD_DIM), 1) < CHUNK
    expanded = {}
    for ci, r in enumerate(chunk_rows):
        bg = bg_ref[r, :]
        for dr in range(2):
            cols = [(jnp.broadcast_to(bg[:, N_DH + dr * HEADS + hd:N_DH + dr * HEADS + hd + 1], (CHUNK, HEAD_DIM)),
                     jnp.broadcast_to(bg[:, dr * HEADS + hd:dr * HEADS + hd + 1], (CHUNK, HEAD_DIM)))
                    for hd in range(HEADS)]
            gcol = jnp.concatenate([c[0] for c in cols], axis=1)
            beta = jnp.concatenate([c[1] for c in cols], axis=1)
            gcol_p = jnp.concatenate([jnp.where(lo_half, cols[0][0], cols[1][0]),
                                      jnp.where(lo_half, cols[2][0], cols[3][0])], axis=1)
            expanded[ci, dr] = (gcol, beta, gcol_p)

    def parts(ci, dr):
        return expanded[ci, dr]

    m1s = []
    for ci in range(cps):
        k = ks[ci]
        k_t = jnp.concatenate([k, k], axis=0).T.astype(BF16)
        zk = jnp.zeros((HEAD_DIM, HEAD_DIM), BF16)
        bd_rows = []
        for hd in range(HEADS):
            piece = half_masked(k_t[hd * HEAD_DIM:(hd + 1) * HEAD_DIM, :], hd)
            bd_rows.append(jnp.concatenate([piece, zk] if hd < 2 else [zk, piece], axis=1))
        bd_k = jnp.concatenate(bd_rows, axis=0)
        kbs = [k * parts(ci, dr)[1] for dr in range(2)]
        m1s.append(_dot(jnp.concatenate(kbs + [qs[ci]], axis=0).astype(BF16), bd_k))

    chains = [(ci, dr) for ci in range(cps) for dr in range(2)]
    decays, neg_as = [], []
    for ci, dr in chains:
        gcol_p = parts(ci, dr)[2]
        grow_p = jnp.sum(gcol_p * eye_p, axis=0, keepdims=True)
        decay = jnp.exp(jnp.where(incl_m[dr], gcol_p - grow_p, -jnp.inf))
        decays.append(decay)
        neg_as.append(jnp.where(strict_m[dr], -(m1s[ci][dr * CHUNK:(dr + 1) * CHUNK] * decay), 0.0))

    p_accs = [eye_p + a for a in neg_as]
    q_pows = mm3_all(neg_as, neg_as)
    for _ in range(4):
        boths = mm3_all([jnp.concatenate([qp, pa], axis=0) for qp, pa in zip(q_pows, p_accs)], q_pows)
        q_pows = [b[:CHUNK] for b in boths]
        p_accs = [pa + b[CHUNK:] for pa, b in zip(p_accs, boths)]
    lasts = mm3_all(p_accs, q_pows)
    p_accs = [pa + la for pa, la in zip(p_accs, lasts)]

    uws, egs = [], []
    for (ci, dr), t_inv in zip(chains, p_accs):
        gcol, beta, _ = parts(ci, dr)
        eg = jnp.exp(gcol)
        egs.append(eg)
        vb = (vs[ci] * beta).astype(BF16)
        kbe = (ks[ci] * (beta * eg)).astype(BF16)
        rhs = jnp.concatenate(
            [jnp.concatenate([vb[:, hd * HEAD_DIM:(hd + 1) * HEAD_DIM], kbe[:, hd * HEAD_DIM:(hd + 1) * HEAD_DIM]], axis=1)
             for hd in range(HEADS)], axis=0)
        res = _dot(blockdiag(t_inv.astype(BF16)), rhs)
        uws.append(jnp.concatenate(
            [res[hd * CHUNK:(hd + 1) * CHUNK, :HEAD_DIM] for hd in range(HEADS)]
            + [res[hd * CHUNK:(hd + 1) * CHUNK, HEAD_DIM:] for hd in range(HEADS)], axis=1))

    for (ci, dr), uw, eg, decay in zip(chains, uws, egs, decays):
        rows = chunk_rows[ci]
        gcol = parts(ci, dr)[0]
        g_last = gcol[CHUNK - 1:CHUNK, :] if dr == 0 else gcol[0:1, :]
        u_ref[dr, rows, :] = uw[:, :KEYW].astype(BF16)
        w_ref[dr, rows, :] = uw[:, KEYW:].astype(BF16)
        qg_ref[dr, rows, :] = (qs[ci] * eg).astype(BF16)
        kd_ref[dr, rows, :] = (ks[ci] * jnp.exp(g_last - gcol)).astype(BF16)
        attn_ref[dr, rows, :] = (m1s[ci][2 * CHUNK:] * decay).astype(BF16)
        a_dec = jnp.exp(g_last)
        for hd in range(HEADS):
            unit = dr * HEADS + hd
            adec_ref[first + ci, unit:unit + 1, :] = a_dec[:, hd * HEAD_DIM:(hd + 1) * HEAD_DIM]


def _dnprep_call(q, k, v, bg):
    bsz, ltot, _ = q.shape
    nch = ltot // CHUNK
    cps = PREP_CHUNKS
    assert nch % cps == 0
    tok = lambda w: pl.BlockSpec((None, cps * CHUNK, w), lambda b, n: (b, n, 0))
    dtok = lambda w: pl.BlockSpec((None, 2, cps * CHUNK, w), lambda b, n: (b, 0, n, 0))
    return pl.pallas_call(
        functools.partial(_dnprep_kernel, cps),
        grid=(bsz, nch // cps),
        in_specs=[tok(KEYW), tok(KEYW), tok(KEYW), tok(HEAD_DIM)],
        out_specs=[dtok(KEYW), dtok(KEYW), dtok(KEYW), dtok(KEYW), dtok(PACKW),
                   pl.BlockSpec((None, cps, N_DH, HEAD_DIM), lambda b, n: (b, n, 0, 0))],
        out_shape=[jax.ShapeDtypeStruct((bsz, 2, ltot, KEYW), BF16),
                   jax.ShapeDtypeStruct((bsz, 2, ltot, KEYW), BF16),
                   jax.ShapeDtypeStruct((bsz, 2, ltot, KEYW), BF16),
                   jax.ShapeDtypeStruct((bsz, 2, ltot, KEYW), BF16),
                   jax.ShapeDtypeStruct((bsz, 2, ltot, PACKW), BF16),
                   jax.ShapeDtypeStruct((bsz, nch, N_DH, HEAD_DIM), F32)],
        compiler_params=_cparams(2),
        name="dn_prep",
    )(q, k, v, bg)


def _dnscan_kernel(uf_ref, wf_ref, qgf_ref, kdf_ref, af_ref, df_ref,
                   ub_ref, wb_ref, qgb_ref, kdb_ref, ab_ref, db_ref,
                   of_ref, ob_ref, s_ref):
    @pl.when(pl.program_id(1) == 0)
    def _():
        s_ref[...] = jnp.zeros_like(s_ref)

    per_dir = ((uf_ref, wf_ref, qgf_ref, kdf_ref, af_ref, df_ref, of_ref),
               (ub_ref, wb_ref, qgb_ref, kdb_ref, ab_ref, db_ref, ob_ref))
    units = [(bi, dr, hd) for bi in range(uf_ref.shape[0]) for dr in range(2) for hd in range(HEADS)]
    states = [s_ref[bi, dr * HEADS + hd] for bi, dr, hd in units]
    n_sub = uf_ref.shape[1] // CHUNK
    for sub in range(n_sub):
        pos = (sub, n_sub - 1 - sub)
        rows = [slice(p * CHUNK, (p + 1) * CHUNK) for p in pos]
        first = []
        for (bi, dr, hd), s in zip(units, states):
            _, w_ref, qg_ref = per_dir[dr][:3]
            lo = hd * HEAD_DIM
            first.append(_dot(jnp.concatenate([w_ref[bi, rows[dr], lo:lo + HEAD_DIM],
                                               qg_ref[bi, rows[dr], lo:lo + HEAD_DIM]], axis=0), s.astype(BF16)))
        new_states = []
        for (bi, dr, hd), both, s in zip(units, first, states):
            u_ref, _, _, kd_ref, a_ref, d_ref, o_ref = per_dir[dr]
            unit = dr * HEADS + hd
            lo = hd * HEAD_DIM
            v_new_b = (u_ref[bi, rows[dr], lo:lo + HEAD_DIM].astype(F32) - both[:CHUNK]).astype(BF16)
            o_ref[bi, rows[dr], lo:lo + HEAD_DIM] = (
                both[CHUNK:] + _dot(a_ref[bi, rows[dr], hd * CHUNK:(hd + 1) * CHUNK], v_new_b)).astype(BF16)
            new_states.append(s * d_ref[bi, pos[dr], unit:unit + 1, :]
                              + _dot_tn(kd_ref[bi, rows[dr], lo:lo + HEAD_DIM], v_new_b))
        states = new_states
    for (bi, dr, hd), s in zip(units, states):
        s_ref[bi, dr * HEADS + hd] = s


def _dnscan_call(u, w, qg, kd, attn, adec, n_lat_chunks):
    bsz, _, ltot, _ = u.shape
    nch = ltot // CHUNK
    bb = SCAN_BATCH if bsz % SCAN_BATCH == 0 else 1
    cpg = SCAN_CHUNKS if (nch % SCAN_CHUNKS == 0 and n_lat_chunks % SCAN_CHUNKS == 0) else 1
    nblk = nch // cpg
    fwd = lambda n: (n + n_lat_chunks // cpg) % nblk
    bwd = lambda n: nblk - 1 - n

    def dspec(wd, dr, order):
        return pl.BlockSpec((bb, None, cpg * CHUNK, wd), lambda b, n: (b, dr, order(n), 0))

    def aspec(order):
        return pl.BlockSpec((bb, cpg, N_DH, HEAD_DIM), lambda b, n: (b, order(n), 0, 0))

    def ospec(order):
        return pl.BlockSpec((bb, cpg * CHUNK, KEYW), lambda b, n: (b, order(n), 0))

    in_specs = []
    for dr, order in ((0, fwd), (1, bwd)):
        in_specs += [dspec(KEYW, dr, order), dspec(KEYW, dr, order), dspec(KEYW, dr, order), dspec(KEYW, dr, order),
                     dspec(PACKW, dr, order), aspec(order)]
    return pl.pallas_call(
        _dnscan_kernel,
        grid=(bsz // bb, nblk),
        in_specs=in_specs,
        out_specs=[ospec(fwd), ospec(bwd)],
        out_shape=[jax.ShapeDtypeStruct((bsz, ltot, KEYW), BF16)] * 2,
        scratch_shapes=[pltpu.VMEM((bb, N_DH, HEAD_DIM, HEAD_DIM), F32)],
        compiler_params=_cparams(2),
        name="dn_scan",
    )(u, w, qg, kd, attn, adec, u, w, qg, kd, attn, adec)


def _dft_mats(n):
    idx = np.arange(n)
    ang = 2.0 * np.pi * ((idx[:, None] * idx[None, :]) % n) / n
    return np.cos(ang), np.sin(ang)


def _fft1_kernel(n2blk, u_ref, f1_ref, tc_ref, ts_ref, br_ref, bi_ref):
    l1 = u_ref.shape[1]
    xts = [pltpu.einshape("abc->bac", u_ref[s].astype(F32)).astype(BF16)
           for s in range(u_ref.shape[0])]
    prods = [[_dot(f1_ref[...], xt[jj]) for jj in range(n2blk)] for xt in xts]
    for s, per_n2 in enumerate(prods):
        for jj, a in enumerate(per_n2):
            ar, ai = a[:l1], a[l1:]
            tc = jnp.concatenate([tc_ref[jj]] * (KEYW // HEAD_DIM), axis=1)
            ts = jnp.concatenate([ts_ref[jj]] * (KEYW // HEAD_DIM), axis=1)
            br_ref[s, jj] = (ar * tc + ai * ts).astype(BF16)
            bi_ref[s, jj] = (ai * tc - ar * ts).astype(BF16)


def _fft2_kernel(k1blk, nlen, br_ref, bi_ref, f2_ref, wc_ref, y_ref):
    rows = k1blk * nlen
    n_rows = br_ref.shape[0]
    swap = lambda t: pltpu.einshape("abc->bac", t)
    brs = [swap(br_ref[s].astype(F32)).astype(BF16) for s in range(n_rows)]
    bis = [swap(bi_ref[s].astype(F32)).astype(BF16) for s in range(n_rows)]
    xs = [[_dot(f2_ref[...], jnp.concatenate([brs[s][jj], bis[s][jj]], axis=0)) for jj in range(k1blk)]
          for s in range(n_rows)]
    for s in range(n_rows):
        xr = jnp.concatenate([x[:nlen] for x in xs[s]], axis=0).astype(BF16)
        xi = jnp.concatenate([x[nlen:] for x in xs[s]], axis=0).astype(BF16)
        y = jnp.concatenate(
            [_dot(jnp.concatenate([xr[:, g * HEAD_DIM:(g + 1) * HEAD_DIM], xi[:, g * HEAD_DIM:(g + 1) * HEAD_DIM]],
                                  axis=1), wc_ref[...]) for g in range(KEYW // HEAD_DIM)], axis=1)
        y_ref[s] = swap(y.reshape(k1blk, nlen, KEYW)).astype(BF16)


def _dftctx_kernel(nlen, u_ref, f_ref, wc_ref, y_ref):
    x = _dot(f_ref[...], u_ref[...])
    xr, xi = x[:nlen].astype(BF16), x[nlen:].astype(BF16)
    for g in range(KEYW // HEAD_DIM):
        gs = slice(g * HEAD_DIM, (g + 1) * HEAD_DIM)
        y_ref[:, gs] = _dot(jnp.concatenate([xr[:, gs], xi[:, gs]], axis=1), wc_ref[...]).astype(BF16)


def _chan_mat(total_len):
    cc, sc = _dft_mats(HEAD_DIM)
    return jnp.asarray(np.concatenate([cc, sc], axis=0) / math.sqrt(total_len * HEAD_DIM), BF16)


def _fourier_lat_call(fn_cat, l_lat):
    bsz, ltot, _ = fn_cat.shape
    l1 = l_lat // CHUNK
    n2blk = 16
    k1blk = 16 if l1 % 16 == 0 else l1
    c1, s1 = _dft_mats(l1)
    f1 = jnp.asarray(np.concatenate([c1, -s1], axis=0), BF16)
    n2 = np.arange(CHUNK)[:, None, None]
    k1 = np.arange(l1)[None, :, None]
    ang = 2.0 * np.pi * (n2 * k1) / l_lat * np.ones((1, 1, HEAD_DIM))
    tc, ts = jnp.asarray(np.cos(ang), F32), jnp.asarray(np.sin(ang), F32)
    c2, s2 = _dft_mats(CHUNK)
    f2 = jnp.asarray(np.block([[c2, s2], [-s2, c2]]), BF16)
    wc = _chan_mat(l_lat)

    u_view = fn_cat.reshape(bsz, ltot // CHUNK, CHUNK, KEYW)
    bb = ROW_BATCH if bsz % ROW_BATCH == 0 else 1
    mid_spec = pl.BlockSpec((bb, n2blk, l1, KEYW), lambda b, i: (b, i, 0, 0))
    br, bi = pl.pallas_call(
        functools.partial(_fft1_kernel, n2blk),
        grid=(bsz // bb, CHUNK // n2blk),
        in_specs=[pl.BlockSpec((bb, l1, n2blk, KEYW), lambda b, i: (b, 0, i, 0)),
                  pl.BlockSpec(f1.shape, lambda b, i: (0, 0)),
                  pl.BlockSpec((n2blk, l1, HEAD_DIM), lambda b, i: (i, 0, 0)),
                  pl.BlockSpec((n2blk, l1, HEAD_DIM), lambda b, i: (i, 0, 0))],
        out_specs=[mid_spec, mid_spec],
        out_shape=[jax.ShapeDtypeStruct((bsz, CHUNK, l1, KEYW), BF16)] * 2,
        compiler_params=_cparams(2),
        name="fourier_stage1",
    )(u_view, f1, tc, ts)

    rspec = pl.BlockSpec((bb, CHUNK, k1blk, KEYW), lambda b, i: (b, 0, i, 0))
    y = pl.pallas_call(
        functools.partial(_fft2_kernel, k1blk, CHUNK),
        grid=(bsz // bb, l1 // k1blk),
        in_specs=[rspec, rspec,
                  pl.BlockSpec(f2.shape, lambda b, i: (0, 0)),
                  pl.BlockSpec(wc.shape, lambda b, i: (0, 0))],
        out_specs=rspec,
        out_shape=jax.ShapeDtypeStruct((bsz, CHUNK, l1, KEYW), BF16),
        compiler_params=_cparams(2),
        name="fourier_stage2",
    )(br, bi, f2, wc)
    return y.reshape(bsz, l_lat, KEYW)


def _fourier_ctx_call(fn_cat, l_lat):
    bsz, ltot, _ = fn_cat.shape
    lc = ltot - l_lat
    c, s = _dft_mats(lc)
    f = jnp.asarray(np.concatenate([c, -s], axis=0), BF16)
    wc = _chan_mat(lc)
    return pl.pallas_call(
        functools.partial(_dftctx_kernel, lc),
        grid=(bsz,),
        in_specs=[pl.BlockSpec((None, lc, KEYW), lambda b: (b, l_lat // lc, 0)),
                  pl.BlockSpec(f.shape, lambda b: (0, 0)),
                  pl.BlockSpec(wc.shape, lambda b: (0, 0))],
        out_specs=pl.BlockSpec((None, lc, KEYW), lambda b: (b, 0, 0)),
        out_shape=jax.ShapeDtypeStruct((bsz, lc, KEYW), BF16),
        compiler_params=_cparams(1),
        name="fourier_ctx",
    )(fn_cat, f, wc)


def _merge_kernel(nt_lat, x_ref, ctx_ref, mod_ref, of_ref, ob_ref, z_ref, sc_ref, fl_ref, fc_ref,
                  g_ref, wgate_ref, bgate_ref, og_ref, wdn_ref, wsc_ref, wfn_ref, wo_ref, o_ref):
    is_lat = pl.program_id(1) < nt_lat
    n_rows = x_ref.shape[0]
    xs = [jnp.where(is_lat, x_ref[s], ctx_ref[s]) for s in range(n_rows)]
    hs = [_modulated(xs[s], g_ref, mod_ref[s, 0:1, :], mod_ref[s, 1:2, :]).astype(BF16) for s in range(n_rows)]
    logits = [_dot(h, wgate_ref[...]) for h in hs]
    branch_in = []
    for s in range(n_rows):
        o = of_ref[s].astype(F32) + ob_ref[s].astype(F32)
        z = z_ref[s]
        dn_parts = []
        for hd in range(HEADS):
            sl = slice(hd * HEAD_DIM, (hd + 1) * HEAD_DIM)
            dn_parts.append(_rms(o[:, sl], HEAD_DIM) * og_ref[...] * _silu(z[:, sl]))
        fn = jnp.where(is_lat, fl_ref[s], fc_ref[s])
        branch_in.append((jnp.concatenate(dn_parts, axis=1).astype(BF16), sc_ref[s].astype(BF16), fn))
    ys = [(_dot(dn, wdn_ref[...]), _dot(sc, wsc_ref[...]), _dot(fn, wfn_ref[...])) for dn, sc, fn in branch_in]
    merged = []
    for s in range(n_rows):
        gates = jax.nn.sigmoid(logits[s] + bgate_ref[...])
        y_dn, y_sc, y_fn = ys[s]
        merged.append((gates[:, :D_MODEL] * y_dn + gates[:, D_MODEL:2 * D_MODEL] * y_sc
                       + gates[:, 2 * D_MODEL:] * y_fn).astype(BF16))
    mixes = [_dot(m, wo_ref[...]) for m in merged]
    for s in range(n_rows):
        o_ref[s] = xs[s] + mod_ref[s, 2:3, :] * mixes[s]


def _merge_call(x_src, ctx_src, ctx_blk, mod_sel, n1g, o_f, o_b, z, sc, fn_lat, fn_ctx, wgate, bgate, og, wdn, wsc,
                wfn, wo, nt_lat, nt_run):
    bsz, _, d = x_src.shape
    ltot = (nt_lat + 1) * TILE
    bb = ROW_BATCH if bsz % ROW_BATCH == 0 else 1
    tok = lambda w: pl.BlockSpec((bb, TILE, w), lambda b, j: (b, j, 0))
    consts = (n1g, wgate, bgate, og, wdn, wsc, wfn, wo)
    return pl.pallas_call(
        functools.partial(_merge_kernel, nt_lat),
        grid=(bsz // bb, nt_run),
        in_specs=_x_specs(bb, nt_lat, ctx_blk, d) + [_mod_spec(bb, nt_lat, d),
                  tok(KEYW), tok(KEYW), tok(KEYW), tok(KEYW),
                  pl.BlockSpec((bb, TILE, KEYW), lambda b, j: (b, jnp.minimum(j, nt_lat - 1), 0)),
                  pl.BlockSpec((bb, TILE, KEYW), lambda b, j: (b, 0, 0))] + [_const_spec(a) for a in consts],
        out_specs=tok(d),
        out_shape=jax.ShapeDtypeStruct((bsz, ltot, d), F32),
        compiler_params=_cparams(2),
        name="branch_merge",
    )(x_src, ctx_src, mod_sel, o_f, o_b, z, sc, fn_lat, fn_ctx, *consts)


def _ffn_kernel(final, x_ref, mod_ref, g_ref, win_ref, wout_ref, fg_ref, o_ref):
    n_rows = x_ref.shape[0]
    hs = [_modulated(x_ref[s], g_ref, mod_ref[s, 3:4, :], mod_ref[s, 4:5, :]).astype(BF16) for s in range(n_rows)]
    abs_ = [_dot(h, win_ref[...]) for h in hs]
    acts = [(_silu(ab[:, :D_FF]) * ab[:, D_FF:]).astype(BF16) for ab in abs_]
    downs = [_dot(act, wout_ref[...]) for act in acts]
    for s in range(n_rows):
        y = x_ref[s] + mod_ref[s, 5:6, :] * downs[s]
        if final:
            y = _rms(y, D_MODEL) * fg_ref[...]
        o_ref[s] = y


def _ffn_call(x_cat, mod_sel, n2g, win, wout, fg, nt_lat, nt_run, final):
    bsz, ltot, d = x_cat.shape
    out_len = nt_run * TILE if final else ltot
    bb = ROW_BATCH if bsz % ROW_BATCH == 0 else 1
    tok = pl.BlockSpec((bb, TILE, d), lambda b, j: (b, j, 0))
    consts = (n2g, win, wout, fg)
    return pl.pallas_call(
        functools.partial(_ffn_kernel, final),
        grid=(bsz // bb, nt_run),
        in_specs=[tok, _mod_spec(bb, nt_lat, d)] + [_const_spec(a) for a in consts],
        out_specs=tok,
        out_shape=jax.ShapeDtypeStruct((bsz, out_len, d), F32),
        compiler_params=_cparams(2),
        name="swiglu_final" if final else "swiglu",
    )(x_cat, mod_sel, *consts)


def kernel(x, c, ctx, c_ctx, w_mod, b_mod, norm1_g, w_in, dn_conv_w, dn_a_log, dn_dt_bias, dn_onorm_g, w_dn_out,
           sc_conv_w, w_sc_out, w_fn_out, w_gate, b_gate, w_o, norm2_g, w_ffn_in, w_ffn_out, final_g):
    bsz, l_lat, d = x.shape
    lc = ctx.shape[1]
    assert d == D_MODEL and lc == TILE and l_lat % TILE == 0 and w_ffn_in.shape[-1] == 2 * D_FF
    depth = w_mod.shape[0]
    nt_lat = l_lat // TILE
    nt_all = nt_lat + 1
    n_lat_chunks = l_lat // CHUNK

    rows = ((bsz + 1 + 7) // 8) * 8
    c_rows = jnp.zeros((rows, d), F32).at[:bsz].set(c).at[bsz].set(c_ctx)
    mod = _mod_call(c_rows, w_mod, b_mod).reshape(depth, rows, 6, d)
    mod_sel = jnp.stack([mod[:, :bsz], jnp.broadcast_to(mod[:, bsz:bsz + 1], (depth, bsz, 6, d))], axis=2)

    x_src, ctx_src, ctx_blk = x, ctx, 0
    row = lambda a: a.reshape(1, -1)
    lane_pad = lambda a: jnp.zeros((1, HEAD_DIM), F32).at[0, N_DH:2 * N_DH].set(a.reshape(-1))
    for layer in range(depth):
        last = layer == depth - 1
        wl = w_in[layer]
        qkv_w = 3 * KEYW
        o_z, o_ba, o_sc, o_fn = qkv_w, qkv_w + KEYW, qkv_w + KEYW + 2 * N_DH, qkv_w + KEYW + 2 * N_DH + 3 * KEYW
        wba = jnp.zeros((d, HEAD_DIM), F32).at[:, :2 * N_DH].set(wl[:, o_ba:o_sc])
        wparts = tuple(a.astype(BF16) for a in (wl[:, :o_z], wl[:, o_z:o_ba], wba, wl[:, o_sc:o_fn], wl[:, o_fn:]))
        q, k, v, z, bg, sc, fn = _inproj_call(
            x_src, ctx_src, ctx_blk, mod_sel[layer], row(norm1_g[layer]), wparts, dn_conv_w[layer], sc_conv_w[layer],
            lane_pad(dn_a_log[layer]), lane_pad(dn_dt_bias[layer]), nt_lat)
        u, w, qg, kd, attn, adec = _dnprep_call(q, k, v, bg)
        o_f, o_b = _dnscan_call(u, w, qg, kd, attn, adec, n_lat_chunks)
        fn_lat = _fourier_lat_call(fn, l_lat)
        nt_run = nt_lat if last else nt_all
        fn_ctx = jnp.zeros((bsz, lc, KEYW), BF16) if last else _fourier_ctx_call(fn, l_lat)
        x_cat = _merge_call(
            x_src, ctx_src, ctx_blk, mod_sel[layer], row(norm1_g[layer]), o_f, o_b, z, sc, fn_lat, fn_ctx,
            w_gate[layer].astype(BF16), row(b_gate[layer]), row(dn_onorm_g[layer]),
            w_dn_out[layer].astype(BF16), w_sc_out[layer].astype(BF16), w_fn_out[layer].astype(BF16),
            w_o[layer].astype(BF16), nt_lat, nt_run)
        x_cat = _ffn_call(x_cat, mod_sel[layer], row(norm2_g[layer]), w_ffn_in[layer].astype(BF16),
                          w_ffn_out[layer].astype(BF16), row(final_g), nt_lat, nt_run, last)
        x_src, ctx_src, ctx_blk = x_cat, x_cat, nt_lat
    return x_cat
```

```python
import functools
import math

import numpy as np
import jax
import jax.numpy as jnp
from jax import lax
from jax.experimental import pallas as pl
from jax.experimental.pallas import tpu as pltpu

F32 = jnp.float32
BF16 = jnp.bfloat16
HIGHEST = lax.Precision.HIGHEST

EPS = 1e-6
D_MODEL = 1024
DEPTH = 2
HEADS = 4
HEAD_DIM = 128
KEYW = HEADS * HEAD_DIM
CHUNK = 64
D_FF = 2816
TILE = 256
N_DH = 2 * HEADS
PACKW = HEADS * CHUNK
PREP_CHUNKS = 22
PREP_GROUPS = (2, 5, 5, 5, 5)
SCAN_BATCH = 8
SCAN_CHUNKS = 4
ROW_BATCH = 2
VMEM_LIMIT = 56 * 1024 * 1024


def _cparams(n_axes):
    return pltpu.CompilerParams(dimension_semantics=("arbitrary",) * n_axes, vmem_limit_bytes=VMEM_LIMIT)


def _dot(a, b):
    return jnp.dot(a, b, preferred_element_type=F32)


def _dot_hi(a, b):
    return jnp.dot(a, b, preferred_element_type=F32, precision=HIGHEST)


def _dot_nt(a, b):
    return lax.dot_general(a, b, (((1,), (1,)), ((), ())), preferred_element_type=F32)


def _dot_tn(a, b):
    return lax.dot_general(a, b, (((0,), (0,)), ((), ())), preferred_element_type=F32)


def _silu(t):
    return t * jax.nn.sigmoid(t)


def _rms(t, axis_size):
    return t * lax.rsqrt(jnp.sum(t * t, axis=-1, keepdims=True) * (1.0 / axis_size) + EPS)


def _mod_kernel(c_ref, w_ref, b_ref, o_ref):
    o_ref[...] = _dot_hi(_silu(c_ref[...]), w_ref[...]) + b_ref[...]


def _mod_call(c_rows, w_mod, b_mod):
    depth, d, n = w_mod.shape
    rows = c_rows.shape[0]
    tn = 1536
    return pl.pallas_call(
        _mod_kernel,
        grid=(depth, n // tn),
        in_specs=[
            pl.BlockSpec((rows, d), lambda l, j: (0, 0)),
            pl.BlockSpec((None, d, tn), lambda l, j: (l, 0, j)),
            pl.BlockSpec((None, 1, tn), lambda l, j: (l, 0, j)),
        ],
        out_specs=pl.BlockSpec((None, rows, tn), lambda l, j: (l, 0, j)),
        out_shape=jax.ShapeDtypeStruct((depth, rows, n), F32),
        compiler_params=_cparams(2),
        name="mod_vectors",
    )(c_rows, w_mod, b_mod.reshape(depth, 1, n))


def _row_masks(is_lat):
    pos = lax.broadcasted_iota(jnp.int32, (TILE, 1), 0)
    row_mask = jnp.where(is_lat, CHUNK - 1, TILE - 1)
    in_row = pos & row_mask
    return jnp.where(in_row == 0, 0.0, 1.0).astype(F32), jnp.where(in_row == row_mask, 0.0, 1.0).astype(F32)


def _mask_rows(t, mask, first):
    pieces = []
    for g in range(TILE // CHUNK):
        lo, hi = g * CHUNK, (g + 1) * CHUNK
        if first:
            pieces += [t[lo:lo + 8] * mask[lo:lo + 8], t[lo + 8:hi]]
        else:
            pieces += [t[lo:hi - 8], t[hi - 8:hi] * mask[hi - 8:hi]]
    return jnp.concatenate(pieces, axis=0)


def _conv3(t, w_ref, m_prev, m_next):
    prev = _mask_rows(pltpu.roll(t, 1, 0), m_prev, True)
    nxt = _mask_rows(pltpu.roll(t, TILE - 1, 0), m_next, False)
    return prev * w_ref[0:1, :] + t * w_ref[1:2, :] + nxt * w_ref[2:3, :]


def _modulated(x, g_ref, shift, scale):
    return _rms(x, D_MODEL) * (g_ref[...] * (1.0 + scale)) + shift


def _inproj_kernel(nt_lat, x_ref, ctx_ref, mod_ref, g_ref, wqkv_ref, wz_ref, wba_ref, wsc_ref, wfn_ref,
                   dnconv_ref, scconv_ref, alog_ref, dtb_ref,
                   q_ref, k_ref, v_ref, z_ref, bg_ref, sc_ref, fn_ref):
    is_lat = pl.program_id(1) < nt_lat
    m_prev, m_next = _row_masks(is_lat)
    n_rows = x_ref.shape[0]
    hs = []
    for s in range(n_rows):
        x = jnp.where(is_lat, x_ref[s], ctx_ref[s])
        hs.append(_modulated(x, g_ref, mod_ref[s, 0:1, :], mod_ref[s, 1:2, :]).astype(BF16))
    projs = [tuple(_dot(h, w_ref[...]) for w_ref in (wqkv_ref, wz_ref, wba_ref, wsc_ref, wfn_ref)) for h in hs]
    for s in range(n_rows):
        p_qkv, p_z, ba, scp, p_fn = projs[s]
        qkv = _silu(_conv3(p_qkv, dnconv_ref, m_prev, m_next))
        for hd in range(HEADS):
            lo = hd * HEAD_DIM
            qh = qkv[:, lo:lo + HEAD_DIM]
            kh = qkv[:, KEYW + lo:KEYW + lo + HEAD_DIM]
            q_ref[s, :, lo:lo + HEAD_DIM] = (qh * lax.rsqrt(jnp.sum(qh * qh, axis=-1, keepdims=True) + EPS)
                                             * (HEAD_DIM ** -0.5))
            k_ref[s, :, lo:lo + HEAD_DIM] = kh * lax.rsqrt(jnp.sum(kh * kh, axis=-1, keepdims=True) + EPS)
        v_ref[s] = qkv[:, 2 * KEYW:]
        z_ref[s] = p_z

        sp_in = ba + dtb_ref[...]
        softplus = jnp.maximum(sp_in, 0.0) + jnp.log1p(jnp.exp(-jnp.abs(sp_in)))
        lane = lax.broadcasted_iota(jnp.int32, ba.shape, 1)
        g = -jnp.exp(alog_ref[...]) * softplus
        in_chunk = lax.broadcasted_iota(jnp.int32, (TILE, 1), 0) & (CHUNK - 1)
        pre = g
        step = 1
        while step < CHUNK:
            pre = pre + jnp.where(in_chunk >= step, pltpu.roll(pre, step, 0), 0.0)
            step *= 2
        total = jnp.concatenate(
            [jnp.broadcast_to(pre[c * CHUNK + CHUNK - 1:(c + 1) * CHUNK, :], (CHUNK, HEAD_DIM))
             for c in range(TILE // CHUNK)], axis=0)
        suf = (total - pre) + g
        gc = jnp.where(lane < N_DH + HEADS, pre, suf)
        bg_ref[s] = jnp.where(lane < N_DH, jax.nn.sigmoid(ba), gc)

        sc_ref[s] = (scp[:, :KEYW] * _conv3(scp[:, KEYW:2 * KEYW] * scp[:, 2 * KEYW:], scconv_ref, m_prev, m_next)
                     ).astype(BF16)
        fn_ref[s] = p_fn.astype(BF16)


def _const_spec(a):
    return pl.BlockSpec(a.shape, lambda b, j: (0,) * a.ndim, pipeline_mode=pl.Buffered(1))


def _mod_spec(bb, nt_lat, d):
    return pl.BlockSpec((bb, None, 6, d), lambda b, j: (b, jnp.where(j < nt_lat, 0, 1), 0, 0))


def _x_specs(bb, nt_lat, ctx_blk, d):
    return [pl.BlockSpec((bb, TILE, d), lambda b, j: (b, jnp.minimum(j, nt_lat - 1), 0)),
            pl.BlockSpec((bb, TILE, d), lambda b, j: (b, ctx_blk, 0))]


def _inproj_call(x_src, ctx_src, ctx_blk, mod_sel, n1g, wparts, dnconv, scconv, alog_row, dtb_row, nt_lat):
    bsz, _, d = x_src.shape
    nt = nt_lat + 1
    ltot = nt * TILE
    bb = ROW_BATCH if bsz % ROW_BATCH == 0 else 1
    wqkv, wz, wba, wsc, wfn = wparts
    tok = lambda w: pl.BlockSpec((bb, TILE, w), lambda b, j: (b, j, 0))
    out_w = (KEYW, KEYW, KEYW, KEYW, HEAD_DIM, KEYW, KEYW)
    consts = (n1g, wqkv, wz, wba, wsc, wfn, dnconv, scconv, alog_row, dtb_row)
    return pl.pallas_call(
        functools.partial(_inproj_kernel, nt_lat),
        grid=(bsz // bb, nt),
        in_specs=_x_specs(bb, nt_lat, ctx_blk, d) + [_mod_spec(bb, nt_lat, d)] + [_const_spec(a) for a in consts],
        out_specs=[tok(w) for w in out_w],
        out_shape=[jax.ShapeDtypeStruct((bsz, ltot, w), F32) for w in out_w[:-2]]
        + [jax.ShapeDtypeStruct((bsz, ltot, w), BF16) for w in out_w[-2:]],
        compiler_params=_cparams(2),
        name="in_proj",
    )(x_src, ctx_src, mod_sel, *consts)


def _split2(t):
    hi = t.astype(BF16)
    return hi, (t - hi.astype(F32)).astype(BF16)


def _mm3(lhs, rhs):
    m = lhs.shape[0]
    lh, ll = _split2(lhs)
    rh, rl = _split2(rhs)
    top = _dot(jnp.concatenate([lh, ll], axis=0), rh)
    return top[:m] + top[m:] + _dot(lh, rl)


def _dnprep_kernel(cps, q_ref, k_ref, v_ref, bg_ref, u_ref, w_ref, qg_ref, kd_ref, attn_ref, adec_ref):
    ri = lax.broadcasted_iota(jnp.int32, (CHUNK, PACKW), 0)
    cj = lax.broadcasted_iota(jnp.int32, (CHUNK, PACKW), 1) & (CHUNK - 1)
    eye_p = jnp.where(ri == cj, 1.0, 0.0).astype(F32)
    incl_m = ((ri >= cj), (ri <= cj))
    strict_m = ((ri > cj), (ri < cj))

    def half_masked(t, hd):
        lane = lax.broadcasted_iota(jnp.int32, t.shape, 1)
        keep = (lane < CHUNK) if hd % 2 == 0 else (lane >= CHUNK)
        return jnp.where(keep, t, jnp.zeros((), BF16))

    def blockdiag(t):
        z = jnp.zeros((CHUNK, HEAD_DIM), BF16)
        blocks = []
        for hd in range(HEADS):
            piece = half_masked(t[:, (hd // 2) * HEAD_DIM:(hd // 2 + 1) * HEAD_DIM], hd)
            blocks.append(jnp.concatenate([piece, z] if hd < 2 else [z, piece], axis=1))
        return jnp.concatenate(blocks, axis=0)

    def mm3_all(lhs_list, rhs_list):
        lsp = [_split2(l) for l in lhs_list]
        rsp = [tuple(blockdiag(p) for p in _split2(r)) for r in rhs_list]
        tops = [_dot(jnp.concatenate([lh, ll], axis=0), rh) for (lh, ll), (rh, _) in zip(lsp, rsp)]
        lows = [_dot(lh, rl) for (lh, _), (_, rl) in zip(lsp, rsp)]
        return [t[:l.shape[0]] + t[l.shape[0]:] + lo for t, lo, l in zip(tops, lows, lhs_list)]

    sizes = PREP_GROUPS if sum(PREP_GROUPS) == cps else (cps,)
    for first, size in zip(np.cumsum((0,) + sizes[:-1]), sizes):
        _dnprep_group(int(first), size, q_ref, k_ref, v_ref, bg_ref,
                      u_ref, w_ref, qg_ref, kd_ref, attn_ref, adec_ref,
                      eye_p, incl_m, strict_m, half_masked, blockdiag, mm3_all)


def _dnprep_group(first, cps, q_ref, k_ref, v_ref, bg_ref, u_ref, w_ref, qg_ref, kd_ref, attn_ref, adec_ref,
                  eye_p, incl_m, strict_m, half_masked, blockdiag, mm3_all):
    chunk_rows = [slice((first + ci) * CHUNK, (first + ci + 1) * CHUNK) for ci in range(cps)]
    qs = [q_ref[r, :] for r in chunk_rows]
    ks = [k_ref[r, :] for r in chunk_rows]
    vs = [v_ref[r, :] for r in chunk_rows]

    lo_half = lax.broadcasted_iota(jnp.int32, (CHUNK, HEAD_DIM), 1) < CHUNK
    expanded = {}
    for ci, r in enumerate(chunk_rows):
        bg = bg_ref[r, :]
        for dr in range(2):
            cols = [(jnp.broadcast_to(bg[:, N_DH + dr * HEADS + hd:N_DH + dr * HEADS + hd + 1], (CHUNK, HEAD_DIM)),
                     jnp.broadcast_to(bg[:, dr * HEADS + hd:dr * HEADS + hd + 1], (CHUNK, HEAD_DIM)))
                    for hd in range(HEADS)]
            gcol = jnp.concatenate([c[0] for c in cols], axis=1)
            beta = jnp.concatenate([c[1] for c in cols], axis=1)
            gcol_p = jnp.concatenate([jnp.where(lo_half, cols[0][0], cols[1][0]),
                                      jnp.where(lo_half, cols[2][0], cols[3][0])], axis=1)
            expanded[ci, dr] = (gcol, beta, gcol_p)

    def parts(ci, dr):
        return expanded[ci, dr]

    m1s = []
    for ci in range(cps):
        k = ks[ci]
        k_t = jnp.concatenate([k, k], axis=0).T.astype(BF16)
        zk = jnp.zeros((HEAD_DIM, HEAD_DIM), BF16)
        bd_rows = []
        for hd in range(HEADS):
            piece = half_masked(k_t[hd * HEAD_DIM:(hd + 1) * HEAD_DIM, :], hd)
            bd_rows.append(jnp.concatenate([piece, zk] if hd < 2 else [zk, piece], axis=1))
        bd_k = jnp.concatenate(bd_rows, axis=0)
        kbs = [k * parts(ci, dr)[1] for dr in range(2)]
        m1s.append(_dot(jnp.concatenate(kbs + [qs[ci]], axis=0).astype(BF16), bd_k))

    chains = [(ci, dr) for ci in range(cps) for dr in range(2)]
    decays, neg_as = [], []
    for ci, dr in chains:
        gcol_p = parts(ci, dr)[2]
        grow_p = jnp.sum(gcol_p * eye_p, axis=0, keepdims=True)
        decay = jnp.exp(jnp.where(incl_m[dr], gcol_p - grow_p, -jnp.inf))
        decays.append(decay)
        neg_as.append(jnp.where(strict_m[dr], -(m1s[ci][dr * CHUNK:(dr + 1) * CHUNK] * decay), 0.0))

    p_accs = [eye_p + a for a in neg_as]
    q_pows = mm3_all(neg_as, neg_as)
    for _ in range(4):
        boths = mm3_all([jnp.concatenate([qp, pa], axis=0) for qp, pa in zip(q_pows, p_accs)], q_pows)
        q_pows = [b[:CHUNK] for b in boths]
        p_accs = [pa + b[CHUNK:] for pa, b in zip(p_accs, boths)]
    lasts = mm3_all(p_accs, q_pows)
    p_accs = [pa + la for pa, la in zip(p_accs, lasts)]

    uws, egs = [], []
    for (ci, dr), t_inv in zip(chains, p_accs):
        gcol, beta, _ = parts(ci, dr)
        eg = jnp.exp(gcol)
        egs.append(eg)
        vb = (vs[ci] * beta).astype(BF16)
        kbe = (ks[ci] * (beta * eg)).astype(BF16)
        rhs = jnp.concatenate(
            [jnp.concatenate([vb[:, hd * HEAD_DIM:(hd + 1) * HEAD_DIM], kbe[:, hd * HEAD_DIM:(hd + 1) * HEAD_DIM]], axis=1)
             for hd in range(HEADS)], axis=0)
        res = _dot(blockdiag(t_inv.astype(BF16)), rhs)
        uws.append(jnp.concatenate(
            [res[hd * CHUNK:(hd + 1) * CHUNK, :HEAD_DIM] for hd in range(HEADS)]
            + [res[hd * CHUNK:(hd + 1) * CHUNK, HEAD_DIM:] for hd in range(HEADS)], axis=1))

    for (ci, dr), uw, eg, decay in zip(chains, uws, egs, decays):
        rows = chunk_rows[ci]
        gcol = parts(ci, dr)[0]
        g_last = gcol[CHUNK - 1:CHUNK, :] if dr == 0 else gcol[0:1, :]
        u_ref[dr, rows, :] = uw[:, :KEYW].astype(BF16)
        w_ref[dr, rows, :] = uw[:, KEYW:].astype(BF16)
        qg_ref[dr, rows, :] = (qs[ci] * eg).astype(BF16)
        kd_ref[dr, rows, :] = (ks[ci] * jnp.exp(g_last - gcol)).astype(BF16)
        attn_ref[dr, rows, :] = (m1s[ci][2 * CHUNK:] * decay).astype(BF16)
        a_dec = jnp.exp(g_last)
        for hd in range(HEADS):
            unit = dr * HEADS + hd
            adec_ref[first + ci, unit:unit + 1, :] = a_dec[:, hd * HEAD_DIM:(hd + 1) * HEAD_DIM]


def _dnprep_call(q, k, v, bg):
    bsz, ltot, _ = q.shape
    nch = ltot // CHUNK
    cps = PREP_CHUNKS if nch % PREP_CHUNKS == 0 else math.gcd(nch, PREP_CHUNKS)
    tok = lambda w: pl.BlockSpec((None, cps * CHUNK, w), lambda b, n: (b, n, 0))
    dtok = lambda w: pl.BlockSpec((None, 2, cps * CHUNK, w), lambda b, n: (b, 0, n, 0))
    return pl.pallas_call(
        functools.partial(_dnprep_kernel, cps),
        grid=(bsz, nch // cps),
        in_specs=[tok(KEYW), tok(KEYW), tok(KEYW), tok(HEAD_DIM)],
        out_specs=[dtok(KEYW), dtok(KEYW), dtok(KEYW), dtok(KEYW), dtok(PACKW),
                   pl.BlockSpec((None, cps, N_DH, HEAD_DIM), lambda b, n: (b, n, 0, 0))],
        out_shape=[jax.ShapeDtypeStruct((bsz, 2, ltot, KEYW), BF16),
                   jax.ShapeDtypeStruct((bsz, 2, ltot, KEYW), BF16),
                   jax.ShapeDtypeStruct((bsz, 2, ltot, KEYW), BF16),
                   jax.ShapeDtypeStruct((bsz, 2, ltot, KEYW), BF16),
                   jax.ShapeDtypeStruct((bsz, 2, ltot, PACKW), BF16),
                   jax.ShapeDtypeStruct((bsz, nch, N_DH, HEAD_DIM), F32)],
        compiler_params=_cparams(2),
        name="dn_prep",
    )(q, k, v, bg)


def _dnscan_kernel(uf_ref, wf_ref, qgf_ref, kdf_ref, af_ref, df_ref,
                   ub_ref, wb_ref, qgb_ref, kdb_ref, ab_ref, db_ref,
                   of_ref, ob_ref, s_ref):
    @pl.when(pl.program_id(1) == 0)
    def _():
        s_ref[...] = jnp.zeros_like(s_ref)

    per_dir = ((uf_ref, wf_ref, qgf_ref, kdf_ref, af_ref, df_ref, of_ref),
               (ub_ref, wb_ref, qgb_ref, kdb_ref, ab_ref, db_ref, ob_ref))
    units = [(bi, dr, hd) for bi in range(uf_ref.shape[0]) for dr in range(2) for hd in range(HEADS)]
    states = [s_ref[bi, dr * HEADS + hd] for bi, dr, hd in units]
    n_sub = uf_ref.shape[1] // CHUNK
    for sub in range(n_sub):
        pos = (sub, n_sub - 1 - sub)
        rows = [slice(p * CHUNK, (p + 1) * CHUNK) for p in pos]
        first = []
        for (bi, dr, hd), s in zip(units, states):
            _, w_ref, qg_ref = per_dir[dr][:3]
            lo = hd * HEAD_DIM
            first.append(_dot(jnp.concatenate([w_ref[bi, rows[dr], lo:lo + HEAD_DIM],
                                               qg_ref[bi, rows[dr], lo:lo + HEAD_DIM]], axis=0), s.astype(BF16)))
        new_states = []
        for (bi, dr, hd), both, s in zip(units, first, states):
            u_ref, _, _, kd_ref, a_ref, d_ref, o_ref = per_dir[dr]
            unit = dr * HEADS + hd
            lo = hd * HEAD_DIM
            v_new_b = (u_ref[bi, rows[dr], lo:lo + HEAD_DIM].astype(F32) - both[:CHUNK]).astype(BF16)
            o_ref[bi, rows[dr], lo:lo + HEAD_DIM] = (
                both[CHUNK:] + _dot(a_ref[bi, rows[dr], hd * CHUNK:(hd + 1) * CHUNK], v_new_b)).astype(BF16)
            new_states.append(s * d_ref[bi, pos[dr], unit:unit + 1, :]
                              + _dot_tn(kd_ref[bi, rows[dr], lo:lo + HEAD_DIM], v_new_b))
        states = new_states
    for (bi, dr, hd), s in zip(units, states):
        s_ref[bi, dr * HEADS + hd] = s


def _dnscan_call(u, w, qg, kd, attn, adec, n_lat_chunks):
    bsz, _, ltot, _ = u.shape
    nch = ltot // CHUNK
    bb = SCAN_BATCH if bsz % SCAN_BATCH == 0 else 1
    cpg = SCAN_CHUNKS if (nch % SCAN_CHUNKS == 0 and n_lat_chunks % SCAN_CHUNKS == 0) else 1
    nblk = nch // cpg
    fwd = lambda n: (n + n_lat_chunks // cpg) % nblk
    bwd = lambda n: nblk - 1 - n

    def dspec(wd, dr, order):
        return pl.BlockSpec((bb, None, cpg * CHUNK, wd), lambda b, n: (b, dr, order(n), 0))

    def aspec(order):
        return pl.BlockSpec((bb, cpg, N_DH, HEAD_DIM), lambda b, n: (b, order(n), 0, 0))

    def ospec(order):
        return pl.BlockSpec((bb, cpg * CHUNK, KEYW), lambda b, n: (b, order(n), 0))

    in_specs = []
    for dr, order in ((0, fwd), (1, bwd)):
        in_specs += [dspec(KEYW, dr, order), dspec(KEYW, dr, order), dspec(KEYW, dr, order), dspec(KEYW, dr, order),
                     dspec(PACKW, dr, order), aspec(order)]
    return pl.pallas_call(
        _dnscan_kernel,
        grid=(bsz // bb, nblk),
        in_specs=in_specs,
        out_specs=[ospec(fwd), ospec(bwd)],
        out_shape=[jax.ShapeDtypeStruct((bsz, ltot, KEYW), BF16)] * 2,
        scratch_shapes=[pltpu.VMEM((bb, N_DH, HEAD_DIM, HEAD_DIM), F32)],
        compiler_params=_cparams(2),
        name="dn_scan",
    )(u, w, qg, kd, attn, adec, u, w, qg, kd, attn, adec)


def _dft_mats(n):
    idx = np.arange(n)
    ang = 2.0 * np.pi * ((idx[:, None] * idx[None, :]) % n) / n
    return np.cos(ang), np.sin(ang)


def _fft1_kernel(n2blk, u_ref, f1_ref, tc_ref, ts_ref, br_ref, bi_ref):
    l1 = u_ref.shape[1]
    xts = [pltpu.einshape("abc->bac", u_ref[s].astype(F32)).astype(BF16)
           for s in range(u_ref.shape[0])]
    prods = [[_dot(f1_ref[...], xt[jj]) for jj in range(n2blk)] for xt in xts]
    for s, per_n2 in enumerate(prods):
        for jj, a in enumerate(per_n2):
            ar, ai = a[:l1], a[l1:]
            tc = jnp.concatenate([tc_ref[jj]] * (KEYW // HEAD_DIM), axis=1)
            ts = jnp.concatenate([ts_ref[jj]] * (KEYW // HEAD_DIM), axis=1)
            br_ref[s, jj] = (ar * tc + ai * ts).astype(BF16)
            bi_ref[s, jj] = (ai * tc - ar * ts).astype(BF16)


def _fft2_kernel(k1blk, nlen, br_ref, bi_ref, f2_ref, wc_ref, y_ref):
    rows = k1blk * nlen
    n_rows = br_ref.shape[0]
    swap = lambda t: pltpu.einshape("abc->bac", t)
    brs = [swap(br_ref[s].astype(F32)).astype(BF16) for s in range(n_rows)]
    bis = [swap(bi_ref[s].astype(F32)).astype(BF16) for s in range(n_rows)]
    xs = [[_dot(f2_ref[...], jnp.concatenate([brs[s][jj], bis[s][jj]], axis=0)) for jj in range(k1blk)]
          for s in range(n_rows)]
    for s in range(n_rows):
        xr = jnp.concatenate([x[:nlen] for x in xs[s]], axis=0).astype(BF16)
        xi = jnp.concatenate([x[nlen:] for x in xs[s]], axis=0).astype(BF16)
        y = jnp.concatenate(
            [_dot(jnp.concatenate([xr[:, g * HEAD_DIM:(g + 1) * HEAD_DIM], xi[:, g * HEAD_DIM:(g + 1) * HEAD_DIM]],
                                  axis=1), wc_ref[...]) for g in range(KEYW // HEAD_DIM)], axis=1)
        y_ref[s] = swap(y.reshape(k1blk, nlen, KEYW)).astype(BF16)


def _dftctx_kernel(nlen, u_ref, f_ref, wc_ref, y_ref):
    x = _dot(f_ref[...], u_ref[...])
    xr, xi = x[:nlen].astype(BF16), x[nlen:].astype(BF16)
    for g in range(KEYW // HEAD_DIM):
        gs = slice(g * HEAD_DIM, (g + 1) * HEAD_DIM)
        y_ref[:, gs] = _dot(jnp.concatenate([xr[:, gs], xi[:, gs]], axis=1), wc_ref[...]).astype(BF16)


def _chan_mat(total_len):
    cc, sc = _dft_mats(HEAD_DIM)
    return jnp.asarray(np.concatenate([cc, sc], axis=0) / math.sqrt(total_len * HEAD_DIM), BF16)


def _fourier_lat_call(fn_cat, l_lat):
    bsz, ltot, _ = fn_cat.shape
    l1 = l_lat // CHUNK
    n2blk = 16
    k1blk = 16 if l1 % 16 == 0 else l1
    c1, s1 = _dft_mats(l1)
    f1 = jnp.asarray(np.concatenate([c1, -s1], axis=0), BF16)
    n2 = np.arange(CHUNK)[:, None, None]
    k1 = np.arange(l1)[None, :, None]
    ang = 2.0 * np.pi * (n2 * k1) / l_lat * np.ones((1, 1, HEAD_DIM))
    tc, ts = jnp.asarray(np.cos(ang), F32), jnp.asarray(np.sin(ang), F32)
    c2, s2 = _dft_mats(CHUNK)
    f2 = jnp.asarray(np.block([[c2, s2], [-s2, c2]]), BF16)
    wc = _chan_mat(l_lat)

    u_view = fn_cat.reshape(bsz, ltot // CHUNK, CHUNK, KEYW)
    bb = ROW_BATCH if bsz % ROW_BATCH == 0 else 1
    mid_spec = pl.BlockSpec((bb, n2blk, l1, KEYW), lambda b, i: (b, i, 0, 0))
    br, bi = pl.pallas_call(
        functools.partial(_fft1_kernel, n2blk),
        grid=(bsz // bb, CHUNK // n2blk),
        in_specs=[pl.BlockSpec((bb, l1, n2blk, KEYW), lambda b, i: (b, 0, i, 0)),
                  pl.BlockSpec(f1.shape, lambda b, i: (0, 0)),
                  pl.BlockSpec((n2blk, l1, HEAD_DIM), lambda b, i: (i, 0, 0)),
                  pl.BlockSpec((n2blk, l1, HEAD_DIM), lambda b, i: (i, 0, 0))],
        out_specs=[mid_spec, mid_spec],
        out_shape=[jax.ShapeDtypeStruct((bsz, CHUNK, l1, KEYW), BF16)] * 2,
        compiler_params=_cparams(2),
        name="fourier_stage1",
    )(u_view, f1, tc, ts)

    rspec = pl.BlockSpec((bb, CHUNK, k1blk, KEYW), lambda b, i: (b, 0, i, 0))
    y = pl.pallas_call(
        functools.partial(_fft2_kernel, k1blk, CHUNK),
        grid=(bsz // bb, l1 // k1blk),
        in_specs=[rspec, rspec,
                  pl.BlockSpec(f2.shape, lambda b, i: (0, 0)),
                  pl.BlockSpec(wc.shape, lambda b, i: (0, 0))],
        out_specs=rspec,
        out_shape=jax.ShapeDtypeStruct((bsz, CHUNK, l1, KEYW), BF16),
        compiler_params=_cparams(2),
        name="fourier_stage2",
    )(br, bi, f2, wc)
    return y.reshape(bsz, l_lat, KEYW)


def _fourier_ctx_call(fn_cat, l_lat):
    bsz, ltot, _ = fn_cat.shape
    lc = ltot - l_lat
    c, s = _dft_mats(lc)
    f = jnp.asarray(np.concatenate([c, -s], axis=0), BF16)
    wc = _chan_mat(lc)
    return pl.pallas_call(
        functools.partial(_dftctx_kernel, lc),
        grid=(bsz,),
        in_specs=[pl.BlockSpec((None, lc, KEYW), lambda b: (b, l_lat // lc, 0)),
                  pl.BlockSpec(f.shape, lambda b: (0, 0)),
                  pl.BlockSpec(wc.shape, lambda b: (0, 0))],
        out_specs=pl.BlockSpec((None, lc, KEYW), lambda b: (b, 0, 0)),
        out_shape=jax.ShapeDtypeStruct((bsz, lc, KEYW), BF16),
        compiler_params=_cparams(1),
        name="fourier_ctx",
    )(fn_cat, f, wc)


def _merge_kernel(nt_lat, x_ref, ctx_ref, mod_ref, of_ref, ob_ref, z_ref, sc_ref, fl_ref, fc_ref,
                  g_ref, wgate_ref, bgate_ref, og_ref, wdn_ref, wsc_ref, wfn_ref, wo_ref, o_ref):
    is_lat = pl.program_id(1) < nt_lat
    n_rows = x_ref.shape[0]
    xs = [jnp.where(is_lat, x_ref[s], ctx_ref[s]) for s in range(n_rows)]
    hs = [_modulated(xs[s], g_ref, mod_ref[s, 0:1, :], mod_ref[s, 1:2, :]).astype(BF16) for s in range(n_rows)]
    logits = [_dot(h, wgate_ref[...]) for h in hs]
    branch_in = []
    for s in range(n_rows):
        o = of_ref[s].astype(F32) + ob_ref[s].astype(F32)
        z = z_ref[s]
        dn_parts = []
        for hd in range(HEADS):
            sl = slice(hd * HEAD_DIM, (hd + 1) * HEAD_DIM)
            dn_parts.append(_rms(o[:, sl], HEAD_DIM) * og_ref[...] * _silu(z[:, sl]))
        fn = jnp.where(is_lat, fl_ref[s], fc_ref[s])
        branch_in.append((jnp.concatenate(dn_parts, axis=1).astype(BF16), sc_ref[s], fn))
    ys = [(_dot(dn, wdn_ref[...]), _dot(sc, wsc_ref[...]), _dot(fn, wfn_ref[...])) for dn, sc, fn in branch_in]
    merged = []
    for s in range(n_rows):
        gates = jax.nn.sigmoid(logits[s] + bgate_ref[...])
        y_dn, y_sc, y_fn = ys[s]
        merged.append((gates[:, :D_MODEL] * y_dn + gates[:, D_MODEL:2 * D_MODEL] * y_sc
                       + gates[:, 2 * D_MODEL:] * y_fn).astype(BF16))
    mixes = [_dot(m, wo_ref[...]) for m in merged]
    for s in range(n_rows):
        o_ref[s] = xs[s] + mod_ref[s, 2:3, :] * mixes[s]


def _merge_call(x_src, ctx_src, ctx_blk, mod_sel, n1g, o_f, o_b, z, sc, fn_lat, fn_ctx, wgate, bgate, og, wdn, wsc,
                wfn, wo, nt_lat, nt_run):
    bsz, _, d = x_src.shape
    ltot = (nt_lat + 1) * TILE
    bb = ROW_BATCH if bsz % ROW_BATCH == 0 else 1
    tok = lambda w: pl.BlockSpec((bb, TILE, w), lambda b, j: (b, j, 0))
    consts = (n1g, wgate, bgate, og, wdn, wsc, wfn, wo)
    return pl.pallas_call(
        functools.partial(_merge_kernel, nt_lat),
        grid=(bsz // bb, nt_run),
        in_specs=_x_specs(bb, nt_lat, ctx_blk, d) + [_mod_spec(bb, nt_lat, d),
                  tok(KEYW), tok(KEYW), tok(KEYW), tok(KEYW),
                  pl.BlockSpec((bb, TILE, KEYW), lambda b, j: (b, jnp.minimum(j, nt_lat - 1), 0)),
                  pl.BlockSpec((bb, TILE, KEYW), lambda b, j: (b, 0, 0))] + [_const_spec(a) for a in consts],
        out_specs=tok(d),
        out_shape=jax.ShapeDtypeStruct((bsz, ltot, d), F32),
        compiler_params=_cparams(2),
        name="branch_merge",
    )(x_src, ctx_src, mod_sel, o_f, o_b, z, sc, fn_lat, fn_ctx, *consts)


def _ffn_kernel(final, x_ref, mod_ref, g_ref, win_ref, wout_ref, fg_ref, o_ref):
    n_rows = x_ref.shape[0]
    hs = [_modulated(x_ref[s], g_ref, mod_ref[s, 3:4, :], mod_ref[s, 4:5, :]).astype(BF16) for s in range(n_rows)]
    abs_ = [_dot(h, win_ref[...]) for h in hs]
    acts = [(_silu(ab[:, :D_FF]) * ab[:, D_FF:]).astype(BF16) for ab in abs_]
    downs = [_dot(act, wout_ref[...]) for act in acts]
    for s in range(n_rows):
        y = x_ref[s] + mod_ref[s, 5:6, :] * downs[s]
        if final:
            y = _rms(y, D_MODEL) * fg_ref[...]
        o_ref[s] = y


def _ffn_call(x_cat, mod_sel, n2g, win, wout, fg, nt_lat, nt_run, final):
    bsz, ltot, d = x_cat.shape
    out_len = nt_run * TILE if final else ltot
    bb = ROW_BATCH if bsz % ROW_BATCH == 0 else 1
    tok = pl.BlockSpec((bb, TILE, d), lambda b, j: (b, j, 0))
    consts = (n2g, win, wout, fg)
    return pl.pallas_call(
        functools.partial(_ffn_kernel, final),
        grid=(bsz // bb, nt_run),
        in_specs=[tok, _mod_spec(bb, nt_lat, d)] + [_const_spec(a) for a in consts],
        out_specs=tok,
        out_shape=jax.ShapeDtypeStruct((bsz, out_len, d), F32),
        compiler_params=_cparams(2),
        name="swiglu_final" if final else "swiglu",
    )(x_cat, mod_sel, *consts)


def kernel(x, c, ctx, c_ctx, w_mod, b_mod, norm1_g, w_in, dn_conv_w, dn_a_log, dn_dt_bias, dn_onorm_g, w_dn_out,
           sc_conv_w, w_sc_out, w_fn_out, w_gate, b_gate, w_o, norm2_g, w_ffn_in, w_ffn_out, final_g):
    bsz, l_lat, d = x.shape
    lc = ctx.shape[1]
    assert d == D_MODEL and lc == TILE and l_lat % TILE == 0 and w_ffn_in.shape[-1] == 2 * D_FF
    depth = w_mod.shape[0]
    nt_lat = l_lat // TILE
    nt_all = nt_lat + 1
    n_lat_chunks = l_lat // CHUNK

    rows = ((bsz + 1 + 7) // 8) * 8
    c_rows = jnp.zeros((rows, d), F32).at[:bsz].set(c).at[bsz].set(c_ctx)
    mod = _mod_call(c_rows, w_mod, b_mod).reshape(depth, rows, 6, d)
    mod_sel = jnp.stack([mod[:, :bsz], jnp.broadcast_to(mod[:, bsz:bsz + 1], (depth, bsz, 6, d))], axis=2)

    x_src, ctx_src, ctx_blk = x, ctx, 0
    row = lambda a: a.reshape(1, -1)
    lane_pad = lambda a: jnp.zeros((1, HEAD_DIM), F32).at[0, N_DH:2 * N_DH].set(a.reshape(-1))
    for layer in range(depth):
        last = layer == depth - 1
        wl = w_in[layer]
        qkv_w = 3 * KEYW
        o_z, o_ba, o_sc, o_fn = qkv_w, qkv_w + KEYW, qkv_w + KEYW + 2 * N_DH, qkv_w + KEYW + 2 * N_DH + 3 * KEYW
        wba = jnp.zeros((d, HEAD_DIM), F32).at[:, :2 * N_DH].set(wl[:, o_ba:o_sc])
        wparts = tuple(a.astype(BF16) for a in (wl[:, :o_z], wl[:, o_z:o_ba], wba, wl[:, o_sc:o_fn], wl[:, o_fn:]))
        q, k, v, z, bg, sc, fn = _inproj_call(
            x_src, ctx_src, ctx_blk, mod_sel[layer], row(norm1_g[layer]), wparts, dn_conv_w[layer], sc_conv_w[layer],
            lane_pad(dn_a_log[layer]), lane_pad(dn_dt_bias[layer]), nt_lat)
        u, w, qg, kd, attn, adec = _dnprep_call(q, k, v, bg)
        o_f, o_b = _dnscan_call(u, w, qg, kd, attn, adec, n_lat_chunks)
        fn_lat = _fourier_lat_call(fn, l_lat)
        nt_run = nt_lat if last else nt_all
        fn_ctx = jnp.zeros((bsz, lc, KEYW), BF16) if last else _fourier_ctx_call(fn, l_lat)
        x_cat = _merge_call(
            x_src, ctx_src, ctx_blk, mod_sel[layer], row(norm1_g[layer]), o_f, o_b, z, sc, fn_lat, fn_ctx,
            w_gate[layer].astype(BF16), row(b_gate[layer]), row(dn_onorm_g[layer]),
            w_dn_out[layer].astype(BF16), w_sc_out[layer].astype(BF16), w_fn_out[layer].astype(BF16),
            w_o[layer].astype(BF16), nt_lat, nt_run)
        x_cat = _ffn_call(x_cat, mod_sel[layer], row(norm2_g[layer]), w_ffn_in[layer].astype(BF16),
                          w_ffn_out[layer].astype(BF16), row(final_g), nt_lat, nt_run, last)
        x_src, ctx_src, ctx_blk = x_cat, x_cat, nt_lat
    return x_cat
```

```python
import functools
import math

import numpy as np
import jax
import jax.numpy as jnp
from jax import lax
from jax.experimental import pallas as pl
from jax.experimental.pallas import tpu as pltpu

F32 = jnp.float32
BF16 = jnp.bfloat16
HIGHEST = lax.Precision.HIGHEST

EPS = 1e-6
D_MODEL = 1024
DEPTH = 2
HEADS = 4
HEAD_DIM = 128
KEYW = HEADS * HEAD_DIM
CHUNK = 64
D_FF = 2816
TILE = 256
N_DH = 2 * HEADS
PACKW = HEADS * CHUNK
PREP_CHUNKS = 22
PREP_GROUPS = (2, 5, 5, 5, 5)
SCAN_BATCH = 8
SCAN_CHUNKS = 4
ROW_BATCH = 2
VMEM_LIMIT = 56 * 1024 * 1024


def _cparams(n_axes):
    return pltpu.CompilerParams(dimension_semantics=("arbitrary",) * n_axes, vmem_limit_bytes=VMEM_LIMIT)


def _dot(a, b):
    return jnp.dot(a, b, preferred_element_type=F32)


def _dot_hi(a, b):
    return jnp.dot(a, b, preferred_element_type=F32, precision=HIGHEST)


def _dot_nt(a, b):
    return lax.dot_general(a, b, (((1,), (1,)), ((), ())), preferred_element_type=F32)


def _dot_tn(a, b):
    return lax.dot_general(a, b, (((0,), (0,)), ((), ())), preferred_element_type=F32)


def _silu(t):
    return t * jax.nn.sigmoid(t)


def _rms(t, axis_size):
    return t * lax.rsqrt(jnp.sum(t * t, axis=-1, keepdims=True) * (1.0 / axis_size) + EPS)


def _mod_kernel(c_ref, w_ref, b_ref, o_ref):
    o_ref[...] = _dot_hi(_silu(c_ref[...]), w_ref[...]) + b_ref[...]


def _mod_call(c_rows, w_mod, b_mod):
    depth, d, n = w_mod.shape
    rows = c_rows.shape[0]
    tn = 1536
    return pl.pallas_call(
        _mod_kernel,
        grid=(depth, n // tn),
        in_specs=[
            pl.BlockSpec((rows, d), lambda l, j: (0, 0)),
            pl.BlockSpec((None, d, tn), lambda l, j: (l, 0, j)),
            pl.BlockSpec((None, 1, tn), lambda l, j: (l, 0, j)),
        ],
        out_specs=pl.BlockSpec((None, rows, tn), lambda l, j: (l, 0, j)),
        out_shape=jax.ShapeDtypeStruct((depth, rows, n), F32),
        compiler_params=_cparams(2),
        name="mod_vectors",
    )(c_rows, w_mod, b_mod.reshape(depth, 1, n))


def _row_masks(is_lat):
    pos = lax.broadcasted_iota(jnp.int32, (TILE, 1), 0)
    row_mask = jnp.where(is_lat, CHUNK - 1, TILE - 1)
    in_row = pos & row_mask
    return jnp.where(in_row == 0, 0.0, 1.0).astype(F32), jnp.where(in_row == row_mask, 0.0, 1.0).astype(F32)


def _mask_rows(t, mask, first):
    pieces = []
    for g in range(TILE // CHUNK):
        lo, hi = g * CHUNK, (g + 1) * CHUNK
        if first:
            pieces += [t[lo:lo + 8] * mask[lo:lo + 8], t[lo + 8:hi]]
        else:
            pieces += [t[lo:hi - 8], t[hi - 8:hi] * mask[hi - 8:hi]]
    return jnp.concatenate(pieces, axis=0)


def _conv3(t, w_ref, m_prev, m_next):
    prev = _mask_rows(pltpu.roll(t, 1, 0), m_prev, True)
    nxt = _mask_rows(pltpu.roll(t, TILE - 1, 0), m_next, False)
    return prev * w_ref[0:1, :] + t * w_ref[1:2, :] + nxt * w_ref[2:3, :]


def _modulated(x, g_ref, shift, scale):
    return _rms(x, D_MODEL) * (g_ref[...] * (1.0 + scale)) + shift


def _inproj_kernel(nt_lat, x_ref, ctx_ref, mod_ref, g_ref, wqkv_ref, wz_ref, wba_ref, wsc_ref, wfn_ref,
                   dnconv_ref, scconv_ref, alog_ref, dtb_ref,
                   q_ref, k_ref, v_ref, z_ref, bg_ref, sc_ref, fn_ref):
    is_lat = pl.program_id(1) < nt_lat
    m_prev, m_next = _row_masks(is_lat)
    n_rows = x_ref.shape[0]
    hs = []
    for s in range(n_rows):
        x = jnp.where(is_lat, x_ref[s], ctx_ref[s])
        hs.append(_modulated(x, g_ref, mod_ref[s, 0:1, :], mod_ref[s, 1:2, :]).astype(BF16))
    projs = [tuple(_dot(h, w_ref[...]) for w_ref in (wqkv_ref, wz_ref, wba_ref, wsc_ref, wfn_ref)) for h in hs]
    for s in range(n_rows):
        p_qkv, p_z, ba, scp, p_fn = projs[s]
        qkv = _silu(_conv3(p_qkv, dnconv_ref, m_prev, m_next))
        for hd in range(HEADS):
            lo = hd * HEAD_DIM
            qh = qkv[:, lo:lo + HEAD_DIM]
            kh = qkv[:, KEYW + lo:KEYW + lo + HEAD_DIM]
            q_ref[s, :, lo:lo + HEAD_DIM] = (qh * lax.rsqrt(jnp.sum(qh * qh, axis=-1, keepdims=True) + EPS)
                                             * (HEAD_DIM ** -0.5))
            k_ref[s, :, lo:lo + HEAD_DIM] = kh * lax.rsqrt(jnp.sum(kh * kh, axis=-1, keepdims=True) + EPS)
        v_ref[s] = qkv[:, 2 * KEYW:]
        z_ref[s] = p_z

        sp_in = ba + dtb_ref[...]
        softplus = jnp.maximum(sp_in, 0.0) + jnp.log1p(jnp.exp(-jnp.abs(sp_in)))
        lane = lax.broadcasted_iota(jnp.int32, ba.shape, 1)
        g = -jnp.exp(alog_ref[...]) * softplus
        in_chunk = lax.broadcasted_iota(jnp.int32, (TILE, 1), 0) & (CHUNK - 1)
        pre = g
        step = 1
        while step < CHUNK:
            pre = pre + jnp.where(in_chunk >= step, pltpu.roll(pre, step, 0), 0.0)
            step *= 2
        total = jnp.concatenate(
            [jnp.broadcast_to(pre[c * CHUNK + CHUNK - 1:(c + 1) * CHUNK, :], (CHUNK, HEAD_DIM))
             for c in range(TILE // CHUNK)], axis=0)
        suf = (total - pre) + g
        gc = jnp.where(lane < N_DH + HEADS, pre, suf)
        bg_ref[s] = jnp.where(lane < N_DH, jax.nn.sigmoid(ba), gc)

        sc_ref[s] = (scp[:, :KEYW] * _conv3(scp[:, KEYW:2 * KEYW] * scp[:, 2 * KEYW:], scconv_ref, m_prev, m_next)
                     ).astype(BF16)
        fn_ref[s] = p_fn.astype(BF16)


def _const_spec(a):
    return pl.BlockSpec(a.shape, lambda b, j: (0,) * a.ndim, pipeline_mode=pl.Buffered(1))


def _mod_spec(bb, nt_lat, d):
    return pl.BlockSpec((bb, None, 6, d), lambda b, j: (b, jnp.where(j < nt_lat, 0, 1), 0, 0))


def _x_specs(bb, nt_lat, ctx_blk, d):
    return [pl.BlockSpec((bb, TILE, d), lambda b, j: (b, jnp.minimum(j, nt_lat - 1), 0)),
            pl.BlockSpec((bb, TILE, d), lambda b, j: (b, ctx_blk, 0))]


def _inproj_call(x_src, ctx_src, ctx_blk, mod_sel, n1g, wparts, dnconv, scconv, alog_row, dtb_row, nt_lat):
    bsz, _, d = x_src.shape
    nt = nt_lat + 1
    ltot = nt * TILE
    bb = ROW_BATCH if bsz % ROW_BATCH == 0 else 1
    wqkv, wz, wba, wsc, wfn = wparts
    tok = lambda w: pl.BlockSpec((bb, TILE, w), lambda b, j: (b, j, 0))
    out_w = (KEYW, KEYW, KEYW, KEYW, HEAD_DIM, KEYW, KEYW)
    consts = (n1g, wqkv, wz, wba, wsc, wfn, dnconv, scconv, alog_row, dtb_row)
    return pl.pallas_call(
        functools.partial(_inproj_kernel, nt_lat),
        grid=(bsz // bb, nt),
        in_specs=_x_specs(bb, nt_lat, ctx_blk, d) + [_mod_spec(bb, nt_lat, d)] + [_const_spec(a) for a in consts],
        out_specs=[tok(w) for w in out_w],
        out_shape=[jax.ShapeDtypeStruct((bsz, ltot, w), F32) for w in out_w[:-2]]
        + [jax.ShapeDtypeStruct((bsz, ltot, w), BF16) for w in out_w[-2:]],
        compiler_params=_cparams(2),
        name="in_proj",
    )(x_src, ctx_src, mod_sel, *consts)


def _split2(t):
    hi = t.astype(BF16)
    return hi, (t - hi.astype(F32)).astype(BF16)


def _mm3(lhs, rhs):
    m = lhs.shape[0]
    lh, ll = _split2(lhs)
    rh, rl = _split2(rhs)
    top = _dot(jnp.concatenate([lh, ll], axis=0), rh)
    return top[:m] + top[m:] + _dot(lh, rl)


def _dnprep_kernel(cps, q_ref, k_ref, v_ref, bg_ref, u_ref, w_ref, qg_ref, kd_ref, attn_ref, adec_ref):
    ri = lax.broadcasted_iota(jnp.int32, (CHUNK, PACKW), 0)
    cj = lax.broadcasted_iota(jnp.int32, (CHUNK, PACKW), 1) & (CHUNK - 1)
    eye_p = jnp.where(ri == cj, 1.0, 0.0).astype(F32)
    incl_m = ((ri >= cj), (ri <= cj))
    strict_m = ((ri > cj), (ri < cj))

    def half_masked(t, hd):
        lane = lax.broadcasted_iota(jnp.int32, t.shape, 1)
        keep = (lane < CHUNK) if hd % 2 == 0 else (lane >= CHUNK)
        return jnp.where(keep, t, jnp.zeros((), BF16))

    def blockdiag(t):
        z = jnp.zeros((CHUNK, HEAD_DIM), BF16)
        blocks = []
        for hd in range(HEADS):
            piece = half_masked(t[:, (hd // 2) * HEAD_DIM:(hd // 2 + 1) * HEAD_DIM], hd)
            blocks.append(jnp.concatenate([piece, z] if hd < 2 else [z, piece], axis=1))
        return jnp.concatenate(blocks, axis=0)

    def mm3_all(lhs_list, rhs_list):
        lsp = [_split2(l) for l in lhs_list]
        rsp = [tuple(blockdiag(p) for p in _split2(r)) for r in rhs_list]
        tops = [_dot(jnp.concatenate([lh, ll], axis=0), rh) for (lh, ll), (rh, _) in zip(lsp, rsp)]
        lows = [_dot(lh, rl) for (lh, _), (_, rl) in zip(lsp, rsp)]
        return [t[:l.shape[0]] + t[l.shape[0]:] + lo for t, lo, l in zip(tops, lows, lhs_list)]

    sizes = PREP_GROUPS if sum(PREP_GROUPS) == cps else (cps,)
    for first, size in zip(np.cumsum((0,) + sizes[:-1]), sizes):
        _dnprep_group(int(first), size, q_ref, k_ref, v_ref, bg_ref,
                      u_ref, w_ref, qg_ref, kd_ref, attn_ref, adec_ref,
                      eye_p, incl_m, strict_m, half_masked, blockdiag, mm3_all)


def _dnprep_group(first, cps, q_ref, k_ref, v_ref, bg_ref, u_ref, w_ref, qg_ref, kd_ref, attn_ref, adec_ref,
                  eye_p, incl_m, strict_m, half_masked, blockdiag, mm3_all):
    chunk_rows = [slice((first + ci) * CHUNK, (first + ci + 1) * CHUNK) for ci in range(cps)]
    qs = [q_ref[r, :] for r in chunk_rows]
    ks = [k_ref[r, :] for r in chunk_rows]
    vs = [v_ref[r, :] for r in chunk_rows]

    lo_half = lax.broadcasted_iota(jnp.int32, (CHUNK, HEAD_DIM), 1) < CHUNK
    expanded = {}
    for ci, r in enumerate(chunk_rows):
        bg = bg_ref[r, :]
        for dr in range(2):
            cols = [(jnp.broadcast_to(bg[:, N_DH + dr * HEADS + hd:N_DH + dr * HEADS + hd + 1], (CHUNK, HEAD_DIM)),
                     jnp.broadcast_to(bg[:, dr * HEADS + hd:dr * HEADS + hd + 1], (CHUNK, HEAD_DIM)))
                    for hd in range(HEADS)]
            gcol = jnp.concatenate([c[0] for c in cols], axis=1)
            beta = jnp.concatenate([c[1] for c in cols], axis=1)
            gcol_p = jnp.concatenate([jnp.where(lo_half, cols[0][0], cols[1][0]),
                                      jnp.where(lo_half, cols[2][0], cols[3][0])], axis=1)
            expanded[ci, dr] = (gcol, beta, gcol_p)

    def parts(ci, dr):
        return expanded[ci, dr]

    m1s = []
    for ci in range(cps):
        k = ks[ci]
        k_t = jnp.concatenate([k, k], axis=0).T.astype(BF16)
        zk = jnp.zeros((HEAD_DIM, HEAD_DIM), BF16)
        bd_rows = []
        for hd in range(HEADS):
            piece = half_masked(k_t[hd * HEAD_DIM:(hd + 1) * HEAD_DIM, :], hd)
            bd_rows.append(jnp.concatenate([piece, zk] if hd < 2 else [zk, piece], axis=1))
        bd_k = jnp.concatenate(bd_rows, axis=0)
        kbs = [k * parts(ci, dr)[1] for dr in range(2)]
        m1s.append(_dot(jnp.concatenate(kbs + [qs[ci]], axis=0).astype(BF16), bd_k))

    chains = [(ci, dr) for ci in range(cps) for dr in range(2)]
    decays, neg_as = [], []
    for ci, dr in chains:
        gcol_p = parts(ci, dr)[2]
        grow_p = jnp.sum(gcol_p * eye_p, axis=0, keepdims=True)
        decay = jnp.exp(jnp.where(incl_m[dr], gcol_p - grow_p, -jnp.inf))
        decays.append(decay)
        neg_as.append(jnp.where(strict_m[dr], -(m1s[ci][dr * CHUNK:(dr + 1) * CHUNK] * decay), 0.0))

    p_accs = [eye_p + a for a in neg_as]
    q_pows = mm3_all(neg_as, neg_as)
    for _ in range(4):
        boths = mm3_all([jnp.concatenate([qp, pa], axis=0) for qp, pa in zip(q_pows, p_accs)], q_pows)
        q_pows = [b[:CHUNK] for b in boths]
        p_accs = [pa + b[CHUNK:] for pa, b in zip(p_accs, boths)]
    lasts = mm3_all(p_accs, q_pows)
    p_accs = [pa + la for pa, la in zip(p_accs, lasts)]

    uws, egs = [], []
    for (ci, dr), t_inv in zip(chains, p_accs):
        gcol, beta, _ = parts(ci, dr)
        eg = jnp.exp(gcol)
        egs.append(eg)
        vb = (vs[ci] * beta).astype(BF16)
        kbe = (ks[ci] * (beta * eg)).astype(BF16)
        rhs = jnp.concatenate(
            [jnp.concatenate([vb[:, hd * HEAD_DIM:(hd + 1) * HEAD_DIM], kbe[:, hd * HEAD_DIM:(hd + 1) * HEAD_DIM]], axis=1)
             for hd in range(HEADS)], axis=0)
        res = _dot(blockdiag(t_inv.astype(BF16)), rhs)
        uws.append(jnp.concatenate(
            [res[hd * CHUNK:(hd + 1) * CHUNK, :HEAD_DIM] for hd in range(HEADS)]
            + [res[hd * CHUNK:(hd + 1) * CHUNK, HEAD_DIM:] for hd in range(HEADS)], axis=1))

    for (ci, dr), uw, eg, decay in zip(chains, uws, egs, decays):
        rows = chunk_rows[ci]
        gcol = parts(ci, dr)[0]
        g_last = gcol[CHUNK - 1:CHUNK, :] if dr == 0 else gcol[0:1, :]
        u_ref[dr, rows, :] = uw[:, :KEYW].astype(BF16)
        w_ref[dr, rows, :] = uw[:, KEYW:].astype(BF16)
        qg_ref[dr, rows, :] = (qs[ci] * eg).astype(BF16)
        kd_ref[dr, rows, :] = (ks[ci] * jnp.exp(g_last - gcol)).astype(BF16)
        attn_ref[dr, rows, :] = (m1s[ci][2 * CHUNK:] * decay).astype(BF16)
        a_dec = jnp.exp(g_last)
        for hd in range(HEADS):
            unit = dr * HEADS + hd
            adec_ref[first + ci, unit:unit + 1, :] = a_dec[:, hd * HEAD_DIM:(hd + 1) * HEAD_DIM]


def _dnprep_call(q, k, v, bg):
    bsz, ltot, _ = q.shape
    nch = ltot // CHUNK
    cps = PREP_CHUNKS if nch % PREP_CHUNKS == 0 else math.gcd(nch, PREP_CHUNKS)
    tok = lambda w: pl.BlockSpec((None, cps * CHUNK, w), lambda b, n: (b, n, 0))
    dtok = lambda w: pl.BlockSpec((None, 2, cps * CHUNK, w), lambda b, n: (b, 0, n, 0))
    return pl.pallas_call(
        functools.partial(_dnprep_kernel, cps),
        grid=(bsz, nch // cps),
        in_specs=[tok(KEYW), tok(KEYW), tok(KEYW), tok(HEAD_DIM)],
        out_specs=[dtok(KEYW), dtok(KEYW), dtok(KEYW), dtok(KEYW), dtok(PACKW),
                   pl.BlockSpec((None, cps, N_DH, HEAD_DIM), lambda b, n: (b, n, 0, 0))],
        out_shape=[jax.ShapeDtypeStruct((bsz, 2, ltot, KEYW), BF16),
                   jax.ShapeDtypeStruct((bsz, 2, ltot, KEYW), BF16),
                   jax.ShapeDtypeStruct((bsz, 2, ltot, KEYW), BF16),
                   jax.ShapeDtypeStruct((bsz, 2, ltot, KEYW), BF16),
                   jax.ShapeDtypeStruct((bsz, 2, ltot, PACKW), BF16),
                   jax.ShapeDtypeStruct((bsz, nch, N_DH, HEAD_DIM), F32)],
        compiler_params=_cparams(2),
        name="dn_prep",
    )(q, k, v, bg)


def _dnscan_kernel(uf_ref, wf_ref, qgf_ref, kdf_ref, af_ref, df_ref,
                   ub_ref, wb_ref, qgb_ref, kdb_ref, ab_ref, db_ref,
                   of_ref, ob_ref, s_ref):
    @pl.when(pl.program_id(1) == 0)
    def _():
        s_ref[...] = jnp.zeros_like(s_ref)

    per_dir = ((uf_ref, wf_ref, qgf_ref, kdf_ref, af_ref, df_ref, of_ref),
               (ub_ref, wb_ref, qgb_ref, kdb_ref, ab_ref, db_ref, ob_ref))
    units = [(bi, dr, hd) for bi in range(uf_ref.shape[0]) for dr in range(2) for hd in range(HEADS)]
    states = [s_ref[bi, dr * HEADS + hd] for bi, dr, hd in units]
    n_sub = uf_ref.shape[1] // CHUNK
    for sub in range(n_sub):
        pos = (sub, n_sub - 1 - sub)
        rows = [slice(p * CHUNK, (p + 1) * CHUNK) for p in pos]
        first = []
        for (bi, dr, hd), s in zip(units, states):
            _, w_ref, qg_ref = per_dir[dr][:3]
            lo = hd * HEAD_DIM
            first.append(_dot(jnp.concatenate([w_ref[bi, rows[dr], lo:lo + HEAD_DIM],
                                               qg_ref[bi, rows[dr], lo:lo + HEAD_DIM]], axis=0), s.astype(BF16)))
        new_states = []
        for (bi, dr, hd), both, s in zip(units, first, states):
            u_ref, _, _, kd_ref, a_ref, d_ref, o_ref = per_dir[dr]
            unit = dr * HEADS + hd
            lo = hd * HEAD_DIM
            v_new_b = (u_ref[bi, rows[dr], lo:lo + HEAD_DIM].astype(F32) - both[:CHUNK]).astype(BF16)
            o_ref[bi, rows[dr], lo:lo + HEAD_DIM] = (
                both[CHUNK:] + _dot(a_ref[bi, rows[dr], hd * CHUNK:(hd + 1) * CHUNK], v_new_b)).astype(BF16)
            new_states.append(s * d_ref[bi, pos[dr], unit:unit + 1, :]
                              + _dot_tn(kd_ref[bi, rows[dr], lo:lo + HEAD_DIM], v_new_b))
        states = new_states
    for (bi, dr, hd), s in zip(units, states):
        s_ref[bi, dr * HEADS + hd] = s


def _dnscan_call(u, w, qg, kd, attn, adec, n_lat_chunks):
    bsz, _, ltot, _ = u.shape
    nch = ltot // CHUNK
    bb = SCAN_BATCH if bsz % SCAN_BATCH == 0 else 1
    cpg = SCAN_CHUNKS if (nch % SCAN_CHUNKS == 0 and n_lat_chunks % SCAN_CHUNKS == 0) else 1
    nblk = nch // cpg
    fwd = lambda n: (n + n_lat_chunks // cpg) % nblk
    bwd = lambda n: nblk - 1 - n

    def dspec(wd, dr, order):
        return pl.BlockSpec((bb, None, cpg * CHUNK, wd), lambda b, n: (b, dr, order(n), 0))

    def aspec(order):
        return pl.BlockSpec((bb, cpg, N_DH, HEAD_DIM), lambda b, n: (b, order(n), 0, 0))

    def ospec(order):
        return pl.BlockSpec((bb, cpg * CHUNK, KEYW), lambda b, n: (b, order(n), 0))

    in_specs = []
    for dr, order in ((0, fwd), (1, bwd)):
        in_specs += [dspec(KEYW, dr, order), dspec(KEYW, dr, order), dspec(KEYW, dr, order), dspec(KEYW, dr, order),
                     dspec(PACKW, dr, order), aspec(order)]
    return pl.pallas_call(
        _dnscan_kernel,
        grid=(bsz // bb, nblk),
        in_specs=in_specs,
        out_specs=[ospec(fwd), ospec(bwd)],
        out_shape=[jax.ShapeDtypeStruct((bsz, ltot, KEYW), BF16)] * 2,
        scratch_shapes=[pltpu.VMEM((bb, N_DH, HEAD_DIM, HEAD_DIM), F32)],
        compiler_params=_cparams(2),
        name="dn_scan",
    )(u, w, qg, kd, attn, adec, u, w, qg, kd, attn, adec)


def _dft_mats(n):
    idx = np.arange(n)
    ang = 2.0 * np.pi * ((idx[:, None] * idx[None, :]) % n) / n
    return np.cos(ang), np.sin(ang)


def _fft1_kernel(n2blk, u_ref, f1_ref, tc_ref, ts_ref, br_ref, bi_ref):
    l1 = u_ref.shape[1]
    xts = [pltpu.einshape("abc->bac", u_ref[s]) for s in range(u_ref.shape[0])]
    prods = [[_dot(f1_ref[...], xt[jj]) for jj in range(n2blk)] for xt in xts]
    for s, per_n2 in enumerate(prods):
        for jj, a in enumerate(per_n2):
            ar, ai = a[:l1], a[l1:]
            tc = jnp.concatenate([tc_ref[jj]] * (KEYW // HEAD_DIM), axis=1)
            ts = jnp.concatenate([ts_ref[jj]] * (KEYW // HEAD_DIM), axis=1)
            br_ref[s, jj] = (ar * tc + ai * ts).astype(BF16)
            bi_ref[s, jj] = (ai * tc - ar * ts).astype(BF16)


def _fft2_kernel(k1blk, nlen, br_ref, bi_ref, f2_ref, wc_ref, y_ref):
    rows = k1blk * nlen
    n_rows = br_ref.shape[0]
    swap = lambda t: pltpu.einshape("abc->bac", t)
    brs = [swap(br_ref[s]) for s in range(n_rows)]
    bis = [swap(bi_ref[s]) for s in range(n_rows)]
    xs = [[_dot(f2_ref[...], jnp.concatenate([brs[s][jj], bis[s][jj]], axis=0)) for jj in range(k1blk)]
          for s in range(n_rows)]
    for s in range(n_rows):
        xr = jnp.concatenate([x[:nlen] for x in xs[s]], axis=0).astype(BF16)
        xi = jnp.concatenate([x[nlen:] for x in xs[s]], axis=0).astype(BF16)
        y = jnp.concatenate(
            [_dot(jnp.concatenate([xr[:, g * HEAD_DIM:(g + 1) * HEAD_DIM], xi[:, g * HEAD_DIM:(g + 1) * HEAD_DIM]],
                                  axis=1), wc_ref[...]) for g in range(KEYW // HEAD_DIM)], axis=1)
        y_ref[s] = swap(y.reshape(k1blk, nlen, KEYW)).astype(BF16)


def _dftctx_kernel(nlen, u_ref, f_ref, wc_ref, y_ref):
    x = _dot(f_ref[...], u_ref[...])
    xr, xi = x[:nlen].astype(BF16), x[nlen:].astype(BF16)
    for g in range(KEYW // HEAD_DIM):
        gs = slice(g * HEAD_DIM, (g + 1) * HEAD_DIM)
        y_ref[:, gs] = _dot(jnp.concatenate([xr[:, gs], xi[:, gs]], axis=1), wc_ref[...]).astype(BF16)


def _chan_mat(total_len):
    cc, sc = _dft_mats(HEAD_DIM)
    return jnp.asarray(np.concatenate([cc, sc], axis=0) / math.sqrt(total_len * HEAD_DIM), BF16)


def _fourier_lat_call(fn_cat, l_lat):
    bsz, ltot, _ = fn_cat.shape
    l1 = l_lat // CHUNK
    n2blk = 16
    k1blk = 16 if l1 % 16 == 0 else l1
    c1, s1 = _dft_mats(l1)
    f1 = jnp.asarray(np.concatenate([c1, -s1], axis=0), BF16)
    n2 = np.arange(CHUNK)[:, None, None]
    k1 = np.arange(l1)[None, :, None]
    ang = 2.0 * np.pi * (n2 * k1) / l_lat * np.ones((1, 1, HEAD_DIM))
    tc, ts = jnp.asarray(np.cos(ang), F32), jnp.asarray(np.sin(ang), F32)
    c2, s2 = _dft_mats(CHUNK)
    f2 = jnp.asarray(np.block([[c2, s2], [-s2, c2]]), BF16)
    wc = _chan_mat(l_lat)

    u_view = fn_cat.reshape(bsz, ltot // CHUNK, CHUNK, KEYW)
    bb = ROW_BATCH if bsz % ROW_BATCH == 0 else 1
    mid_spec = pl.BlockSpec((bb, n2blk, l1, KEYW), lambda b, i: (b, i, 0, 0))
    br, bi = pl.pallas_call(
        functools.partial(_fft1_kernel, n2blk),
        grid=(bsz // bb, CHUNK // n2blk),
        in_specs=[pl.BlockSpec((bb, l1, n2blk, KEYW), lambda b, i: (b, 0, i, 0)),
                  pl.BlockSpec(f1.shape, lambda b, i: (0, 0)),
                  pl.BlockSpec((n2blk, l1, HEAD_DIM), lambda b, i: (i, 0, 0)),
                  pl.BlockSpec((n2blk, l1, HEAD_DIM), lambda b, i: (i, 0, 0))],
        out_specs=[mid_spec, mid_spec],
        out_shape=[jax.ShapeDtypeStruct((bsz, CHUNK, l1, KEYW), BF16)] * 2,
        compiler_params=_cparams(2),
        name="fourier_stage1",
    )(u_view, f1, tc, ts)

    rspec = pl.BlockSpec((bb, CHUNK, k1blk, KEYW), lambda b, i: (b, 0, i, 0))
    y = pl.pallas_call(
        functools.partial(_fft2_kernel, k1blk, CHUNK),
        grid=(bsz // bb, l1 // k1blk),
        in_specs=[rspec, rspec,
                  pl.BlockSpec(f2.shape, lambda b, i: (0, 0)),
                  pl.BlockSpec(wc.shape, lambda b, i: (0, 0))],
        out_specs=rspec,
        out_shape=jax.ShapeDtypeStruct((bsz, CHUNK, l1, KEYW), BF16),
        compiler_params=_cparams(2),
        name="fourier_stage2",
    )(br, bi, f2, wc)
    return y.reshape(bsz, l_lat, KEYW)


def _fourier_ctx_call(fn_cat, l_lat):
    bsz, ltot, _ = fn_cat.shape
    lc = ltot - l_lat
    c, s = _dft_mats(lc)
    f = jnp.asarray(np.concatenate([c, -s], axis=0), BF16)
    wc = _chan_mat(lc)
    return pl.pallas_call(
        functools.partial(_dftctx_kernel, lc),
        grid=(bsz,),
        in_specs=[pl.BlockSpec((None, lc, KEYW), lambda b: (b, l_lat // lc, 0)),
                  pl.BlockSpec(f.shape, lambda b: (0, 0)),
                  pl.BlockSpec(wc.shape, lambda b: (0, 0))],
        out_specs=pl.BlockSpec((None, lc, KEYW), lambda b: (b, 0, 0)),
        out_shape=jax.ShapeDtypeStruct((bsz, lc, KEYW), BF16),
        compiler_params=_cparams(1),
        name="fourier_ctx",
    )(fn_cat, f, wc)


def _merge_kernel(nt_lat, x_ref, ctx_ref, mod_ref, of_ref, ob_ref, z_ref, sc_ref, fl_ref, fc_ref,
                  g_ref, wgate_ref, bgate_ref, og_ref, wdn_ref, wsc_ref, wfn_ref, wo_ref, o_ref):
    is_lat = pl.program_id(1) < nt_lat
    n_rows = x_ref.shape[0]
    xs = [jnp.where(is_lat, x_ref[s], ctx_ref[s]) for s in range(n_rows)]
    hs = [_modulated(xs[s], g_ref, mod_ref[s, 0:1, :], mod_ref[s, 1:2, :]).astype(BF16) for s in range(n_rows)]
    logits = [_dot(h, wgate_ref[...]) for h in hs]
    branch_in = []
    for s in range(n_rows):
        o = of_ref[s].astype(F32) + ob_ref[s].astype(F32)
        z = z_ref[s]
        dn_parts = []
        for hd in range(HEADS):
            sl = slice(hd * HEAD_DIM, (hd + 1) * HEAD_DIM)
            dn_parts.append(_rms(o[:, sl], HEAD_DIM) * og_ref[...] * _silu(z[:, sl]))
        fn = jnp.where(is_lat, fl_ref[s], fc_ref[s])
        branch_in.append((jnp.concatenate(dn_parts, axis=1).astype(BF16), sc_ref[s], fn))
    ys = [(_dot(dn, wdn_ref[...]), _dot(sc, wsc_ref[...]), _dot(fn, wfn_ref[...])) for dn, sc, fn in branch_in]
    merged = []
    for s in range(n_rows):
        gates = jax.nn.sigmoid(logits[s] + bgate_ref[...])
        y_dn, y_sc, y_fn = ys[s]
        merged.append((gates[:, :D_MODEL] * y_dn + gates[:, D_MODEL:2 * D_MODEL] * y_sc
                       + gates[:, 2 * D_MODEL:] * y_fn).astype(BF16))
    mixes = [_dot(m, wo_ref[...]) for m in merged]
    for s in range(n_rows):
        o_ref[s] = xs[s] + mod_ref[s, 2:3, :] * mixes[s]


def _merge_call(x_src, ctx_src, ctx_blk, mod_sel, n1g, o_f, o_b, z, sc, fn_lat, fn_ctx, wgate, bgate, og, wdn, wsc,
                wfn, wo, nt_lat, nt_run):
    bsz, _, d = x_src.shape
    ltot = (nt_lat + 1) * TILE
    bb = ROW_BATCH if bsz % ROW_BATCH == 0 else 1
    tok = lambda w: pl.BlockSpec((bb, TILE, w), lambda b, j: (b, j, 0))
    consts = (n1g, wgate, bgate, og, wdn, wsc, wfn, wo)
    return pl.pallas_call(
        functools.partial(_merge_kernel, nt_lat),
        grid=(bsz // bb, nt_run),
        in_specs=_x_specs(bb, nt_lat, ctx_blk, d) + [_mod_spec(bb, nt_lat, d),
                  tok(KEYW), tok(KEYW), tok(KEYW), tok(KEYW),
                  pl.BlockSpec((bb, TILE, KEYW), lambda b, j: (b, jnp.minimum(j, nt_lat - 1), 0)),
                  pl.BlockSpec((bb, TILE, KEYW), lambda b, j: (b, 0, 0))] + [_const_spec(a) for a in consts],
        out_specs=tok(d),
        out_shape=jax.ShapeDtypeStruct((bsz, ltot, d), F32),
        compiler_params=_cparams(2),
        name="branch_merge",
    )(x_src, ctx_src, mod_sel, o_f, o_b, z, sc, fn_lat, fn_ctx, *consts)


def _ffn_kernel(final, x_ref, mod_ref, g_ref, win_ref, wout_ref, fg_ref, o_ref):
    n_rows = x_ref.shape[0]
    hs = [_modulated(x_ref[s], g_ref, mod_ref[s, 3:4, :], mod_ref[s, 4:5, :]).astype(BF16) for s in range(n_rows)]
    abs_ = [_dot(h, win_ref[...]) for h in hs]
    acts = [(_silu(ab[:, :D_FF]) * ab[:, D_FF:]).astype(BF16) for ab in abs_]
    downs = [_dot(act, wout_ref[...]) for act in acts]
    for s in range(n_rows):
        y = x_ref[s] + mod_ref[s, 5:6, :] * downs[s]
        if final:
            y = _rms(y, D_MODEL) * fg_ref[...]
        o_ref[s] = y


def _ffn_call(x_cat, mod_sel, n2g, win, wout, fg, nt_lat, nt_run, final):
    bsz, ltot, d = x_cat.shape
    out_len = nt_run * TILE if final else ltot
    bb = ROW_BATCH if bsz % ROW_BATCH == 0 else 1
    tok = pl.BlockSpec((bb, TILE, d), lambda b, j: (b, j, 0))
    consts = (n2g, win, wout, fg)
    return pl.pallas_call(
        functools.partial(_ffn_kernel, final),
        grid=(bsz // bb, nt_run),
        in_specs=[tok, _mod_spec(bb, nt_lat, d)] + [_const_spec(a) for a in consts],
        out_specs=tok,
        out_shape=jax.ShapeDtypeStruct((bsz, out_len, d), F32),
        compiler_params=_cparams(2),
        name="swiglu_final" if final else "swiglu",
    )(x_cat, mod_sel, *consts)


def kernel(x, c, ctx, c_ctx, w_mod, b_mod, norm1_g, w_in, dn_conv_w, dn_a_log, dn_dt_bias, dn_onorm_g, w_dn_out,
           sc_conv_w, w_sc_out, w_fn_out, w_gate, b_gate, w_o, norm2_g, w_ffn_in, w_ffn_out, final_g):
    bsz, l_lat, d = x.shape
    lc = ctx.shape[1]
    assert d == D_MODEL and lc == TILE and l_lat % TILE == 0 and w_ffn_in.shape[-1] == 2 * D_FF
    depth = w_mod.shape[0]
    nt_lat = l_lat // TILE
    nt_all = nt_lat + 1
    n_lat_chunks = l_lat // CHUNK

    rows = ((bsz + 1 + 7) // 8) * 8
    c_rows = jnp.zeros((rows, d), F32).at[:bsz].set(c).at[bsz].set(c_ctx)
    mod = _mod_call(c_rows, w_mod, b_mod).reshape(depth, rows, 6, d)
    mod_sel = jnp.stack([mod[:, :bsz], jnp.broadcast_to(mod[:, bsz:bsz + 1], (depth, bsz, 6, d))], axis=2)

    x_src, ctx_src, ctx_blk = x, ctx, 0
    row = lambda a: a.reshape(1, -1)
    lane_pad = lambda a: jnp.zeros((1, HEAD_DIM), F32).at[0, N_DH:2 * N_DH].set(a.reshape(-1))
    for layer in range(depth):
        last = layer == depth - 1
        wl = w_in[layer]
        qkv_w = 3 * KEYW
        o_z, o_ba, o_sc, o_fn = qkv_w, qkv_w + KEYW, qkv_w + KEYW + 2 * N_DH, qkv_w + KEYW + 2 * N_DH + 3 * KEYW
        wba = jnp.zeros((d, HEAD_DIM), F32).at[:, :2 * N_DH].set(wl[:, o_ba:o_sc])
        wparts = tuple(a.astype(BF16) for a in (wl[:, :o_z], wl[:, o_z:o_ba], wba, wl[:, o_sc:o_fn], wl[:, o_fn:]))
        q, k, v, z, bg, sc, fn = _inproj_call(
            x_src, ctx_src, ctx_blk, mod_sel[layer], row(norm1_g[layer]), wparts, dn_conv_w[layer], sc_conv_w[layer],
            lane_pad(dn_a_log[layer]), lane_pad(dn_dt_bias[layer]), nt_lat)
        u, w, qg, kd, attn, adec = _dnprep_call(q, k, v, bg)
        o_f, o_b = _dnscan_call(u, w, qg, kd, attn, adec, n_lat_chunks)
        fn_lat = _fourier_lat_call(fn, l_lat)
        nt_run = nt_lat if last else nt_all
        fn_ctx = jnp.zeros((bsz, lc, KEYW), BF16) if last else _fourier_ctx_call(fn, l_lat)
        x_cat = _merge_call(
            x_src, ctx_src, ctx_blk, mod_sel[layer], row(norm1_g[layer]), o_f, o_b, z, sc, fn_lat, fn_ctx,
            w_gate[layer].astype(BF16), row(b_gate[layer]), row(dn_onorm_g[layer]),
            w_dn_out[layer].astype(BF16), w_sc_out[layer].astype(BF16), w_fn_out[layer].astype(BF16),
            w_o[layer].astype(BF16), nt_lat, nt_run)
        x_cat = _ffn_call(x_cat, mod_sel[layer], row(norm2_g[layer]), w_ffn_in[layer].astype(BF16),
                          w_ffn_out[layer].astype(BF16), row(final_g), nt_lat, nt_run, last)
        x_src, ctx_src, ctx_blk = x_cat, x_cat, nt_lat
    return x_cat
```

```python
import functools
import math

import numpy as np
import jax
import jax.numpy as jnp
from jax import lax
from jax.experimental import pallas as pl
from jax.experimental.pallas import tpu as pltpu

F32 = jnp.float32
BF16 = jnp.bfloat16
HIGHEST = lax.Precision.HIGHEST

EPS = 1e-6
D_MODEL = 1024
HEADS = 4
HEAD_DIM = 128
KEYW = HEADS * HEAD_DIM
CHUNK = 64
D_FF = 2816
TILE = 256
N_DH = 2 * HEADS
PACKW = HEADS * CHUNK
PREP_CHUNKS = 22
PREP_GROUPS = (2, 5, 5, 5, 5)
SCAN_BATCH = 8
SCAN_CHUNKS = 4
ROW_BATCH = 2
VMEM_LIMIT = 56 * 1024 * 1024


def _cparams(n_axes):
    return pltpu.CompilerParams(dimension_semantics=("arbitrary",) * n_axes, vmem_limit_bytes=VMEM_LIMIT)


def _dot(a, b):
    return jnp.dot(a, b, preferred_element_type=F32)


def _dot_hi(a, b):
    return jnp.dot(a, b, preferred_element_type=F32, precision=HIGHEST)


def _dot_tn(a, b):
    return lax.dot_general(a, b, (((0,), (0,)), ((), ())), preferred_element_type=F32)


def _silu(t):
    return t * jax.nn.sigmoid(t)


def _rms(t, axis_size):
    return t * lax.rsqrt(jnp.sum(t * t, axis=-1, keepdims=True) * (1.0 / axis_size) + EPS)


def _mod_kernel(c_ref, w_ref, b_ref, o_ref):
    o_ref[...] = _dot_hi(_silu(c_ref[...]), w_ref[...]) + b_ref[...]


def _mod_call(c_rows, w_mod, b_mod):
    depth, d, n = w_mod.shape
    rows = c_rows.shape[0]
    tn = 1536
    return pl.pallas_call(
        _mod_kernel,
        grid=(depth, n // tn),
        in_specs=[
            pl.BlockSpec((rows, d), lambda l, j: (0, 0)),
            pl.BlockSpec((None, d, tn), lambda l, j: (l, 0, j)),
            pl.BlockSpec((None, 1, tn), lambda l, j: (l, 0, j)),
        ],
        out_specs=pl.BlockSpec((None, rows, tn), lambda l, j: (l, 0, j)),
        out_shape=jax.ShapeDtypeStruct((depth, rows, n), F32),
        compiler_params=_cparams(2),
        name="mod_vectors",
    )(c_rows, w_mod, b_mod.reshape(depth, 1, n))


def _row_masks(is_lat):
    pos = lax.broadcasted_iota(jnp.int32, (TILE, 1), 0)
    row_mask = jnp.where(is_lat, CHUNK - 1, TILE - 1)
    in_row = pos & row_mask
    return jnp.where(in_row == 0, 0.0, 1.0).astype(F32), jnp.where(in_row == row_mask, 0.0, 1.0).astype(F32)


def _mask_rows(t, mask, first):
    pieces = []
    for g in range(TILE // CHUNK):
        lo, hi = g * CHUNK, (g + 1) * CHUNK
        if first:
            pieces += [t[lo:lo + 8] * mask[lo:lo + 8], t[lo + 8:hi]]
        else:
            pieces += [t[lo:hi - 8], t[hi - 8:hi] * mask[hi - 8:hi]]
    return jnp.concatenate(pieces, axis=0)


def _conv3(t, w_ref, m_prev, m_next):
    prev = _mask_rows(pltpu.roll(t, 1, 0), m_prev, True)
    nxt = _mask_rows(pltpu.roll(t, TILE - 1, 0), m_next, False)
    return prev * w_ref[0:1, :] + t * w_ref[1:2, :] + nxt * w_ref[2:3, :]


def _modulated(x, g_ref, shift, scale):
    return _rms(x, D_MODEL) * (g_ref[...] * (1.0 + scale)) + shift


def _inproj_kernel(nt_lat, x_ref, ctx_ref, mod_ref, g_ref, wqkv_ref, wz_ref, wba_ref, wsc_ref, wfn_ref,
                   dnconv_ref, scconv_ref, alog_ref, dtb_ref,
                   q_ref, k_ref, v_ref, z_ref, bg_ref, sc_ref, fn_ref):
    is_lat = pl.program_id(1) < nt_lat
    m_prev, m_next = _row_masks(is_lat)
    n_rows = x_ref.shape[0]
    hs = []
    for s in range(n_rows):
        x = jnp.where(is_lat, x_ref[s], ctx_ref[s])
        hs.append(_modulated(x, g_ref, mod_ref[s, 0:1, :], mod_ref[s, 1:2, :]).astype(BF16))
    projs = [tuple(_dot(h, w_ref[...]) for w_ref in (wqkv_ref, wz_ref, wba_ref, wsc_ref, wfn_ref)) for h in hs]
    for s in range(n_rows):
        p_qkv, p_z, ba, scp, p_fn = projs[s]
        qkv = _silu(_conv3(p_qkv, dnconv_ref, m_prev, m_next))
        for hd in range(HEADS):
            lo = hd * HEAD_DIM
            qh = qkv[:, lo:lo + HEAD_DIM]
            kh = qkv[:, KEYW + lo:KEYW + lo + HEAD_DIM]
            q_ref[s, :, lo:lo + HEAD_DIM] = (qh * lax.rsqrt(jnp.sum(qh * qh, axis=-1, keepdims=True) + EPS)
                                             * (HEAD_DIM ** -0.5))
            k_ref[s, :, lo:lo + HEAD_DIM] = kh * lax.rsqrt(jnp.sum(kh * kh, axis=-1, keepdims=True) + EPS)
        v_ref[s] = qkv[:, 2 * KEYW:]
        z_ref[s] = p_z

        sp_in = ba + dtb_ref[...]
        softplus = jnp.maximum(sp_in, 0.0) + jnp.log1p(jnp.exp(-jnp.abs(sp_in)))
        lane = lax.broadcasted_iota(jnp.int32, ba.shape, 1)
        g = -jnp.exp(alog_ref[...]) * softplus
        in_chunk = lax.broadcasted_iota(jnp.int32, (TILE, 1), 0) & (CHUNK - 1)
        pre = g
        step = 1
        while step < CHUNK:
            pre = pre + jnp.where(in_chunk >= step, pltpu.roll(pre, step, 0), 0.0)
            step *= 2
        total = jnp.concatenate(
            [jnp.broadcast_to(pre[c * CHUNK + CHUNK - 1:(c + 1) * CHUNK, :], (CHUNK, HEAD_DIM))
             for c in range(TILE // CHUNK)], axis=0)
        suf = (total - pre) + g
        gc = jnp.where(lane < N_DH + HEADS, pre, suf)
        bg_ref[s] = jnp.where(lane < N_DH, jax.nn.sigmoid(ba), gc)

        sc_ref[s] = (scp[:, :KEYW] * _conv3(scp[:, KEYW:2 * KEYW] * scp[:, 2 * KEYW:], scconv_ref, m_prev, m_next)
                     ).astype(BF16)
        fn_ref[s] = p_fn.astype(BF16)


def _const_spec(a):
    return pl.BlockSpec(a.shape, lambda b, j: (0,) * a.ndim, pipeline_mode=pl.Buffered(1))


def _mod_spec(bb, nt_lat, d):
    return pl.BlockSpec((bb, None, 6, d), lambda b, j: (b, jnp.where(j < nt_lat, 0, 1), 0, 0))


def _x_specs(bb, nt_lat, ctx_blk, d):
    return [pl.BlockSpec((bb, TILE, d), lambda b, j: (b, jnp.minimum(j, nt_lat - 1), 0)),
            pl.BlockSpec((bb, TILE, d), lambda b, j: (b, ctx_blk, 0))]


def _inproj_call(x_src, ctx_src, ctx_blk, mod_sel, n1g, wparts, dnconv, scconv, alog_row, dtb_row, nt_lat):
    bsz, _, d = x_src.shape
    nt = nt_lat + 1
    ltot = nt * TILE
    bb = ROW_BATCH if bsz % ROW_BATCH == 0 else 1
    wqkv, wz, wba, wsc, wfn = wparts
    tok = lambda w: pl.BlockSpec((bb, TILE, w), lambda b, j: (b, j, 0))
    out_w = (KEYW, KEYW, KEYW, KEYW, HEAD_DIM, KEYW, KEYW)
    consts = (n1g, wqkv, wz, wba, wsc, wfn, dnconv, scconv, alog_row, dtb_row)
    return pl.pallas_call(
        functools.partial(_inproj_kernel, nt_lat),
        grid=(bsz // bb, nt),
        in_specs=_x_specs(bb, nt_lat, ctx_blk, d) + [_mod_spec(bb, nt_lat, d)] + [_const_spec(a) for a in consts],
        out_specs=[tok(w) for w in out_w],
        out_shape=[jax.ShapeDtypeStruct((bsz, ltot, w), F32) for w in out_w[:-2]]
        + [jax.ShapeDtypeStruct((bsz, ltot, w), BF16) for w in out_w[-2:]],
        compiler_params=_cparams(2),
        name="in_proj",
    )(x_src, ctx_src, mod_sel, *consts)


def _split2(t):
    hi = t.astype(BF16)
    return hi, (t - hi.astype(F32)).astype(BF16)


def _dnprep_kernel(cps, q_ref, k_ref, v_ref, bg_ref, u_ref, w_ref, qg_ref, kd_ref, attn_ref, adec_ref):
    ri = lax.broadcasted_iota(jnp.int32, (CHUNK, PACKW), 0)
    cj = lax.broadcasted_iota(jnp.int32, (CHUNK, PACKW), 1) & (CHUNK - 1)
    eye_p = jnp.where(ri == cj, 1.0, 0.0).astype(F32)
    incl_m = ((ri >= cj), (ri <= cj))
    strict_m = ((ri > cj), (ri < cj))

    def half_masked(t, hd):
        lane = lax.broadcasted_iota(jnp.int32, t.shape, 1)
        keep = (lane < CHUNK) if hd % 2 == 0 else (lane >= CHUNK)
        return jnp.where(keep, t, jnp.zeros((), BF16))

    def blockdiag(t):
        z = jnp.zeros((CHUNK, HEAD_DIM), BF16)
        blocks = []
        for hd in range(HEADS):
            piece = half_masked(t[:, (hd // 2) * HEAD_DIM:(hd // 2 + 1) * HEAD_DIM], hd)
            blocks.append(jnp.concatenate([piece, z] if hd < 2 else [z, piece], axis=1))
        return jnp.concatenate(blocks, axis=0)

    def mm3_all(lhs_list, rhs_list):
        lsp = [_split2(l) for l in lhs_list]
        rsp = [tuple(blockdiag(p) for p in _split2(r)) for r in rhs_list]
        tops = [_dot(jnp.concatenate([lh, ll], axis=0), rh) for (lh, ll), (rh, _) in zip(lsp, rsp)]
        lows = [_dot(lh, rl) for (lh, _), (_, rl) in zip(lsp, rsp)]
        return [t[:l.shape[0]] + t[l.shape[0]:] + lo for t, lo, l in zip(tops, lows, lhs_list)]

    sizes = PREP_GROUPS if sum(PREP_GROUPS) == cps else (cps,)
    for first, size in zip(np.cumsum((0,) + sizes[:-1]), sizes):
        _dnprep_group(int(first), size, q_ref, k_ref, v_ref, bg_ref,
                      u_ref, w_ref, qg_ref, kd_ref, attn_ref, adec_ref,
                      eye_p, incl_m, strict_m, half_masked, blockdiag, mm3_all)


def _dnprep_group(first, cps, q_ref, k_ref, v_ref, bg_ref, u_ref, w_ref, qg_ref, kd_ref, attn_ref, adec_ref,
                  eye_p, incl_m, strict_m, half_masked, blockdiag, mm3_all):
    chunk_rows = [slice((first + ci) * CHUNK, (first + ci + 1) * CHUNK) for ci in range(cps)]
    qs = [q_ref[r, :] for r in chunk_rows]
    ks = [k_ref[r, :] for r in chunk_rows]
    vs = [v_ref[r, :] for r in chunk_rows]

    lo_half = lax.broadcasted_iota(jnp.int32, (CHUNK, HEAD_DIM), 1) < CHUNK
    expanded = {}
    for ci, r in enumerate(chunk_rows):
        bg = bg_ref[r, :]
        for dr in range(2):
            cols = [(jnp.broadcast_to(bg[:, N_DH + dr * HEADS + hd:N_DH + dr * HEADS + hd + 1], (CHUNK, HEAD_DIM)),
                     jnp.broadcast_to(bg[:, dr * HEADS + hd:dr * HEADS + hd + 1], (CHUNK, HEAD_DIM)))
                    for hd in range(HEADS)]
            gcol = jnp.concatenate([c[0] for c in cols], axis=1)
            beta = jnp.concatenate([c[1] for c in cols], axis=1)
            gcol_p = jnp.concatenate([jnp.where(lo_half, cols[0][0], cols[1][0]),
                                      jnp.where(lo_half, cols[2][0], cols[3][0])], axis=1)
            expanded[ci, dr] = (gcol, beta, gcol_p)

    def parts(ci, dr):
        return expanded[ci, dr]

    m1s = []
    for ci in range(cps):
        k = ks[ci]
        k_t = jnp.concatenate([k, k], axis=0).T.astype(BF16)
        zk = jnp.zeros((HEAD_DIM, HEAD_DIM), BF16)
        bd_rows = []
        for hd in range(HEADS):
            piece = half_masked(k_t[hd * HEAD_DIM:(hd + 1) * HEAD_DIM, :], hd)
            bd_rows.append(jnp.concatenate([piece, zk] if hd < 2 else [zk, piece], axis=1))
        bd_k = jnp.concatenate(bd_rows, axis=0)
        kbs = [k * parts(ci, dr)[1] for dr in range(2)]
        m1s.append(_dot(jnp.concatenate(kbs + [qs[ci]], axis=0).astype(BF16), bd_k))

    chains = [(ci, dr) for ci in range(cps) for dr in range(2)]
    decays, neg_as = [], []
    for ci, dr in chains:
        gcol_p = parts(ci, dr)[2]
        grow_p = jnp.sum(gcol_p * eye_p, axis=0, keepdims=True)
        decay = jnp.exp(jnp.where(incl_m[dr], gcol_p - grow_p, -jnp.inf))
        decays.append(decay)
        neg_as.append(jnp.where(strict_m[dr], -(m1s[ci][dr * CHUNK:(dr + 1) * CHUNK] * decay), 0.0))

    p_accs = [eye_p + a for a in neg_as]
    q_pows = mm3_all(neg_as, neg_as)
    for _ in range(4):
        boths = mm3_all([jnp.concatenate([qp, pa], axis=0) for qp, pa in zip(q_pows, p_accs)], q_pows)
        q_pows = [b[:CHUNK] for b in boths]
        p_accs = [pa + b[CHUNK:] for pa, b in zip(p_accs, boths)]
    lasts = mm3_all(p_accs, q_pows)
    p_accs = [pa + la for pa, la in zip(p_accs, lasts)]

    uws, egs = [], []
    for (ci, dr), t_inv in zip(chains, p_accs):
        gcol, beta, _ = parts(ci, dr)
        eg = jnp.exp(gcol)
        egs.append(eg)
        vb = (vs[ci] * beta).astype(BF16)
        kbe = (ks[ci] * (beta * eg)).astype(BF16)
        rhs = jnp.concatenate(
            [jnp.concatenate([vb[:, hd * HEAD_DIM:(hd + 1) * HEAD_DIM], kbe[:, hd * HEAD_DIM:(hd + 1) * HEAD_DIM]], axis=1)
             for hd in range(HEADS)], axis=0)
        res = _dot(blockdiag(t_inv.astype(BF16)), rhs)
        uws.append(jnp.concatenate(
            [res[hd * CHUNK:(hd + 1) * CHUNK, :HEAD_DIM] for hd in range(HEADS)]
            + [res[hd * CHUNK:(hd + 1) * CHUNK, HEAD_DIM:] for hd in range(HEADS)], axis=1))

    for (ci, dr), uw, eg, decay in zip(chains, uws, egs, decays):
        rows = chunk_rows[ci]
        gcol = parts(ci, dr)[0]
        g_last = gcol[CHUNK - 1:CHUNK, :] if dr == 0 else gcol[0:1, :]
        u_ref[dr, rows, :] = uw[:, :KEYW].astype(BF16)
        w_ref[dr, rows, :] = uw[:, KEYW:].astype(BF16)
        qg_ref[dr, rows, :] = (qs[ci] * eg).astype(BF16)
        kd_ref[dr, rows, :] = (ks[ci] * jnp.exp(g_last - gcol)).astype(BF16)
        attn_ref[dr, rows, :] = (m1s[ci][2 * CHUNK:] * decay).astype(BF16)
        a_dec = jnp.exp(g_last)
        for hd in range(HEADS):
            unit = dr * HEADS + hd
            adec_ref[first + ci, unit:unit + 1, :] = a_dec[:, hd * HEAD_DIM:(hd + 1) * HEAD_DIM]


def _dnprep_call(q, k, v, bg):
    bsz, ltot, _ = q.shape
    nch = ltot // CHUNK
    cps = PREP_CHUNKS if nch % PREP_CHUNKS == 0 else math.gcd(nch, PREP_CHUNKS)
    tok = lambda w: pl.BlockSpec((None, cps * CHUNK, w), lambda b, n: (b, n, 0))
    dtok = lambda w: pl.BlockSpec((None, 2, cps * CHUNK, w), lambda b, n: (b, 0, n, 0))
    return pl.pallas_call(
        functools.partial(_dnprep_kernel, cps),
        grid=(bsz, nch // cps),
        in_specs=[tok(KEYW), tok(KEYW), tok(KEYW), tok(HEAD_DIM)],
        out_specs=[dtok(KEYW), dtok(KEYW), dtok(KEYW), dtok(KEYW), dtok(PACKW),
                   pl.BlockSpec((None, cps, N_DH, HEAD_DIM), lambda b, n: (b, n, 0, 0))],
        out_shape=[jax.ShapeDtypeStruct((bsz, 2, ltot, KEYW), BF16),
                   jax.ShapeDtypeStruct((bsz, 2, ltot, KEYW), BF16),
                   jax.ShapeDtypeStruct((bsz, 2, ltot, KEYW), BF16),
                   jax.ShapeDtypeStruct((bsz, 2, ltot, KEYW), BF16),
                   jax.ShapeDtypeStruct((bsz, 2, ltot, PACKW), BF16),
                   jax.ShapeDtypeStruct((bsz, nch, N_DH, HEAD_DIM), F32)],
        compiler_params=_cparams(2),
        name="dn_prep",
    )(q, k, v, bg)


def _dnscan_kernel(uf_ref, wf_ref, qgf_ref, kdf_ref, af_ref, df_ref,
                   ub_ref, wb_ref, qgb_ref, kdb_ref, ab_ref, db_ref,
                   of_ref, ob_ref, s_ref):
    @pl.when(pl.program_id(1) == 0)
    def _():
        s_ref[...] = jnp.zeros_like(s_ref)

    per_dir = ((uf_ref, wf_ref, qgf_ref, kdf_ref, af_ref, df_ref, of_ref),
               (ub_ref, wb_ref, qgb_ref, kdb_ref, ab_ref, db_ref, ob_ref))
    units = [(bi, dr, hd) for bi in range(uf_ref.shape[0]) for dr in range(2) for hd in range(HEADS)]
    states = [s_ref[bi, dr * HEADS + hd] for bi, dr, hd in units]
    n_sub = uf_ref.shape[1] // CHUNK
    for sub in range(n_sub):
        pos = (sub, n_sub - 1 - sub)
        rows = [slice(p * CHUNK, (p + 1) * CHUNK) for p in pos]
        first = []
        for (bi, dr, hd), s in zip(units, states):
            _, w_ref, qg_ref = per_dir[dr][:3]
            lo = hd * HEAD_DIM
            first.append(_dot(jnp.concatenate([w_ref[bi, rows[dr], lo:lo + HEAD_DIM],
                                               qg_ref[bi, rows[dr], lo:lo + HEAD_DIM]], axis=0), s.astype(BF16)))
        new_states = []
        for (bi, dr, hd), both, s in zip(units, first, states):
            u_ref, _, _, kd_ref, a_ref, d_ref, o_ref = per_dir[dr]
            unit = dr * HEADS + hd
            lo = hd * HEAD_DIM
            v_new_b = (u_ref[bi, rows[dr], lo:lo + HEAD_DIM].astype(F32) - both[:CHUNK]).astype(BF16)
            o_ref[bi, rows[dr], lo:lo + HEAD_DIM] = (
                both[CHUNK:] + _dot(a_ref[bi, rows[dr], hd * CHUNK:(hd + 1) * CHUNK], v_new_b)).astype(BF16)
            new_states.append(s * d_ref[bi, pos[dr], unit:unit + 1, :]
                              + _dot_tn(kd_ref[bi, rows[dr], lo:lo + HEAD_DIM], v_new_b))
        states = new_states
    for (bi, dr, hd), s in zip(units, states):
        s_ref[bi, dr * HEADS + hd] = s


def _dnscan_call(u, w, qg, kd, attn, adec, n_lat_chunks):
    bsz, _, ltot, _ = u.shape
    nch = ltot // CHUNK
    bb = SCAN_BATCH if bsz % SCAN_BATCH == 0 else 1
    cpg = SCAN_CHUNKS if (nch % SCAN_CHUNKS == 0 and n_lat_chunks % SCAN_CHUNKS == 0) else 1
    nblk = nch // cpg
    fwd = lambda n: (n + n_lat_chunks // cpg) % nblk
    bwd = lambda n: nblk - 1 - n

    def dspec(wd, dr, order):
        return pl.BlockSpec((bb, None, cpg * CHUNK, wd), lambda b, n: (b, dr, order(n), 0))

    def aspec(order):
        return pl.BlockSpec((bb, cpg, N_DH, HEAD_DIM), lambda b, n: (b, order(n), 0, 0))

    def ospec(order):
        return pl.BlockSpec((bb, cpg * CHUNK, KEYW), lambda b, n: (b, order(n), 0))

    in_specs = []
    for dr, order in ((0, fwd), (1, bwd)):
        in_specs += [dspec(KEYW, dr, order), dspec(KEYW, dr, order), dspec(KEYW, dr, order), dspec(KEYW, dr, order),
                     dspec(PACKW, dr, order), aspec(order)]
    return pl.pallas_call(
        _dnscan_kernel,
        grid=(bsz // bb, nblk),
        in_specs=in_specs,
        out_specs=[ospec(fwd), ospec(bwd)],
        out_shape=[jax.ShapeDtypeStruct((bsz, ltot, KEYW), BF16)] * 2,
        scratch_shapes=[pltpu.VMEM((bb, N_DH, HEAD_DIM, HEAD_DIM), F32)],
        compiler_params=_cparams(2),
        name="dn_scan",
    )(u, w, qg, kd, attn, adec, u, w, qg, kd, attn, adec)


def _dft_mats(n):
    idx = np.arange(n)
    ang = 2.0 * np.pi * ((idx[:, None] * idx[None, :]) % n) / n
    return np.cos(ang), np.sin(ang)


def _fft1_kernel(n2blk, u_ref, f1_ref, tc_ref, ts_ref, br_ref, bi_ref):
    l1 = u_ref.shape[1]
    xts = [pltpu.einshape("abc->bac", u_ref[s]) for s in range(u_ref.shape[0])]
    prods = [[_dot(f1_ref[...], xt[jj]) for jj in range(n2blk)] for xt in xts]
    for s, per_n2 in enumerate(prods):
        for jj, a in enumerate(per_n2):
            ar, ai = a[:l1], a[l1:]
            tc = jnp.concatenate([tc_ref[jj]] * (KEYW // HEAD_DIM), axis=1)
            ts = jnp.concatenate([ts_ref[jj]] * (KEYW // HEAD_DIM), axis=1)
            br_ref[s, jj] = (ar * tc + ai * ts).astype(BF16)
            bi_ref[s, jj] = (ai * tc - ar * ts).astype(BF16)


def _fft2_kernel(k1blk, nlen, br_ref, bi_ref, f2_ref, wc_ref, y_ref):
    rows = k1blk * nlen
    n_rows = br_ref.shape[0]
    swap = lambda t: pltpu.einshape("abc->bac", t)
    brs = [swap(br_ref[s]) for s in range(n_rows)]
    bis = [swap(bi_ref[s]) for s in range(n_rows)]
    xs = [[_dot(f2_ref[...], jnp.concatenate([brs[s][jj], bis[s][jj]], axis=0)) for jj in range(k1blk)]
          for s in range(n_rows)]
    for s in range(n_rows):
        xr = jnp.concatenate([x[:nlen] for x in xs[s]], axis=0).astype(BF16)
        xi = jnp.concatenate([x[nlen:] for x in xs[s]], axis=0).astype(BF16)
        y = jnp.concatenate(
            [_dot(jnp.concatenate([xr[:, g * HEAD_DIM:(g + 1) * HEAD_DIM], xi[:, g * HEAD_DIM:(g + 1) * HEAD_DIM]],
                                  axis=1), wc_ref[...]) for g in range(KEYW // HEAD_DIM)], axis=1)
        y_ref[s] = swap(y.reshape(k1blk, nlen, KEYW)).astype(BF16)


def _dftctx_kernel(nlen, u_ref, f_ref, wc_ref, y_ref):
    x = _dot(f_ref[...], u_ref[...])
    xr, xi = x[:nlen].astype(BF16), x[nlen:].astype(BF16)
    for g in range(KEYW // HEAD_DIM):
        gs = slice(g * HEAD_DIM, (g + 1) * HEAD_DIM)
        y_ref[:, gs] = _dot(jnp.concatenate([xr[:, gs], xi[:, gs]], axis=1), wc_ref[...]).astype(BF16)


def _chan_mat(total_len):
    cc, sc = _dft_mats(HEAD_DIM)
    return jnp.asarray(np.concatenate([cc, sc], axis=0) / math.sqrt(total_len * HEAD_DIM), BF16)


def _fourier_lat_call(fn_cat, l_lat):
    bsz, ltot, _ = fn_cat.shape
    l1 = l_lat // CHUNK
    n2blk = 16
    k1blk = 16 if l1 % 16 == 0 else l1
    c1, s1 = _dft_mats(l1)
    f1 = jnp.asarray(np.concatenate([c1, -s1], axis=0), BF16)
    n2 = np.arange(CHUNK)[:, None, None]
    k1 = np.arange(l1)[None, :, None]
    ang = 2.0 * np.pi * (n2 * k1) / l_lat * np.ones((1, 1, HEAD_DIM))
    tc, ts = jnp.asarray(np.cos(ang), F32), jnp.asarray(np.sin(ang), F32)
    c2, s2 = _dft_mats(CHUNK)
    f2 = jnp.asarray(np.block([[c2, s2], [-s2, c2]]), BF16)
    wc = _chan_mat(l_lat)

    u_view = fn_cat.reshape(bsz, ltot // CHUNK, CHUNK, KEYW)
    bb = ROW_BATCH if bsz % ROW_BATCH == 0 else 1
    mid_spec = pl.BlockSpec((bb, n2blk, l1, KEYW), lambda b, i: (b, i, 0, 0))
    br, bi = pl.pallas_call(
        functools.partial(_fft1_kernel, n2blk),
        grid=(bsz // bb, CHUNK // n2blk),
        in_specs=[pl.BlockSpec((bb, l1, n2blk, KEYW), lambda b, i: (b, 0, i, 0)),
                  pl.BlockSpec(f1.shape, lambda b, i: (0, 0)),
                  pl.BlockSpec((n2blk, l1, HEAD_DIM), lambda b, i: (i, 0, 0)),
                  pl.BlockSpec((n2blk, l1, HEAD_DIM), lambda b, i: (i, 0, 0))],
        out_specs=[mid_spec, mid_spec],
        out_shape=[jax.ShapeDtypeStruct((bsz, CHUNK, l1, KEYW), BF16)] * 2,
        compiler_params=_cparams(2),
        name="fourier_stage1",
    )(u_view, f1, tc, ts)

    rspec = pl.BlockSpec((bb, CHUNK, k1blk, KEYW), lambda b, i: (b, 0, i, 0))
    y = pl.pallas_call(
        functools.partial(_fft2_kernel, k1blk, CHUNK),
        grid=(bsz // bb, l1 // k1blk),
        in_specs=[rspec, rspec,
                  pl.BlockSpec(f2.shape, lambda b, i: (0, 0)),
                  pl.BlockSpec(wc.shape, lambda b, i: (0, 0))],
        out_specs=rspec,
        out_shape=jax.ShapeDtypeStruct((bsz, CHUNK, l1, KEYW), BF16),
        compiler_params=_cparams(2),
        name="fourier_stage2",
    )(br, bi, f2, wc)
    return y.reshape(bsz, l_lat, KEYW)


def _fourier_ctx_call(fn_cat, l_lat):
    bsz, ltot, _ = fn_cat.shape
    lc = ltot - l_lat
    c, s = _dft_mats(lc)
    f = jnp.asarray(np.concatenate([c, -s], axis=0), BF16)
    wc = _chan_mat(lc)
    return pl.pallas_call(
        functools.partial(_dftctx_kernel, lc),
        grid=(bsz,),
        in_specs=[pl.BlockSpec((None, lc, KEYW), lambda b: (b, l_lat // lc, 0)),
                  pl.BlockSpec(f.shape, lambda b: (0, 0)),
                  pl.BlockSpec(wc.shape, lambda b: (0, 0))],
        out_specs=pl.BlockSpec((None, lc, KEYW), lambda b: (b, 0, 0)),
        out_shape=jax.ShapeDtypeStruct((bsz, lc, KEYW), BF16),
        compiler_params=_cparams(1),
        name="fourier_ctx",
    )(fn_cat, f, wc)


def _merge_kernel(nt_lat, x_ref, ctx_ref, mod_ref, of_ref, ob_ref, z_ref, sc_ref, fl_ref, fc_ref,
                  g_ref, wgate_ref, bgate_ref, og_ref, wdn_ref, wsc_ref, wfn_ref, wo_ref, o_ref):
    is_lat = pl.program_id(1) < nt_lat
    n_rows = x_ref.shape[0]
    xs = [jnp.where(is_lat, x_ref[s], ctx_ref[s]) for s in range(n_rows)]
    hs = [_modulated(xs[s], g_ref, mod_ref[s, 0:1, :], mod_ref[s, 1:2, :]).astype(BF16) for s in range(n_rows)]
    logits = [_dot(h, wgate_ref[...]) for h in hs]
    branch_in = []
    for s in range(n_rows):
        o = of_ref[s].astype(F32) + ob_ref[s].astype(F32)
        z = z_ref[s]
        dn_parts = []
        for hd in range(HEADS):
            sl = slice(hd * HEAD_DIM, (hd + 1) * HEAD_DIM)
            dn_parts.append(_rms(o[:, sl], HEAD_DIM) * og_ref[...] * _silu(z[:, sl]))
        fn = jnp.where(is_lat, fl_ref[s], fc_ref[s])
        branch_in.append((jnp.concatenate(dn_parts, axis=1).astype(BF16), sc_ref[s], fn))
    ys = [(_dot(dn, wdn_ref[...]), _dot(sc, wsc_ref[...]), _dot(fn, wfn_ref[...])) for dn, sc, fn in branch_in]
    merged = []
    for s in range(n_rows):
        gates = jax.nn.sigmoid(logits[s] + bgate_ref[...])
        y_dn, y_sc, y_fn = ys[s]
        merged.append((gates[:, :D_MODEL] * y_dn + gates[:, D_MODEL:2 * D_MODEL] * y_sc
                       + gates[:, 2 * D_MODEL:] * y_fn).astype(BF16))
    mixes = [_dot(m, wo_ref[...]) for m in merged]
    for s in range(n_rows):
        o_ref[s] = xs[s] + mod_ref[s, 2:3, :] * mixes[s]


def _merge_call(x_src, ctx_src, ctx_blk, mod_sel, n1g, o_f, o_b, z, sc, fn_lat, fn_ctx, wgate, bgate, og, wdn, wsc,
                wfn, wo, nt_lat, nt_run):
    bsz, _, d = x_src.shape
    ltot = (nt_lat + 1) * TILE
    bb = ROW_BATCH if bsz % ROW_BATCH == 0 else 1
    tok = lambda w: pl.BlockSpec((bb, TILE, w), lambda b, j: (b, j, 0))
    consts = (n1g, wgate, bgate, og, wdn, wsc, wfn, wo)
    return pl.pallas_call(
        functools.partial(_merge_kernel, nt_lat),
        grid=(bsz // bb, nt_run),
        in_specs=_x_specs(bb, nt_lat, ctx_blk, d) + [_mod_spec(bb, nt_lat, d),
                  tok(KEYW), tok(KEYW), tok(KEYW), tok(KEYW),
                  pl.BlockSpec((bb, TILE, KEYW), lambda b, j: (b, jnp.minimum(j, nt_lat - 1), 0)),
                  pl.BlockSpec((bb, TILE, KEYW), lambda b, j: (b, 0, 0))] + [_const_spec(a) for a in consts],
        out_specs=tok(d),
        out_shape=jax.ShapeDtypeStruct((bsz, ltot, d), F32),
        compiler_params=_cparams(2),
        name="branch_merge",
    )(x_src, ctx_src, mod_sel, o_f, o_b, z, sc, fn_lat, fn_ctx, *consts)


def _ffn_kernel(final, x_ref, mod_ref, g_ref, win_ref, wout_ref, fg_ref, o_ref):
    n_rows = x_ref.shape[0]
    hs = [_modulated(x_ref[s], g_ref, mod_ref[s, 3:4, :], mod_ref[s, 4:5, :]).astype(BF16) for s in range(n_rows)]
    abs_ = [_dot(h, win_ref[...]) for h in hs]
    acts = [(_silu(ab[:, :D_FF]) * ab[:, D_FF:]).astype(BF16) for ab in abs_]
    downs = [_dot(act, wout_ref[...]) for act in acts]
    for s in range(n_rows):
        y = x_ref[s] + mod_ref[s, 5:6, :] * downs[s]
        if final:
            y = _rms(y, D_MODEL) * fg_ref[...]
        o_ref[s] = y


def _ffn_call(x_cat, mod_sel, n2g, win, wout, fg, nt_lat, nt_run, final):
    bsz, ltot, d = x_cat.shape
    out_len = nt_run * TILE if final else ltot
    bb = ROW_BATCH if bsz % ROW_BATCH == 0 else 1
    tok = pl.BlockSpec((bb, TILE, d), lambda b, j: (b, j, 0))
    consts = (n2g, win, wout, fg)
    return pl.pallas_call(
        functools.partial(_ffn_kernel, final),
        grid=(bsz // bb, nt_run),
        in_specs=[tok, _mod_spec(bb, nt_lat, d)] + [_const_spec(a) for a in consts],
        out_specs=tok,
        out_shape=jax.ShapeDtypeStruct((bsz, out_len, d), F32),
        compiler_params=_cparams(2),
        name="swiglu_final" if final else "swiglu",
    )(x_cat, mod_sel, *consts)


def kernel(x, c, ctx, c_ctx, w_mod, b_mod, norm1_g, w_in, dn_conv_w, dn_a_log, dn_dt_bias, dn_onorm_g, w_dn_out,
           sc_conv_w, w_sc_out, w_fn_out, w_gate, b_gate, w_o, norm2_g, w_ffn_in, w_ffn_out, final_g):
    bsz, l_lat, d = x.shape
    lc = ctx.shape[1]
    assert d == D_MODEL and lc == TILE and l_lat % TILE == 0 and w_ffn_in.shape[-1] == 2 * D_FF
    depth = w_mod.shape[0]
    nt_lat = l_lat // TILE
    nt_all = nt_lat + 1
    n_lat_chunks = l_lat // CHUNK

    rows = ((bsz + 1 + 7) // 8) * 8
    c_rows = jnp.zeros((rows, d), F32).at[:bsz].set(c).at[bsz].set(c_ctx)
    mod = _mod_call(c_rows, w_mod, b_mod).reshape(depth, rows, 6, d)
    mod_sel = jnp.stack([mod[:, :bsz], jnp.broadcast_to(mod[:, bsz:bsz + 1], (depth, bsz, 6, d))], axis=2)

    x_src, ctx_src, ctx_blk = x, ctx, 0
    row = lambda a: a.reshape(1, -1)
    lane_pad = lambda a: jnp.zeros((1, HEAD_DIM), F32).at[0, N_DH:2 * N_DH].set(a.reshape(-1))
    for layer in range(depth):
        last = layer == depth - 1
        wl = w_in[layer]
        qkv_w = 3 * KEYW
        o_z, o_ba, o_sc, o_fn = qkv_w, qkv_w + KEYW, qkv_w + KEYW + 2 * N_DH, qkv_w + KEYW + 2 * N_DH + 3 * KEYW
        wba = jnp.zeros((d, HEAD_DIM), F32).at[:, :2 * N_DH].set(wl[:, o_ba:o_sc])
        wparts = tuple(a.astype(BF16) for a in (wl[:, :o_z], wl[:, o_z:o_ba], wba, wl[:, o_sc:o_fn], wl[:, o_fn:]))
        q, k, v, z, bg, sc, fn = _inproj_call(
            x_src, ctx_src, ctx_blk, mod_sel[layer], row(norm1_g[layer]), wparts, dn_conv_w[layer], sc_conv_w[layer],
            lane_pad(dn_a_log[layer]), lane_pad(dn_dt_bias[layer]), nt_lat)
        u, w, qg, kd, attn, adec = _dnprep_call(q, k, v, bg)
        o_f, o_b = _dnscan_call(u, w, qg, kd, attn, adec, n_lat_chunks)
        fn_lat = _fourier_lat_call(fn, l_lat)
        nt_run = nt_lat if last else nt_all
        fn_ctx = jnp.zeros((bsz, lc, KEYW), BF16) if last else _fourier_ctx_call(fn, l_lat)
        x_cat = _merge_call(
            x_src, ctx_src, ctx_blk, mod_sel[layer], row(norm1_g[layer]), o_f, o_b, z, sc, fn_lat, fn_ctx,
            w_gate[layer].astype(BF16), row(b_gate[layer]), row(dn_onorm_g[layer]),
            w_dn_out[layer].astype(BF16), w_sc_out[layer].astype(BF16), w_fn_out[layer].astype(BF16),
            w_o[layer].astype(BF16), nt_lat, nt_run)
        x_cat = _ffn_call(x_cat, mod_sel[layer], row(norm2_g[layer]), w_ffn_in[layer].astype(BF16),
                          w_ffn_out[layer].astype(BF16), row(final_g), nt_lat, nt_run, last)
        x_src, ctx_src, ctx_blk = x_cat, x_cat, nt_lat
    return x_cat
```
